```python
import jax, jax.numpy as jnp
from jax import lax
import numpy as np

D_MODEL = 2048
BATCH = 2
SEQ = 4096
DEPTH = 2
DEC_BATCH = 8
DEC_SEQ = 16
PAST_LEN = 2048

CHUNK = 64
N_PAST_CHUNKS = 8
PAST_ROWS = N_PAST_CHUNKS * CHUNK
BAND = (N_PAST_CHUNKS + 1) * CHUNK
N_A_LAYERS = DEPTH // 2
N_B_LAYERS = DEPTH - N_A_LAYERS
CONV_WIDTH = 3
HEAD_DIM = 128
N_HEADS = D_MODEL // HEAD_DIM
D_ATTN = N_HEADS * HEAD_DIM
D_FF = ((8 * D_MODEL // 3 + 255) // 256) * 256
MAX_REL = 128
N_REL = 2 * MAX_REL + 1
EPS = 1e-6
FFN_RES = 0.5
NEG_INF = -1e30

kernel_name = 'yoco_shortconv_chunkband_streaming_step'


def rms_norm(x, g):
    xf = x.astype(jnp.float32)
    y = xf * lax.rsqrt(jnp.mean(xf * xf, axis=-1, keepdims=True) + EPS)
    return (y * g.astype(jnp.float32)).astype(x.dtype)


def swiglu(x, wg, wu, wd):
    return (jax.nn.silu(x @ wg) * (x @ wu)) @ wd


def short_conv_mixer(xn, w_in, conv_w, w_out, prev):
    b, c, xb = jnp.split(xn @ w_in, 3, axis=-1)
    u = c * xb
    up = jnp.concatenate([prev.astype(u.dtype), u], axis=1)
    t = u.shape[1]
    conv = conv_w[0] * up[:, 0:t]
    for tap in range(1, CONV_WIDTH):
        conv = conv + conv_w[tap] * up[:, tap:tap + t]
    return (b * conv) @ w_out, up[:, -(CONV_WIDTH - 1):]


def rel_bias_matrix(rel_bias, offset, tq, tk):
    dist = jnp.arange(tq)[:, None] - jnp.arange(tk)[None, :] + offset
    idx = jnp.clip(dist, -MAX_REL, MAX_REL) + MAX_REL
    return rel_bias[:, idx]


def attend(q, k, v, bias, valid=None):
    s = jnp.einsum('bqhd,bkhd->bhqk', q, k).astype(jnp.float32) * (HEAD_DIM ** -0.5)
    s = s + bias.astype(jnp.float32)
    if valid is not None:
        s = jnp.where(valid, s, NEG_INF)
    p = jax.nn.softmax(s, axis=-1).astype(v.dtype)
    return jnp.einsum('bhqk,bkhd->bqhd', p, v)


def chunk_band_attention_prompt(q, k, v, rel_bias):
    bsz, s, h, dh = q.shape
    n_chunks = s // CHUNK
    pad = ((0, 0), (PAST_ROWS, 0), (0, 0), (0, 0))
    kp = jnp.pad(k, pad)
    vp = jnp.pad(v, pad)
    bias = rel_bias_matrix(rel_bias, PAST_ROWS, CHUNK, BAND)

    def one_chunk(c):
        start = c * CHUNK
        qc = lax.dynamic_slice_in_dim(q, start, CHUNK, axis=1)
        kc = lax.dynamic_slice_in_dim(kp, start, BAND, axis=1)
        vc = lax.dynamic_slice_in_dim(vp, start, BAND, axis=1)
        valid = (start - PAST_ROWS + jnp.arange(BAND)) >= 0
        return attend(qc, kc, vc, bias, valid)

    out = lax.map(one_chunk, jnp.arange(n_chunks))
    return jnp.moveaxis(out, 0, 1).reshape(bsz, s, h * dh)


def chunk_band_attention_sample(q, k_new, v_new, cache_k, cache_v, rel_bias):
    bsz, t, h, dh = q.shape
    cl = cache_k.shape[1]
    k = jnp.concatenate([cache_k.astype(k_new.dtype), k_new], axis=1)
    v = jnp.concatenate([cache_v.astype(v_new.dtype), v_new], axis=1)
    bias = rel_bias_matrix(rel_bias, cl, t, cl + t)
    return attend(q, k, v, bias).reshape(bsz, t, h * dh)


def trunk(x, conv_prev, cache_k, cache_v, ffn_norm, ffn_w_gate, ffn_w_up, ffn_w_down,
          mix_norm, conv_w_in, conv_w, conv_w_out, kv_norm, w_kv, k_gain,
          w_q, q_gain, rel_bias, w_o):
    bsz, t, _ = x.shape
    h = x
    new_conv = []
    k_sh = None
    v_sh = None
    for layer in range(DEPTH):
        h = h + FFN_RES * swiglu(rms_norm(h, ffn_norm[layer, 0]), ffn_w_gate[layer, 0],
                                 ffn_w_up[layer, 0], ffn_w_down[layer, 0])
        hn = rms_norm(h, mix_norm[layer])
        if layer < N_A_LAYERS:
            y, st = short_conv_mixer(hn, conv_w_in[layer], conv_w[layer], conv_w_out[layer],
                                     conv_prev[layer])
            new_conv.append(st)
        else:
            j = layer - N_A_LAYERS
            q = rms_norm((hn @ w_q[j]).reshape(bsz, t, N_HEADS, HEAD_DIM), q_gain[j])
            if cache_k is None:
                att = chunk_band_attention_prompt(q, k_sh, v_sh, rel_bias[j])
            else:
                att = chunk_band_attention_sample(q, k_sh, v_sh, cache_k, cache_v, rel_bias[j])
            y = att @ w_o[j]
        h = h + y
        h = h + FFN_RES * swiglu(rms_norm(h, ffn_norm[layer, 1]), ffn_w_gate[layer, 1],
                                 ffn_w_up[layer, 1], ffn_w_down[layer, 1])
        if layer == N_A_LAYERS - 1:
            kv = rms_norm(h, kv_norm) @ w_kv
            k_flat, v_flat = jnp.split(kv, 2, axis=-1)
            k_sh = rms_norm(k_flat.reshape(bsz, t, N_HEADS, HEAD_DIM), k_gain)
            v_sh = v_flat.reshape(bsz, t, N_HEADS, HEAD_DIM)
    return h, jnp.stack(new_conv), k_sh, v_sh


def setup_inputs(seed: int = 0) -> dict:
    key = jax.random.key(seed)
    ks = jax.random.split(key, 24)
    f32 = jnp.float32

    def nrm(k, shape, scale):
        return scale * jax.random.normal(k, shape, f32)

    kv_rows = min(PAST_ROWS, PAST_LEN)
    return {
        'x_prompt': nrm(ks[0], (BATCH, SEQ, D_MODEL), 1.0),
        'x_sample': nrm(ks[1], (DEC_BATCH, DEC_SEQ, D_MODEL), 1.0),
        'state_conv': nrm(ks[2], (N_A_LAYERS, DEC_BATCH, CONV_WIDTH - 1, D_MODEL), 1.0),
        'cache_k': nrm(ks[3], (DEC_BATCH, kv_rows, N_HEADS, HEAD_DIM), 1.0),
        'cache_v': nrm(ks[4], (DEC_BATCH, kv_rows, N_HEADS, HEAD_DIM), 1.0),
        'ffn_norm': 1.0 + nrm(ks[5], (DEPTH, 2, D_MODEL), 0.02),
        'ffn_w_gate': nrm(ks[6], (DEPTH, 2, D_MODEL, D_FF), D_MODEL ** -0.5),
        'ffn_w_up': nrm(ks[7], (DEPTH, 2, D_MODEL, D_FF), D_MODEL ** -0.5),
        'ffn_w_down': nrm(ks[8], (DEPTH, 2, D_FF, D_MODEL), D_FF ** -0.5),
        'mix_norm': 1.0 + nrm(ks[9], (DEPTH, D_MODEL), 0.02),
        'conv_w_in': nrm(ks[10], (N_A_LAYERS, D_MODEL, 3 * D_MODEL), D_MODEL ** -0.5),
        'conv_w': nrm(ks[11], (N_A_LAYERS, CONV_WIDTH, D_MODEL), CONV_WIDTH ** -0.5),
        'conv_w_out': nrm(ks[12], (N_A_LAYERS, D_MODEL, D_MODEL), D_MODEL ** -0.5),
        'kv_norm': 1.0 + nrm(ks[13], (D_MODEL,), 0.02),
        'w_kv': nrm(ks[14], (D_MODEL, 2 * D_ATTN), D_MODEL ** -0.5),
        'k_gain': 1.0 + nrm(ks[15], (HEAD_DIM,), 0.02),
        'w_q': nrm(ks[16], (N_B_LAYERS, D_MODEL, D_ATTN), D_MODEL ** -0.5),
        'q_gain': 1.0 + nrm(ks[17], (N_B_LAYERS, HEAD_DIM), 0.02),
        'rel_bias': nrm(ks[18], (N_B_LAYERS, N_HEADS, N_REL), 0.2),
        'w_o': nrm(ks[19], (N_B_LAYERS, D_ATTN, D_MODEL), D_ATTN ** -0.5),
    }


def reference(x_prompt, x_sample, state_conv, cache_k, cache_v, ffn_norm, ffn_w_gate, ffn_w_up,
              ffn_w_down, mix_norm, conv_w_in, conv_w, conv_w_out, kv_norm, w_kv, k_gain,
              w_q, q_gain, rel_bias, w_o):
    conv_zero = jnp.zeros((N_A_LAYERS, x_prompt.shape[0], CONV_WIDTH - 1, x_prompt.shape[2]),
                          x_prompt.dtype)
    y_prompt, conv_p, k_p, v_p = trunk(
        x_prompt, conv_zero, None, None, ffn_norm, ffn_w_gate, ffn_w_up, ffn_w_down,
        mix_norm, conv_w_in, conv_w, conv_w_out, kv_norm, w_kv, k_gain,
        w_q, q_gain, rel_bias, w_o)
    y_sample, conv_s, k_s, v_s = trunk(
        x_sample, state_conv, cache_k, cache_v, ffn_norm, ffn_w_gate, ffn_w_up, ffn_w_down,
        mix_norm, conv_w_in, conv_w, conv_w_out, kv_norm, w_kv, k_gain,
        w_q, q_gain, rel_bias, w_o)
    keep = min(PAST_ROWS, x_prompt.shape[1])
    return (y_prompt, y_sample, conv_p, k_p[:, -keep:], v_p[:, -keep:], conv_s, k_s, v_s)
```

```python
import functools

import jax
import jax.numpy as jnp
from jax import lax
from jax.experimental import pallas as pl
from jax.experimental.pallas import tpu as pltpu

F32 = jnp.float32
BF16 = jnp.bfloat16

D_MODEL = 2048
HEAD_DIM = 128
N_HEADS = D_MODEL // HEAD_DIM
CHUNK = 64
N_PAST_CHUNKS = 8
PAST_ROWS = N_PAST_CHUNKS * CHUNK
MAX_REL = 128
CONV_WIDTH = 3
EPS = 1e-6
FFN_RES = 0.5
NEG_INF = -1e30
ATTN_SCALE = HEAD_DIM ** -0.5

LANES = 128
V7X_VMEM_BYTES = 64 * 1024 * 1024
VMEM_LIMIT_BYTES = 56 * 1024 * 1024

FFN_COL_TILE = 512
MIX_COL_TILE = 512
KV_COL_TILE = 1024
ATTN_Q_TILE = 512
ATTN_SUB = 256
ATTN_WIN = ATTN_SUB + PAST_ROWS


def _params(semantics):
    return pltpu.CompilerParams(dimension_semantics=semantics,
                                vmem_limit_bytes=VMEM_LIMIT_BYTES)


def _rms_rows(x, g):
    ms = jnp.mean(x * x, axis=-1, keepdims=True)
    return (x * lax.rsqrt(ms + EPS)) * g


def _ffn_kernel(h_ref, g_ref, wg_ref, wu_ref, wd_ref, out_ref, xn_ref):
    f = pl.program_id(1)

    @pl.when(f == 0)
    def _():
        xn_ref[...] = _rms_rows(h_ref[...], g_ref[...]).astype(BF16)

    xn = xn_ref[...]
    gate = jnp.dot(xn, wg_ref[...], preferred_element_type=F32)
    up = jnp.dot(xn, wu_ref[...], preferred_element_type=F32)
    act = (gate * jax.nn.sigmoid(gate) * up).astype(BF16)
    y = FFN_RES * jnp.dot(act, wd_ref[...], preferred_element_type=F32)

    @pl.when(f == 0)
    def _():
        out_ref[...] = h_ref[...] + y

    @pl.when(f != 0)
    def _():
        out_ref[...] += y


def _ffn(h, norm_g, wg, wu, wd, layer, slot, tm):
    m, d = h.shape
    d_ff = wg.shape[-1]
    tf = FFN_COL_TILE
    grid = (m // tm, d_ff // tf)
    return pl.pallas_call(
        _ffn_kernel,
        out_shape=jax.ShapeDtypeStruct((m, d), F32),
        grid=grid,
        in_specs=[
            pl.BlockSpec((tm, d), lambda i, f: (i, 0)),
            pl.BlockSpec((None, None, 1, d), lambda i, f: (layer, slot, 0, 0)),
            pl.BlockSpec((None, None, d, tf), lambda i, f: (layer, slot, 0, f)),
            pl.BlockSpec((None, None, d, tf), lambda i, f: (layer, slot, 0, f)),
            pl.BlockSpec((None, None, tf, d), lambda i, f: (layer, slot, f, 0)),
        ],
        out_specs=pl.BlockSpec((tm, d), lambda i, f: (i, 0)),
        scratch_shapes=[pltpu.VMEM((tm, d), BF16)],
        compiler_params=_params(("parallel", "arbitrary")),
        name="ffn",
    )(h, norm_g, wg, wu, wd)


def _mixer_kernel(h_ref, g_ref, wb_ref, wc_ref, wx_ref, cw_ref, wo_ref, p0_ref, p1_ref,
                  out_ref, st_ref, xn_ref, carry_ref, *, seq_len, tm):
    i = pl.program_id(0)
    j = pl.program_id(1)

    @pl.when(j == 0)
    def _():
        xn_ref[...] = _rms_rows(h_ref[...], g_ref[...]).astype(BF16)

    xn = xn_ref[...]
    b = jnp.dot(xn, wb_ref[...], preferred_element_type=F32)
    c = jnp.dot(xn, wc_ref[...], preferred_element_type=F32)
    x = jnp.dot(xn, wx_ref[...], preferred_element_type=F32)
    u = c * x
    tn = u.shape[1]
    row = lax.broadcasted_iota(jnp.int32, (tm, tn), 0)
    if seq_len >= tm:
        tiles_per_seq = seq_len // tm
        @pl.when((i % tiles_per_seq) == 0)
        def _():
            carry_ref[j] = p0_ref[0]

        prev = carry_ref[j]
        p0 = prev[0:1, :]
        p1 = prev[1:2, :]
        pos = row
        carry_ref[j] = u[tm - 2:tm, :]
        st_ref[0] = u[tm - 2:tm, :]
    else:
        p0 = p0_ref[...]
        p1 = p1_ref[...]
        pos = row % seq_len
        u3 = u.reshape(tm // seq_len, seq_len, tn)
        st_ref[...] = u3[:, seq_len - 2:seq_len, :]
    um1 = jnp.where(pos == 0, p1, pltpu.roll(u, 1, 0))
    um2 = jnp.where(pos == 0, p0, jnp.where(pos == 1, p1, pltpu.roll(u, 2, 0)))
    cw = cw_ref[...]
    conv = cw[0:1, :] * um2 + cw[1:2, :] * um1 + cw[2:3, :] * u
    v = (b * conv).astype(BF16)
    y = jnp.dot(v, wo_ref[...], preferred_element_type=F32)

    @pl.when(j == 0)
    def _():
        out_ref[...] = h_ref[...] + y

    @pl.when(j != 0)
    def _():
        out_ref[...] += y


def _mixer(h, norm_g, w_in, conv_w, w_out, state, layer, seq_len, tm):
    m, d = h.shape
    n_seq = m // seq_len
    tn = MIX_COL_TILE
    n_j = d // tn
    n_i = m // tm
    if seq_len >= tm:
        tiles_per_seq = seq_len // tm
        p0, p1 = state, state
        p_spec = pl.BlockSpec((1, CONV_WIDTH - 1, tn), lambda i, j: (i // tiles_per_seq, 0, j))
        st_shape = jax.ShapeDtypeStruct((n_i, CONV_WIDTH - 1, d), F32)
        st_spec = pl.BlockSpec((1, CONV_WIDTH - 1, tn), lambda i, j: (i, 0, j))
    else:
        assert tm == m and tm % seq_len == 0
        p0 = jnp.repeat(state[:, 0], seq_len, axis=0)
        p1 = jnp.repeat(state[:, 1], seq_len, axis=0)
        p_spec = pl.BlockSpec((tm, tn), lambda i, j: (i, j))
        st_shape = jax.ShapeDtypeStruct((n_seq, CONV_WIDTH - 1, d), F32)
        st_spec = pl.BlockSpec((n_seq, CONV_WIDTH - 1, tn), lambda i, j: (0, 0, j))
    out, st = pl.pallas_call(
        functools.partial(_mixer_kernel, seq_len=seq_len, tm=tm),
        out_shape=(jax.ShapeDtypeStruct((m, d), F32), st_shape),
        grid=(n_i, n_j),
        in_specs=[
            pl.BlockSpec((tm, d), lambda i, j: (i, 0)),
            pl.BlockSpec((None, 1, d), lambda i, j: (layer, 0, 0)),
            pl.BlockSpec((None, d, tn), lambda i, j: (layer, 0, j)),
            pl.BlockSpec((None, d, tn), lambda i, j: (layer, 0, n_j + j)),
            pl.BlockSpec((None, d, tn), lambda i, j: (layer, 0, 2 * n_j + j)),
            pl.BlockSpec((None, CONV_WIDTH, tn), lambda i, j: (layer, 0, j)),
            pl.BlockSpec((None, tn, d), lambda i, j: (layer, j, 0)),
            p_spec,
            p_spec,
        ],
        out_specs=(pl.BlockSpec((tm, d), lambda i, j: (i, 0)), st_spec),
        scratch_shapes=[pltpu.VMEM((tm, d), BF16),
                        pltpu.VMEM((n_j, CONV_WIDTH - 1, tn), F32)],
        compiler_params=_params(("arbitrary", "arbitrary")),
        name="conv_mixer",
    )(h, norm_g, w_in, w_in, w_in, conv_w, w_out, p0, p1)
    if seq_len >= tm:
        tiles_per_seq = seq_len // tm
        st = st[tiles_per_seq - 1::tiles_per_seq]
    return out, st


def _kv_kernel(h_ref, g_ref, wk_ref, wv_ref, kg_ref, k32_ref, v32_ref, k16_ref, v16_ref, xn_ref):
    j = pl.program_id(1)

    @pl.when(j == 0)
    def _():
        xn_ref[...] = _rms_rows(h_ref[...], g_ref[...]).astype(BF16)

    xn = xn_ref[...]
    k = jnp.dot(xn, wk_ref[...], preferred_element_type=F32)
    v = jnp.dot(xn, wv_ref[...], preferred_element_type=F32)
    kg = kg_ref[...]
    v32_ref[...] = v
    for hd in range(k.shape[1] // HEAD_DIM):
        cols = slice(hd * HEAD_DIM, (hd + 1) * HEAD_DIM)
        kh = _rms_rows(k[:, cols], kg)
        k32_ref[:, cols] = kh
        k16_ref[hd] = kh.astype(BF16)
        v16_ref[hd] = v[:, cols].astype(BF16)


def _kv_proj(h, norm_g, w_kv, k_gain, tm):
    m, d = h.shape
    tn = KV_COL_TILE
    n_j = d // tn
    hpb = tn // HEAD_DIM
    return pl.pallas_call(
        _kv_kernel,
        out_shape=(jax.ShapeDtypeStruct((m, d), F32), jax.ShapeDtypeStruct((m, d), F32),
                   jax.ShapeDtypeStruct((N_HEADS, m, HEAD_DIM), BF16),
                   jax.ShapeDtypeStruct((N_HEADS, m, HEAD_DIM), BF16)),
        grid=(m // tm, n_j),
        in_specs=[
            pl.BlockSpec((tm, d), lambda i, j: (i, 0)),
            pl.BlockSpec((1, d), lambda i, j: (0, 0)),
            pl.BlockSpec((d, tn), lambda i, j: (0, j)),
            pl.BlockSpec((d, tn), lambda i, j: (0, n_j + j)),
            pl.BlockSpec((1, HEAD_DIM), lambda i, j: (0, 0)),
        ],
        out_specs=(pl.BlockSpec((tm, tn), lambda i, j: (i, j)),
                   pl.BlockSpec((tm, tn), lambda i, j: (i, j)),
                   pl.BlockSpec((hpb, tm, HEAD_DIM), lambda i, j: (j, i, 0)),
                   pl.BlockSpec((hpb, tm, HEAD_DIM), lambda i, j: (j, i, 0))),
        scratch_shapes=[pltpu.VMEM((tm, d), BF16)],
        compiler_params=_params(("parallel", "arbitrary")),
        name="kv_proj",
    )(h, norm_g, w_kv, w_kv, k_gain)


def _q_kernel(h_ref, g_ref, wq_ref, qg_ref, q16_ref):
    xn = _rms_rows(h_ref[...], g_ref[...]).astype(BF16)
    q = jnp.dot(xn, wq_ref[...], preferred_element_type=F32)
    qg = qg_ref[...]
    for hd in range(N_HEADS):
        cols = slice(hd * HEAD_DIM, (hd + 1) * HEAD_DIM)
        q16_ref[hd] = _rms_rows(q[:, cols], qg).astype(BF16)


def _q_proj(h, norm_g, w_q, q_gain, layer, j, tm):
    m, d = h.shape
    return pl.pallas_call(
        _q_kernel,
        out_shape=jax.ShapeDtypeStruct((N_HEADS, m, HEAD_DIM), BF16),
        grid=(m // tm,),
        in_specs=[
            pl.BlockSpec((tm, d), lambda i: (i, 0)),
            pl.BlockSpec((None, 1, d), lambda i: (layer, 0, 0)),
            pl.BlockSpec((None, d, d), lambda i: (j, 0, 0)),
            pl.BlockSpec((None, 1, HEAD_DIM), lambda i: (j, 0, 0)),
        ],
        out_specs=pl.BlockSpec((N_HEADS, tm, HEAD_DIM), lambda i: (0, i, 0)),
        compiler_params=_params(("parallel",)),
        name="q_proj",
    )(h, norm_g, w_q, q_gain)


def _o_kernel(h_ref, att_ref, wo_ref, out_ref, att2d_ref):
    for hd in range(N_HEADS):
        att2d_ref[:, hd * HEAD_DIM:(hd + 1) * HEAD_DIM] = att_ref[hd]
    out_ref[...] = h_ref[...] + jnp.dot(att2d_ref[...], wo_ref[...], preferred_element_type=F32)


def _o_proj(h, att16, w_o, j, tm):
    m, d = h.shape
    return pl.pallas_call(
        _o_kernel,
        out_shape=jax.ShapeDtypeStruct((m, d), F32),
        grid=(m // tm,),
        in_specs=[
            pl.BlockSpec((tm, d), lambda i: (i, 0)),
            pl.BlockSpec((N_HEADS, tm, HEAD_DIM), lambda i: (0, i, 0)),
            pl.BlockSpec((None, d, d), lambda i: (j, 0, 0)),
        ],
        out_specs=pl.BlockSpec((tm, d), lambda i: (i, 0)),
        scratch_shapes=[pltpu.VMEM((tm, d), BF16)],
        compiler_params=_params(("parallel",)),
        name="o_proj",
    )(h, att16, w_o)


def _softmax_rows(s):
    m = jnp.max(s, axis=-1, keepdims=True)
    e = jnp.exp(s - m)
    return e * (1.0 / jnp.sum(e, axis=-1, keepdims=True))


def _attn_prompt_kernel(q_ref, kp_ref, kc_ref, vp_ref, vc_ref, bias_ref, o_ref,
                        kall_ref, vall_ref):
    i = pl.program_id(1)
    kall_ref[:, 0:ATTN_Q_TILE, :] = kp_ref[...]
    kall_ref[:, ATTN_Q_TILE:2 * ATTN_Q_TILE, :] = kc_ref[...]
    vall_ref[:, 0:ATTN_Q_TILE, :] = vp_ref[...]
    vall_ref[:, ATTN_Q_TILE:2 * ATTN_Q_TILE, :] = vc_ref[...]
    n_before_start = jnp.where(i == 0, ATTN_Q_TILE, 0)
    col = lax.broadcasted_iota(jnp.int32, (ATTN_SUB, ATTN_WIN), 1)

    def head_body(h, carry):
        for s in range(ATTN_Q_TILE // ATTN_SUB):
            lo = s * ATTN_SUB
            q = q_ref[h, lo:lo + ATTN_SUB, :]
            kw = kall_ref[h, lo:lo + ATTN_WIN, :]
            vw = vall_ref[h, lo:lo + ATTN_WIN, :]
            sc = lax.dot_general(q, kw, (((1,), (1,)), ((), ())), preferred_element_type=F32)
            sc = sc * ATTN_SCALE + bias_ref[h]
            sc = jnp.where(col < n_before_start - lo, NEG_INF, sc)
            p = _softmax_rows(sc).astype(BF16)
            o = jnp.dot(p, vw, preferred_element_type=F32)
            o_ref[h, lo:lo + ATTN_SUB, :] = o.astype(BF16)
        return carry

    lax.fori_loop(0, N_HEADS, head_body, 0)


def _prompt_bias(rel_bias):
    a = jnp.arange(ATTN_SUB)[:, None]
    w = jnp.arange(ATTN_WIN)[None, :]
    idx = jnp.clip(a - w + PAST_ROWS, -MAX_REL, MAX_REL) + MAX_REL
    qc = a // CHUNK
    kc = w // CHUNK
    band = (kc >= qc) & (kc <= qc + N_PAST_CHUNKS)
    return jnp.where(band[None], rel_bias[:, idx], NEG_INF).astype(F32)


def _attn_prompt(q16, k16, v16, rel_bias, n_seq, seq_len):
    tiles = seq_len // ATTN_Q_TILE
    blk = (N_HEADS, ATTN_Q_TILE, HEAD_DIM)
    cur = pl.BlockSpec(blk, lambda b, i: (0, b * tiles + i, 0))
    past = pl.BlockSpec(blk, lambda b, i: (0, b * tiles + jnp.maximum(i - 1, 0), 0))
    bias = _prompt_bias(rel_bias)
    return pl.pallas_call(
        _attn_prompt_kernel,
        out_shape=jax.ShapeDtypeStruct(q16.shape, BF16),
        grid=(n_seq, tiles),
        in_specs=[cur, past, cur, past, cur,
                  pl.BlockSpec(bias.shape, lambda b, i: (0, 0, 0))],
        out_specs=cur,
        scratch_shapes=[pltpu.VMEM((N_HEADS, 2 * ATTN_Q_TILE, HEAD_DIM), BF16),
                        pltpu.VMEM((N_HEADS, 2 * ATTN_Q_TILE, HEAD_DIM), BF16)],
        compiler_params=_params(("parallel", "arbitrary")),
        name="attn_prompt",
    )(q16, k16, k16, v16, v16, bias)


def _attn_sample_kernel(q_ref, kn_ref, vn_ref, ck_ref, cv_ref, bc_ref, bn_ref, o_ref):
    for h in range(N_HEADS):
        cols = slice(h * HEAD_DIM, (h + 1) * HEAD_DIM)
        q = q_ref[h]
        kc = ck_ref[0, :, cols].astype(BF16)
        vc = cv_ref[0, :, cols].astype(BF16)
        dims = (((1,), (1,)), ((), ()))
        sc = lax.dot_general(q, kc, dims, preferred_element_type=F32) * ATTN_SCALE + bc_ref[h]
        sn = lax.dot_general(q, kn_ref[h], dims, preferred_element_type=F32) * ATTN_SCALE + bn_ref[h]
        m = jnp.maximum(jnp.max(sc, axis=-1, keepdims=True), jnp.max(sn, axis=-1, keepdims=True))
        ec = jnp.exp(sc - m)
        en = jnp.exp(sn - m)
        inv = 1.0 / (jnp.sum(ec, axis=-1, keepdims=True) + jnp.sum(en, axis=-1, keepdims=True))
        o = jnp.dot((ec * inv).astype(BF16), vc, preferred_element_type=F32)
        o = o + jnp.dot((en * inv).astype(BF16), vn_ref[h], preferred_element_type=F32)
        o_ref[h] = o.astype(BF16)


def _attn_sample(q16, k16, v16, cache_k, cache_v, rel_bias, n_seq, seq_len):
    cache_len = cache_k.shape[1]
    ck = cache_k.reshape(n_seq, cache_len, D_MODEL)
    cv = cache_v.reshape(n_seq, cache_len, D_MODEL)
    a = jnp.arange(seq_len)[:, None]
    w = jnp.arange(cache_len + seq_len)[None, :]
    idx = jnp.clip(a - w + cache_len, -MAX_REL, MAX_REL) + MAX_REL
    bias = rel_bias[:, idx].astype(F32)
    bias_c = bias[:, :, :cache_len]
    bias_n = bias[:, :, cache_len:]
    new = pl.BlockSpec((N_HEADS, seq_len, HEAD_DIM), lambda b: (0, b, 0))
    cache = pl.BlockSpec((1, cache_len, D_MODEL), lambda b: (b, 0, 0))
    return pl.pallas_call(
        _attn_sample_kernel,
        out_shape=jax.ShapeDtypeStruct(q16.shape, BF16),
        grid=(n_seq,),
        in_specs=[new, new, new, cache, cache,
                  pl.BlockSpec(bias_c.shape, lambda b: (0, 0, 0)),
                  pl.BlockSpec(bias_n.shape, lambda b: (0, 0, 0))],
        out_specs=new,
        compiler_params=_params(("parallel",)),
        name="attn_sample",
    )(q16, k16, v16, ck, cv, bias_c, bias_n)


def _trunk(x, conv_state, cache_k, cache_v, p, tm):
    n_seq, seq_len, d = x.shape
    h = x.reshape(n_seq * seq_len, d)
    ffn = lambda h, layer, slot: _ffn(h, p["ffn_norm"], p["wg"], p["wu"], p["wd"], layer, slot, tm)

    h = ffn(h, 0, 0)
    h, new_state = _mixer(h, p["mix_norm"], p["conv_w_in"], p["conv_w"], p["conv_w_out"],
                          conv_state[0], 0, seq_len, tm)
    h = ffn(h, 0, 1)
    k32, v32, k16, v16 = _kv_proj(h, p["kv_norm"], p["w_kv"], p["k_gain"], tm)

    h = ffn(h, 1, 0)
    q16 = _q_proj(h, p["mix_norm"], p["w_q"], p["q_gain"], 1, 0, tm)
    if cache_k is None:
        att16 = _attn_prompt(q16, k16, v16, p["rel_bias"][0], n_seq, seq_len)
    else:
        att16 = _attn_sample(q16, k16, v16, cache_k, cache_v, p["rel_bias"][0], n_seq, seq_len)
    h = _o_proj(h, att16, p["w_o"], 0, tm)
    h = ffn(h, 1, 1)
    return (h.reshape(n_seq, seq_len, d), new_state[None],
            k32.reshape(n_seq, seq_len, N_HEADS, HEAD_DIM),
            v32.reshape(n_seq, seq_len, N_HEADS, HEAD_DIM))


def kernel(x_prompt, x_sample, state_conv, cache_k, cache_v, ffn_norm, ffn_w_gate, ffn_w_up,
           ffn_w_down, mix_norm, conv_w_in, conv_w, conv_w_out, kv_norm, w_kv, k_gain,
           w_q, q_gain, rel_bias, w_o):
    depth = ffn_norm.shape[0]
    assert depth == 2 and conv_w_in.shape[0] == 1 and w_q.shape[0] == 1
    p = {
        "ffn_norm": ffn_norm.reshape(depth, 2, 1, D_MODEL),
        "wg": ffn_w_gate.astype(BF16),
        "wu": ffn_w_up.astype(BF16),
        "wd": ffn_w_down.astype(BF16),
        "mix_norm": mix_norm.reshape(depth, 1, D_MODEL),
        "conv_w_in": conv_w_in.astype(BF16),
        "conv_w": conv_w,
        "conv_w_out": conv_w_out.astype(BF16),
        "kv_norm": kv_norm.reshape(1, D_MODEL),
        "w_kv": w_kv.astype(BF16),
        "k_gain": k_gain.reshape(1, HEAD_DIM),
        "w_q": w_q.astype(BF16),
        "q_gain": q_gain.reshape(-1, 1, HEAD_DIM),
        "rel_bias": rel_bias,
        "w_o": w_o.astype(BF16),
    }
    n_p, t_p, _ = x_prompt.shape
    n_s, t_s, _ = x_sample.shape
    conv_zero = jnp.zeros((1, n_p, CONV_WIDTH - 1, D_MODEL), F32)
    y_p, conv_p, k_p, v_p = _trunk(x_prompt, conv_zero, None, None, p, tm=512)
    y_s, conv_s, k_s, v_s = _trunk(x_sample, state_conv, cache_k, cache_v, p, tm=n_s * t_s)
    keep = min(PAST_ROWS, t_p)
    return (y_p, y_s, conv_p, k_p[:, -keep:], v_p[:, -keep:], conv_s, k_s, v_s)
```

```python
import functools

import jax
import jax.numpy as jnp
from jax import lax
from jax.experimental import pallas as pl
from jax.experimental.pallas import tpu as pltpu

F32 = jnp.float32
BF16 = jnp.bfloat16

D_MODEL = 2048
HEAD_DIM = 128
N_HEADS = D_MODEL // HEAD_DIM
CHUNK = 64
N_PAST_CHUNKS = 8
PAST_ROWS = N_PAST_CHUNKS * CHUNK
MAX_REL = 128
CONV_WIDTH = 3
EPS = 1e-6
FFN_RES = 0.5
NEG_INF = -1e30
ATTN_SCALE = HEAD_DIM ** -0.5

LANES = 128
V7X_VMEM_BYTES = 64 * 1024 * 1024
VMEM_LIMIT_BYTES = 56 * 1024 * 1024

PROMPT_ROW_TILE = 512
PROMPT_FFN_ROW_TILE = 1024
FFN_COL_TILE = 512
MIX_COL_TILE = 512
KV_COL_TILE = 1024
ATTN_Q_TILE = 512
ATTN_SUB = 256
ATTN_WIN = ATTN_SUB + PAST_ROWS


def _params(semantics):
    return pltpu.CompilerParams(dimension_semantics=semantics,
                                vmem_limit_bytes=VMEM_LIMIT_BYTES)


def _rms_rows(x, g):
    ms = jnp.mean(x * x, axis=-1, keepdims=True)
    return (x * lax.rsqrt(ms + EPS)) * g


def _ffn_kernel(h_ref, g_ref, wg_ref, wu_ref, wd_ref, out_ref, xn_ref):
    f = pl.program_id(1)

    @pl.when(f == 0)
    def _():
        h = h_ref[...]
        xn_ref[...] = _rms_rows(h, g_ref[...]).astype(BF16)
        out_ref[...] = h

    xn = xn_ref[...]
    gate = jnp.dot(xn, wg_ref[...], preferred_element_type=F32)
    up = jnp.dot(xn, wu_ref[...], preferred_element_type=F32)
    act = (gate * jax.nn.sigmoid(gate) * (FFN_RES * up)).astype(BF16)
    out_ref[...] += jnp.dot(act, wd_ref[...], preferred_element_type=F32)


def _ffn(h, norm_g, wg, wu, wd, layer, slot, tm):
    m, d = h.shape
    d_ff = wg.shape[-1]
    tf = FFN_COL_TILE
    grid = (m // tm, d_ff // tf)
    return pl.pallas_call(
        _ffn_kernel,
        out_shape=jax.ShapeDtypeStruct((m, d), F32),
        grid=grid,
        in_specs=[
            pl.BlockSpec((tm, d), lambda i, f: (i, 0)),
            pl.BlockSpec((None, None, 1, d), lambda i, f: (layer, slot, 0, 0)),
            pl.BlockSpec((None, None, d, tf), lambda i, f: (layer, slot, 0, f)),
            pl.BlockSpec((None, None, d, tf), lambda i, f: (layer, slot, 0, f)),
            pl.BlockSpec((None, None, tf, d), lambda i, f: (layer, slot, f, 0)),
        ],
        out_specs=pl.BlockSpec((tm, d), lambda i, f: (i, 0)),
        scratch_shapes=[pltpu.VMEM((tm, d), BF16)],
        compiler_params=_params(("parallel", "arbitrary")),
        name="ffn",
    )(h, norm_g, wg, wu, wd)


def _mixer_kernel(h_ref, g_ref, wb_ref, wc_ref, wx_ref, cw_ref, wo_ref, p0_ref, p1_ref,
                  out_ref, st_ref, xn_ref, carry_ref, *, seq_len, tm):
    i = pl.program_id(0)
    j = pl.program_id(1)

    @pl.when(j == 0)
    def _():
        h = h_ref[...]
        xn_ref[...] = _rms_rows(h, g_ref[...]).astype(BF16)
        out_ref[...] = h

    if seq_len >= tm:
        @pl.when((i % (seq_len // tm)) == 0)
        def _():
            carry_ref[j] = p0_ref[0]

    xn = xn_ref[...]
    b = jnp.dot(xn, wb_ref[...], preferred_element_type=F32)
    c = jnp.dot(xn, wc_ref[...], preferred_element_type=F32)
    x = jnp.dot(xn, wx_ref[...], preferred_element_type=F32)
    u = c * x
    tn = u.shape[1]
    row = lax.broadcasted_iota(jnp.int32, (tm, tn), 0)
    if seq_len >= tm:
        prev = carry_ref[j]
        p0 = prev[0:1, :]
        p1 = prev[1:2, :]
        pos = row
        carry_ref[j] = u[tm - 2:tm, :]
        st_ref[0] = u[tm - 2:tm, :]
    else:
        p0 = p0_ref[...]
        p1 = p1_ref[...]
        pos = row % seq_len
        u3 = u.reshape(tm // seq_len, seq_len, tn)
        st_ref[...] = u3[:, seq_len - 2:seq_len, :]
    um1 = jnp.where(pos == 0, p1, pltpu.roll(u, 1, 0))
    um2 = jnp.where(pos == 0, p0, jnp.where(pos == 1, p1, pltpu.roll(u, 2, 0)))
    cw = cw_ref[...]
    conv = cw[0:1, :] * um2 + cw[1:2, :] * um1 + cw[2:3, :] * u
    v = (b * conv).astype(BF16)
    out_ref[...] += jnp.dot(v, wo_ref[...], preferred_element_type=F32)


def _mixer(h, norm_g, w_in, conv_w, w_out, state, layer, seq_len, tm):
    m, d = h.shape
    n_seq = m // seq_len
    tn = MIX_COL_TILE
    n_j = d // tn
    n_i = m // tm
    if seq_len >= tm:
        tiles_per_seq = seq_len // tm
        p0, p1 = state, state
        p_spec = pl.BlockSpec((1, CONV_WIDTH - 1, tn), lambda i, j: (i // tiles_per_seq, 0, j))
        st_shape = jax.ShapeDtypeStruct((n_i, CONV_WIDTH - 1, d), F32)
        st_spec = pl.BlockSpec((1, CONV_WIDTH - 1, tn), lambda i, j: (i, 0, j))
    else:
        assert tm == m and tm % seq_len == 0
        p0 = jnp.repeat(state[:, 0], seq_len, axis=0)
        p1 = jnp.repeat(state[:, 1], seq_len, axis=0)
        p_spec = pl.BlockSpec((tm, tn), lambda i, j: (i, j))
        st_shape = jax.ShapeDtypeStruct((n_seq, CONV_WIDTH - 1, d), F32)
        st_spec = pl.BlockSpec((n_seq, CONV_WIDTH - 1, tn), lambda i, j: (0, 0, j))
    out, st = pl.pallas_call(
        functools.partial(_mixer_kernel, seq_len=seq_len, tm=tm),
        out_shape=(jax.ShapeDtypeStruct((m, d), F32), st_shape),
        grid=(n_i, n_j),
        in_specs=[
            pl.BlockSpec((tm, d), lambda i, j: (i, 0)),
            pl.BlockSpec((None, 1, d), lambda i, j: (layer, 0, 0)),
            pl.BlockSpec((None, d, tn), lambda i, j: (layer, 0, j)),
            pl.BlockSpec((None, d, tn), lambda i, j: (layer, 0, n_j + j)),
            pl.BlockSpec((None, d, tn), lambda i, j: (layer, 0, 2 * n_j + j)),
            pl.BlockSpec((None, CONV_WIDTH, tn), lambda i, j: (layer, 0, j)),
            pl.BlockSpec((None, tn, d), lambda i, j: (layer, j, 0)),
            p_spec,
            p_spec,
        ],
        out_specs=(pl.BlockSpec((tm, d), lambda i, j: (i, 0)), st_spec),
        scratch_shapes=[pltpu.VMEM((tm, d), BF16),
                        pltpu.VMEM((n_j, CONV_WIDTH - 1, tn), F32)],
        compiler_params=_params(("arbitrary", "arbitrary")),
        name="conv_mixer",
    )(h, norm_g, w_in, w_in, w_in, conv_w, w_out, p0, p1)
    if seq_len >= tm:
        tiles_per_seq = seq_len // tm
        st = st[tiles_per_seq - 1::tiles_per_seq]
    return out, st


def _kv_kernel(h_ref, g_ref, wk_ref, wv_ref, kg_ref, k32_ref, v32_ref, k16_ref, v16_ref, xn_ref):
    j = pl.program_id(1)

    @pl.when(j == 0)
    def _():
        xn_ref[...] = _rms_rows(h_ref[...], g_ref[...]).astype(BF16)

    xn = xn_ref[...]
    k = jnp.dot(xn, wk_ref[...], preferred_element_type=F32)
    v = jnp.dot(xn, wv_ref[...], preferred_element_type=F32)
    kg = kg_ref[...]
    v32_ref[...] = v
    for hd in range(k.shape[1] // HEAD_DIM):
        cols = slice(hd * HEAD_DIM, (hd + 1) * HEAD_DIM)
        kh = _rms_rows(k[:, cols], kg)
        k32_ref[:, cols] = kh
        k16_ref[hd] = kh.astype(BF16)
        v16_ref[hd] = v[:, cols].astype(BF16)


def _kv_proj(h, norm_g, w_kv, k_gain, tm):
    m, d = h.shape
    tn = KV_COL_TILE
    n_j = d // tn
    hpb = tn // HEAD_DIM
    return pl.pallas_call(
        _kv_kernel,
        out_shape=(jax.ShapeDtypeStruct((m, d), F32), jax.ShapeDtypeStruct((m, d), F32),
                   jax.ShapeDtypeStruct((N_HEADS, m, HEAD_DIM), BF16),
                   jax.ShapeDtypeStruct((N_HEADS, m, HEAD_DIM), BF16)),
        grid=(m // tm, n_j),
        in_specs=[
            pl.BlockSpec((tm, d), lambda i, j: (i, 0)),
            pl.BlockSpec((1, d), lambda i, j: (0, 0)),
            pl.BlockSpec((d, tn), lambda i, j: (0, j)),
            pl.BlockSpec((d, tn), lambda i, j: (0, n_j + j)),
            pl.BlockSpec((1, HEAD_DIM), lambda i, j: (0, 0)),
        ],
        out_specs=(pl.BlockSpec((tm, tn), lambda i, j: (i, j)),
                   pl.BlockSpec((tm, tn), lambda i, j: (i, j)),
                   pl.BlockSpec((hpb, tm, HEAD_DIM), lambda i, j: (j, i, 0)),
                   pl.BlockSpec((hpb, tm, HEAD_DIM), lambda i, j: (j, i, 0))),
        scratch_shapes=[pltpu.VMEM((tm, d), BF16)],
        compiler_params=_params(("parallel", "arbitrary")),
        name="kv_proj",
    )(h, norm_g, w_kv, w_kv, k_gain)


def _q_kernel(h_ref, g_ref, wq_ref, qg_ref, q16_ref):
    xn = _rms_rows(h_ref[...], g_ref[...]).astype(BF16)
    q = jnp.dot(xn, wq_ref[...], preferred_element_type=F32)
    qg = qg_ref[...]
    for hd in range(N_HEADS):
        cols = slice(hd * HEAD_DIM, (hd + 1) * HEAD_DIM)
        q16_ref[hd] = _rms_rows(q[:, cols], qg).astype(BF16)


def _q_proj(h, norm_g, w_q, q_gain, layer, j, tm):
    m, d = h.shape
    return pl.pallas_call(
        _q_kernel,
        out_shape=jax.ShapeDtypeStruct((N_HEADS, m, HEAD_DIM), BF16),
        grid=(m // tm,),
        in_specs=[
            pl.BlockSpec((tm, d), lambda i: (i, 0)),
            pl.BlockSpec((None, 1, d), lambda i: (layer, 0, 0)),
            pl.BlockSpec((None, d, d), lambda i: (j, 0, 0)),
            pl.BlockSpec((None, 1, HEAD_DIM), lambda i: (j, 0, 0)),
        ],
        out_specs=pl.BlockSpec((N_HEADS, tm, HEAD_DIM), lambda i: (0, i, 0)),
        compiler_params=_params(("parallel",)),
        name="q_proj",
    )(h, norm_g, w_q, q_gain)


def _o_kernel(h_ref, att_ref, wo_ref, out_ref, att2d_ref):
    for hd in range(N_HEADS):
        att2d_ref[:, hd * HEAD_DIM:(hd + 1) * HEAD_DIM] = att_ref[hd]
    out_ref[...] = h_ref[...] + jnp.dot(att2d_ref[...], wo_ref[...], preferred_element_type=F32)


def _o_proj(h, att16, w_o, j, tm):
    m, d = h.shape
    return pl.pallas_call(
        _o_kernel,
        out_shape=jax.ShapeDtypeStruct((m, d), F32),
        grid=(m // tm,),
        in_specs=[
            pl.BlockSpec((tm, d), lambda i: (i, 0)),
            pl.BlockSpec((N_HEADS, tm, HEAD_DIM), lambda i: (0, i, 0)),
            pl.BlockSpec((None, d, d), lambda i: (j, 0, 0)),
        ],
        out_specs=pl.BlockSpec((tm, d), lambda i: (i, 0)),
        scratch_shapes=[pltpu.VMEM((tm, d), BF16)],
        compiler_params=_params(("parallel",)),
        name="o_proj",
    )(h, att16, w_o)


def _softmax_rows(s):
    m = jnp.max(s, axis=-1, keepdims=True)
    e = jnp.exp(s - m)
    return e * (1.0 / jnp.sum(e, axis=-1, keepdims=True))


def _attn_prompt_kernel(q_ref, kp_ref, kc_ref, vp_ref, vc_ref, bias_ref, o_ref,
                        kall_ref, vall_ref):
    i = pl.program_id(1)
    kall_ref[:, 0:ATTN_Q_TILE, :] = kp_ref[...]
    kall_ref[:, ATTN_Q_TILE:2 * ATTN_Q_TILE, :] = kc_ref[...]
    vall_ref[:, 0:ATTN_Q_TILE, :] = vp_ref[...]
    vall_ref[:, ATTN_Q_TILE:2 * ATTN_Q_TILE, :] = vc_ref[...]
    n_before_start = jnp.where(i == 0, ATTN_Q_TILE, 0)
    col = lax.broadcasted_iota(jnp.int32, (ATTN_SUB, ATTN_WIN), 1)

    def head_body(h, carry):
        for s in range(ATTN_Q_TILE // ATTN_SUB):
            lo = s * ATTN_SUB
            q = q_ref[h, lo:lo + ATTN_SUB, :]
            kw = kall_ref[h, lo:lo + ATTN_WIN, :]
            vw = vall_ref[h, lo:lo + ATTN_WIN, :]
            sc = lax.dot_general(q, kw, (((1,), (1,)), ((), ())), preferred_element_type=F32)
            sc = sc * ATTN_SCALE + bias_ref[h]
            sc = jnp.where(col < n_before_start - lo, NEG_INF, sc)
            p = _softmax_rows(sc).astype(BF16)
            o = jnp.dot(p, vw, preferred_element_type=F32)
            o_ref[h, lo:lo + ATTN_SUB, :] = o.astype(BF16)
        return carry

    lax.fori_loop(0, N_HEADS, head_body, 0)


def _prompt_bias(rel_bias):
    period = pl.next_power_of_2(ATTN_SUB + ATTN_WIN - 1)
    j = jnp.arange(period)
    k = jnp.where(j < ATTN_WIN, j, j - period)
    idx = jnp.clip(PAST_ROWS - k, -MAX_REL, MAX_REL) + MAX_REL
    vec = rel_bias[:, idx].astype(F32)
    flat = jnp.tile(vec, (1, ATTN_SUB))[:, :ATTN_SUB * (period - 1)]
    toeplitz = flat.reshape(N_HEADS, ATTN_SUB, period - 1)[:, :, :ATTN_WIN]
    qc = jnp.arange(ATTN_SUB)[:, None] // CHUNK
    kc = jnp.arange(ATTN_WIN)[None, :] // CHUNK
    band = (kc >= qc) & (kc <= qc + N_PAST_CHUNKS)
    return jnp.where(band[None], toeplitz, NEG_INF)


def _attn_prompt(q16, k16, v16, rel_bias, n_seq, seq_len):
    tiles = seq_len // ATTN_Q_TILE
    blk = (N_HEADS, ATTN_Q_TILE, HEAD_DIM)
    cur = pl.BlockSpec(blk, lambda b, i: (0, b * tiles + i, 0))
    past = pl.BlockSpec(blk, lambda b, i: (0, b * tiles + jnp.maximum(i - 1, 0), 0))
    bias = _prompt_bias(rel_bias)
    return pl.pallas_call(
        _attn_prompt_kernel,
        out_shape=jax.ShapeDtypeStruct(q16.shape, BF16),
        grid=(n_seq, tiles),
        in_specs=[cur, past, cur, past, cur,
                  pl.BlockSpec(bias.shape, lambda b, i: (0, 0, 0))],
        out_specs=cur,
        scratch_shapes=[pltpu.VMEM((N_HEADS, 2 * ATTN_Q_TILE, HEAD_DIM), BF16),
                        pltpu.VMEM((N_HEADS, 2 * ATTN_Q_TILE, HEAD_DIM), BF16)],
        compiler_params=_params(("parallel", "arbitrary")),
        name="attn_prompt",
    )(q16, k16, k16, v16, v16, bias)


def _attn_sample_kernel(q_ref, kn_ref, vn_ref, ck_ref, cv_ref, bc_ref, bn_ref, o_ref):
    for h in range(N_HEADS):
        cols = slice(h * HEAD_DIM, (h + 1) * HEAD_DIM)
        q = q_ref[h]
        kc = ck_ref[0, :, cols].astype(BF16)
        vc = cv_ref[0, :, cols].astype(BF16)
        dims = (((1,), (1,)), ((), ()))
        sc = lax.dot_general(q, kc, dims, preferred_element_type=F32) * ATTN_SCALE + bc_ref[h]
        sn = lax.dot_general(q, kn_ref[h], dims, preferred_element_type=F32) * ATTN_SCALE + bn_ref[h]
        m = jnp.maximum(jnp.max(sc, axis=-1, keepdims=True), jnp.max(sn, axis=-1, keepdims=True))
        ec = jnp.exp(sc - m)
        en = jnp.exp(sn - m)
        inv = 1.0 / (jnp.sum(ec, axis=-1, keepdims=True) + jnp.sum(en, axis=-1, keepdims=True))
        o = jnp.dot((ec * inv).astype(BF16), vc, preferred_element_type=F32)
        o = o + jnp.dot((en * inv).astype(BF16), vn_ref[h], preferred_element_type=F32)
        o_ref[h] = o.astype(BF16)


def _attn_sample(q16, k16, v16, cache_k, cache_v, rel_bias, n_seq, seq_len):
    cache_len = cache_k.shape[1]
    ck = cache_k.reshape(n_seq, cache_len, D_MODEL)
    cv = cache_v.reshape(n_seq, cache_len, D_MODEL)
    a = jnp.arange(seq_len)[:, None]
    w = jnp.arange(cache_len + seq_len)[None, :]
    idx = jnp.clip(a - w + cache_len, -MAX_REL, MAX_REL) + MAX_REL
    bias = rel_bias[:, idx].astype(F32)
    bias_c = bias[:, :, :cache_len]
    bias_n = bias[:, :, cache_len:]
    new = pl.BlockSpec((N_HEADS, seq_len, HEAD_DIM), lambda b: (0, b, 0))
    cache = pl.BlockSpec((1, cache_len, D_MODEL), lambda b: (b, 0, 0))
    return pl.pallas_call(
        _attn_sample_kernel,
        out_shape=jax.ShapeDtypeStruct(q16.shape, BF16),
        grid=(n_seq,),
        in_specs=[new, new, new, cache, cache,
                  pl.BlockSpec(bias_c.shape, lambda b: (0, 0, 0)),
                  pl.BlockSpec(bias_n.shape, lambda b: (0, 0, 0))],
        out_specs=new,
        compiler_params=_params(("parallel",)),
        name="attn_sample",
    )(q16, k16, v16, ck, cv, bias_c, bias_n)


def _trunk(x, conv_state, cache_k, cache_v, p, tm, tm_ffn):
    n_seq, seq_len, d = x.shape
    h = x.reshape(n_seq * seq_len, d)
    ffn = lambda h, layer, slot: _ffn(h, p["ffn_norm"], p["wg"], p["wu"], p["wd"], layer, slot,
                                      tm_ffn)

    h = ffn(h, 0, 0)
    h, new_state = _mixer(h, p["mix_norm"], p["conv_w_in"], p["conv_w"], p["conv_w_out"],
                          conv_state[0], 0, seq_len, tm)
    h = ffn(h, 0, 1)
    k32, v32, k16, v16 = _kv_proj(h, p["kv_norm"], p["w_kv"], p["k_gain"], tm)

    h = ffn(h, 1, 0)
    q16 = _q_proj(h, p["mix_norm"], p["w_q"], p["q_gain"], 1, 0, tm)
    if cache_k is None:
        att16 = _attn_prompt(q16, k16, v16, p["rel_bias"][0], n_seq, seq_len)
    else:
        att16 = _attn_sample(q16, k16, v16, cache_k, cache_v, p["rel_bias"][0], n_seq, seq_len)
    h = _o_proj(h, att16, p["w_o"], 0, tm)
    h = ffn(h, 1, 1)
    return (h.reshape(n_seq, seq_len, d), new_state[None],
            k32.reshape(n_seq, seq_len, N_HEADS, HEAD_DIM),
            v32.reshape(n_seq, seq_len, N_HEADS, HEAD_DIM))


def kernel(x_prompt, x_sample, state_conv, cache_k, cache_v, ffn_norm, ffn_w_gate, ffn_w_up,
           ffn_w_down, mix_norm, conv_w_in, conv_w, conv_w_out, kv_norm, w_kv, k_gain,
           w_q, q_gain, rel_bias, w_o):
    depth = ffn_norm.shape[0]
    assert depth == 2 and conv_w_in.shape[0] == 1 and w_q.shape[0] == 1
    p = {
        "ffn_norm": ffn_norm.reshape(depth, 2, 1, D_MODEL),
        "wg": ffn_w_gate.astype(BF16),
        "wu": ffn_w_up.astype(BF16),
        "wd": ffn_w_down.astype(BF16),
        "mix_norm": mix_norm.reshape(depth, 1, D_MODEL),
        "conv_w_in": conv_w_in.astype(BF16),
        "conv_w": conv_w,
        "conv_w_out": conv_w_out.astype(BF16),
        "kv_norm": kv_norm.reshape(1, D_MODEL),
        "w_kv": w_kv.astype(BF16),
        "k_gain": k_gain.reshape(1, HEAD_DIM),
        "w_q": w_q.astype(BF16),
        "q_gain": q_gain.reshape(-1, 1, HEAD_DIM),
        "rel_bias": rel_bias,
        "w_o": w_o.astype(BF16),
    }
    n_p, t_p, _ = x_prompt.shape
    n_s, t_s, _ = x_sample.shape
    conv_zero = jnp.zeros((1, n_p, CONV_WIDTH - 1, D_MODEL), F32)
    y_p, conv_p, k_p, v_p = _trunk(x_prompt, conv_zero, None, None, p,
                                   tm=PROMPT_ROW_TILE, tm_ffn=PROMPT_FFN_ROW_TILE)
    y_s, conv_s, k_s, v_s = _trunk(x_sample, state_conv, cache_k, cache_v, p,
                                   tm=n_s * t_s, tm_ffn=n_s * t_s)
    keep = min(PAST_ROWS, t_p)
    return (y_p, y_s, conv_p, k_p[:, -keep:], v_p[:, -keep:], conv_s, k_s, v_s)
```

```python
import collections
import functools

import jax
import jax.numpy as jnp
from jax import lax
from jax.experimental import pallas as pl
from jax.experimental.pallas import tpu as pltpu

F32 = jnp.float32
BF16 = jnp.bfloat16

D_MODEL = 2048
HEAD_DIM = 128
N_HEADS = D_MODEL // HEAD_DIM
CHUNK = 64
N_PAST_CHUNKS = 8
PAST_ROWS = N_PAST_CHUNKS * CHUNK
MAX_REL = 128
CONV_WIDTH = 3
EPS = 1e-6
FFN_RES = 0.5
NEG_INF = -1e30
ATTN_SCALE = HEAD_DIM ** -0.5

V7X_VMEM_BYTES = 64 * 1024 * 1024
VMEM_LIMIT_BYTES = V7X_VMEM_BYTES - 8 * 1024 * 1024

PROMPT_ROW_TILE = 512
PROMPT_FFN_ROW_TILE = 1024
FFN_COL_TILE = 512
MIX_COL_TILE = 512
KV_COL_TILE = 1024
CAST_COL_TILE = 512
ATTN_Q_TILE = 512
ATTN_SUB = 256
ATTN_WIN = ATTN_SUB + PAST_ROWS

Weight = collections.namedtuple("Weight", ["arr", "lead", "col_off"])


def _weight(arr, lead=(), col_off=0):
    return Weight(arr, tuple(lead), col_off)


def _wspec(w, block, index):
    lead = w.lead

    def index_map(*ids):
        r, c = index(*ids)
        return lead + (r, c + w.col_off)

    return pl.BlockSpec((None,) * len(lead) + block, index_map)


def _is_f32(w):
    return w.arr.dtype == F32


def _bf16_weight(w_ref, w16_ref):
    if w16_ref is None:
        return w_ref[...]
    w16 = w_ref[...].astype(BF16)
    w16_ref[...] = w16
    return w16


def _params(semantics):
    return pltpu.CompilerParams(dimension_semantics=semantics,
                                vmem_limit_bytes=VMEM_LIMIT_BYTES)


def _rms_rows(x, g):
    ms = jnp.mean(x * x, axis=-1, keepdims=True)
    return (x * lax.rsqrt(ms + EPS)) * g


def _ffn_kernel(h_ref, g_ref, wg_ref, wu_ref, wd_ref, out_ref, *rest, emit):
    wg16_ref, wu16_ref, wd16_ref = rest[:3] if emit else (None, None, None)
    xn_ref = rest[-1]
    f = pl.program_id(1)

    @pl.when(f == 0)
    def _():
        h = h_ref[...]
        xn_ref[...] = _rms_rows(h, g_ref[...]).astype(BF16)
        out_ref[...] = h

    xn = xn_ref[...]
    gate = jnp.dot(xn, _bf16_weight(wg_ref, wg16_ref), preferred_element_type=F32)
    up = jnp.dot(xn, _bf16_weight(wu_ref, wu16_ref), preferred_element_type=F32)
    act = (gate * jax.nn.sigmoid(gate) * (FFN_RES * up)).astype(BF16)
    out_ref[...] += jnp.dot(act, _bf16_weight(wd_ref, wd16_ref), preferred_element_type=F32)


def _ffn(h, norm_g, g_lead, wg, wu, wd, tm):
    m, d = h.shape
    d_ff = wg.arr.shape[-1]
    tf = FFN_COL_TILE
    emit = _is_f32(wg)
    out_shape = [jax.ShapeDtypeStruct((m, d), F32)]
    out_specs = [pl.BlockSpec((tm, d), lambda i, f: (i, 0))]
    if emit:
        assert m == tm
        out_shape += [jax.ShapeDtypeStruct((d, d_ff), BF16), jax.ShapeDtypeStruct((d, d_ff), BF16),
                      jax.ShapeDtypeStruct((d_ff, d), BF16)]
        out_specs += [pl.BlockSpec((d, tf), lambda i, f: (0, f)),
                      pl.BlockSpec((d, tf), lambda i, f: (0, f)),
                      pl.BlockSpec((tf, d), lambda i, f: (f, 0))]
    res = pl.pallas_call(
        functools.partial(_ffn_kernel, emit=emit),
        out_shape=out_shape,
        grid=(m // tm, d_ff // tf),
        in_specs=[
            pl.BlockSpec((tm, d), lambda i, f: (i, 0)),
            pl.BlockSpec((None,) * len(g_lead) + (1, d), lambda i, f: tuple(g_lead) + (0, 0)),
            _wspec(wg, (d, tf), lambda i, f: (0, f)),
            _wspec(wu, (d, tf), lambda i, f: (0, f)),
            _wspec(wd, (tf, d), lambda i, f: (f, 0)),
        ],
        out_specs=out_specs,
        scratch_shapes=[pltpu.VMEM((tm, d), BF16)],
        compiler_params=_params(("parallel", "arbitrary")),
        name="ffn",
    )(h, norm_g, wg.arr, wu.arr, wd.arr)
    return res[0], tuple(_weight(a) for a in res[1:])


def _mixer_kernel(h_ref, g_ref, wb_ref, wc_ref, wx_ref, cw_ref, wo_ref, p0_ref, p1_ref,
                  out_ref, st_ref, *rest, seq_len, tm, emit):
    wb16_ref, wc16_ref, wx16_ref, wo16_ref = rest[:4] if emit else (None,) * 4
    xn_ref, carry_ref = rest[-2:]
    i = pl.program_id(0)
    j = pl.program_id(1)

    @pl.when(j == 0)
    def _():
        h = h_ref[...]
        xn_ref[...] = _rms_rows(h, g_ref[...]).astype(BF16)
        out_ref[...] = h

    if seq_len >= tm:
        @pl.when((i % (seq_len // tm)) == 0)
        def _():
            carry_ref[j] = p0_ref[0]

    xn = xn_ref[...]
    b = jnp.dot(xn, _bf16_weight(wb_ref, wb16_ref), preferred_element_type=F32)
    c = jnp.dot(xn, _bf16_weight(wc_ref, wc16_ref), preferred_element_type=F32)
    x = jnp.dot(xn, _bf16_weight(wx_ref, wx16_ref), preferred_element_type=F32)
    u = c * x
    tn = u.shape[1]
    row = lax.broadcasted_iota(jnp.int32, (tm, tn), 0)
    if seq_len >= tm:
        prev = carry_ref[j]
        p0 = prev[0:1, :]
        p1 = prev[1:2, :]
        pos = row
        carry_ref[j] = u[tm - 2:tm, :]
        st_ref[0] = u[tm - 2:tm, :]
    else:
        p0 = p0_ref[...]
        p1 = p1_ref[...]
        pos = row % seq_len
        u3 = u.reshape(tm // seq_len, seq_len, tn)
        st_ref[...] = u3[:, seq_len - 2:seq_len, :]
    um1 = jnp.where(pos == 0, p1, pltpu.roll(u, 1, 0))
    um2 = jnp.where(pos == 0, p0, jnp.where(pos == 1, p1, pltpu.roll(u, 2, 0)))
    cw = cw_ref[...]
    conv = cw[0:1, :] * um2 + cw[1:2, :] * um1 + cw[2:3, :] * u
    v = (b * conv).astype(BF16)
    out_ref[...] += jnp.dot(v, _bf16_weight(wo_ref, wo16_ref), preferred_element_type=F32)


def _mixer(h, norm_g, g_lead, wb, wc, wx, conv_w, cw_lead, wo, state, seq_len, tm):
    m, d = h.shape
    n_seq = m // seq_len
    tn = MIX_COL_TILE
    n_j = d // tn
    n_i = m // tm
    emit = _is_f32(wb)
    if seq_len >= tm:
        tiles_per_seq = seq_len // tm
        p0, p1 = state, state
        p_spec = pl.BlockSpec((1, CONV_WIDTH - 1, tn), lambda i, j: (i // tiles_per_seq, 0, j))
        st_shape = jax.ShapeDtypeStruct((n_i, CONV_WIDTH - 1, d), F32)
        st_spec = pl.BlockSpec((1, CONV_WIDTH - 1, tn), lambda i, j: (i, 0, j))
    else:
        assert tm == m and tm % seq_len == 0
        p0 = jnp.repeat(state[:, 0], seq_len, axis=0)
        p1 = jnp.repeat(state[:, 1], seq_len, axis=0)
        p_spec = pl.BlockSpec((tm, tn), lambda i, j: (i, j))
        st_shape = jax.ShapeDtypeStruct((n_seq, CONV_WIDTH - 1, d), F32)
        st_spec = pl.BlockSpec((n_seq, CONV_WIDTH - 1, tn), lambda i, j: (0, 0, j))
    out_shape = [jax.ShapeDtypeStruct((m, d), F32), st_shape]
    out_specs = [pl.BlockSpec((tm, d), lambda i, j: (i, 0)), st_spec]
    if emit:
        assert m == tm
        out_shape += [jax.ShapeDtypeStruct((d, d), BF16)] * 4
        out_specs += [pl.BlockSpec((d, tn), lambda i, j: (0, j))] * 3
        out_specs += [pl.BlockSpec((tn, d), lambda i, j: (j, 0))]
    res = pl.pallas_call(
        functools.partial(_mixer_kernel, seq_len=seq_len, tm=tm, emit=emit),
        out_shape=out_shape,
        grid=(n_i, n_j),
        in_specs=[
            pl.BlockSpec((tm, d), lambda i, j: (i, 0)),
            pl.BlockSpec((None,) * len(g_lead) + (1, d), lambda i, j: tuple(g_lead) + (0, 0)),
            _wspec(wb, (d, tn), lambda i, j: (0, j)),
            _wspec(wc, (d, tn), lambda i, j: (0, j)),
            _wspec(wx, (d, tn), lambda i, j: (0, j)),
            pl.BlockSpec((None,) * len(cw_lead) + (CONV_WIDTH, tn),
                         lambda i, j: tuple(cw_lead) + (0, j)),
            _wspec(wo, (tn, d), lambda i, j: (j, 0)),
            p_spec,
            p_spec,
        ],
        out_specs=out_specs,
        scratch_shapes=[pltpu.VMEM((tm, d), BF16),
                        pltpu.VMEM((n_j, CONV_WIDTH - 1, tn), F32)],
        compiler_params=_params(("arbitrary", "arbitrary")),
        name="conv_mixer",
    )(h, norm_g, wb.arr, wc.arr, wx.arr, conv_w, wo.arr, p0, p1)
    out, st = res[0], res[1]
    if seq_len >= tm:
        st = st[seq_len // tm - 1::seq_len // tm]
    return out, st, tuple(_weight(a) for a in res[2:])


def _kv_kernel(h_ref, g_ref, wk_ref, wv_ref, kg_ref, k32_ref, v32_ref, k16_ref, v16_ref,
               *rest, emit):
    wk16_ref, wv16_ref = rest[:2] if emit else (None, None)
    xn_ref = rest[-1]
    j = pl.program_id(1)

    @pl.when(j == 0)
    def _():
        xn_ref[...] = _rms_rows(h_ref[...], g_ref[...]).astype(BF16)

    xn = xn_ref[...]
    k = jnp.dot(xn, _bf16_weight(wk_ref, wk16_ref), preferred_element_type=F32)
    v = jnp.dot(xn, _bf16_weight(wv_ref, wv16_ref), preferred_element_type=F32)
    kg = kg_ref[...]
    v32_ref[...] = v
    for hd in range(k.shape[1] // HEAD_DIM):
        cols = slice(hd * HEAD_DIM, (hd + 1) * HEAD_DIM)
        kh = _rms_rows(k[:, cols], kg)
        k32_ref[:, cols] = kh
        k16_ref[hd] = kh.astype(BF16)
        v16_ref[hd] = v[:, cols].astype(BF16)


def _kv_proj(h, norm_g, wk, wv, k_gain, tm):
    m, d = h.shape
    emit = _is_f32(wk)
    tn = CAST_COL_TILE if emit else KV_COL_TILE
    n_j = d // tn
    hpb = tn // HEAD_DIM
    out_shape = [jax.ShapeDtypeStruct((m, d), F32), jax.ShapeDtypeStruct((m, d), F32),
                 jax.ShapeDtypeStruct((N_HEADS, m, HEAD_DIM), BF16),
                 jax.ShapeDtypeStruct((N_HEADS, m, HEAD_DIM), BF16)]
    out_specs = [pl.BlockSpec((tm, tn), lambda i, j: (i, j)),
                 pl.BlockSpec((tm, tn), lambda i, j: (i, j)),
                 pl.BlockSpec((hpb, tm, HEAD_DIM), lambda i, j: (j, i, 0)),
                 pl.BlockSpec((hpb, tm, HEAD_DIM), lambda i, j: (j, i, 0))]
    if emit:
        assert m == tm
        out_shape += [jax.ShapeDtypeStruct((d, d), BF16)] * 2
        out_specs += [pl.BlockSpec((d, tn), lambda i, j: (0, j))] * 2
    res = pl.pallas_call(
        functools.partial(_kv_kernel, emit=emit),
        out_shape=out_shape,
        grid=(m // tm, n_j),
        in_specs=[
            pl.BlockSpec((tm, d), lambda i, j: (i, 0)),
            pl.BlockSpec((1, d), lambda i, j: (0, 0)),
            _wspec(wk, (d, tn), lambda i, j: (0, j)),
            _wspec(wv, (d, tn), lambda i, j: (0, j)),
            pl.BlockSpec((1, HEAD_DIM), lambda i, j: (0, 0)),
        ],
        out_specs=out_specs,
        scratch_shapes=[pltpu.VMEM((tm, d), BF16)],
        compiler_params=_params(("parallel", "arbitrary")),
        name="kv_proj",
    )(h, norm_g, wk.arr, wv.arr, k_gain)
    return res[:4], tuple(_weight(a) for a in res[4:])


def _q_kernel(h_ref, g_ref, wq_ref, qg_ref, q16_ref, *rest, emit):
    wq16_ref = rest[0] if emit else None
    xn_ref = rest[-1]

    @pl.when(pl.program_id(1) == 0)
    def _():
        xn_ref[...] = _rms_rows(h_ref[...], g_ref[...]).astype(BF16)

    q = jnp.dot(xn_ref[...], _bf16_weight(wq_ref, wq16_ref), preferred_element_type=F32)
    qg = qg_ref[...]
    for hd in range(q.shape[1] // HEAD_DIM):
        cols = slice(hd * HEAD_DIM, (hd + 1) * HEAD_DIM)
        q16_ref[hd] = _rms_rows(q[:, cols], qg).astype(BF16)


def _q_proj(h, norm_g, g_lead, wq, q_gain, qg_lead, tm):
    m, d = h.shape
    emit = _is_f32(wq)
    tn = CAST_COL_TILE if emit else d
    hpb = tn // HEAD_DIM
    out_shape = [jax.ShapeDtypeStruct((N_HEADS, m, HEAD_DIM), BF16)]
    out_specs = [pl.BlockSpec((hpb, tm, HEAD_DIM), lambda i, j: (j, i, 0))]
    if emit:
        assert m == tm
        out_shape += [jax.ShapeDtypeStruct((d, d), BF16)]
        out_specs += [pl.BlockSpec((d, tn), lambda i, j: (0, j))]
    res = pl.pallas_call(
        functools.partial(_q_kernel, emit=emit),
        out_shape=out_shape,
        grid=(m // tm, d // tn),
        in_specs=[
            pl.BlockSpec((tm, d), lambda i, j: (i, 0)),
            pl.BlockSpec((None,) * len(g_lead) + (1, d), lambda i, j: tuple(g_lead) + (0, 0)),
            _wspec(wq, (d, tn), lambda i, j: (0, j)),
            pl.BlockSpec((None,) * len(qg_lead) + (1, HEAD_DIM),
                         lambda i, j: tuple(qg_lead) + (0, 0)),
        ],
        out_specs=out_specs,
        scratch_shapes=[pltpu.VMEM((tm, d), BF16)],
        compiler_params=_params(("parallel", "arbitrary")),
        name="q_proj",
    )(h, norm_g, wq.arr, q_gain)
    return res[0], tuple(_weight(a) for a in res[1:])


def _o_kernel(h_ref, att_ref, wo_ref, out_ref, *rest, emit):
    wo16_ref = rest[0] if emit else None
    att2d_ref = rest[-1]

    @pl.when(pl.program_id(1) == 0)
    def _():
        for hd in range(N_HEADS):
            att2d_ref[:, hd * HEAD_DIM:(hd + 1) * HEAD_DIM] = att_ref[hd]

    out_ref[...] = h_ref[...] + jnp.dot(att2d_ref[...], _bf16_weight(wo_ref, wo16_ref),
                                        preferred_element_type=F32)


def _o_proj(h, att16, wo, tm):
    m, d = h.shape
    emit = _is_f32(wo)
    tn = CAST_COL_TILE if emit else d
    out_shape = [jax.ShapeDtypeStruct((m, d), F32)]
    out_specs = [pl.BlockSpec((tm, tn), lambda i, j: (i, j))]
    if emit:
        assert m == tm
        out_shape += [jax.ShapeDtypeStruct((d, d), BF16)]
        out_specs += [pl.BlockSpec((d, tn), lambda i, j: (0, j))]
    res = pl.pallas_call(
        functools.partial(_o_kernel, emit=emit),
        out_shape=out_shape,
        grid=(m // tm, d // tn),
        in_specs=[
            pl.BlockSpec((tm, tn), lambda i, j: (i, j)),
            pl.BlockSpec((N_HEADS, tm, HEAD_DIM), lambda i, j: (0, i, 0)),
            _wspec(wo, (d, tn), lambda i, j: (0, j)),
        ],
        out_specs=out_specs,
        scratch_shapes=[pltpu.VMEM((tm, d), BF16)],
        compiler_params=_params(("parallel", "arbitrary")),
        name="o_proj",
    )(h, att16, wo.arr)
    return res[0], tuple(_weight(a) for a in res[1:])


def _softmax_rows(s):
    m = jnp.max(s, axis=-1, keepdims=True)
    e = jnp.exp(s - m)
    return e * (1.0 / jnp.sum(e, axis=-1, keepdims=True))


def _attn_prompt_kernel(q_ref, kp_ref, kc_ref, vp_ref, vc_ref, bias_ref, o_ref,
                        kall_ref, vall_ref):
    i = pl.program_id(1)
    kall_ref[:, 0:ATTN_Q_TILE, :] = kp_ref[...]
    kall_ref[:, ATTN_Q_TILE:2 * ATTN_Q_TILE, :] = kc_ref[...]
    vall_ref[:, 0:ATTN_Q_TILE, :] = vp_ref[...]
    vall_ref[:, ATTN_Q_TILE:2 * ATTN_Q_TILE, :] = vc_ref[...]
    n_before_start = jnp.where(i == 0, ATTN_Q_TILE, 0)
    col = lax.broadcasted_iota(jnp.int32, (ATTN_SUB, ATTN_WIN), 1)

    def head_body(h, carry):
        for s in range(ATTN_Q_TILE // ATTN_SUB):
            lo = s * ATTN_SUB
            q = q_ref[h, lo:lo + ATTN_SUB, :]
            kw = kall_ref[h, lo:lo + ATTN_WIN, :]
            vw = vall_ref[h, lo:lo + ATTN_WIN, :]
            sc = lax.dot_general(q, kw, (((1,), (1,)), ((), ())), preferred_element_type=F32)
            sc = sc * ATTN_SCALE + bias_ref[h]
            sc = jnp.where(col < n_before_start - lo, NEG_INF, sc)
            p = _softmax_rows(sc).astype(BF16)
            o = jnp.dot(p, vw, preferred_element_type=F32)
            o_ref[h, lo:lo + ATTN_SUB, :] = o.astype(BF16)
        return carry

    lax.fori_loop(0, N_HEADS, head_body, 0)


def _prompt_bias(rel_bias):
    period = pl.next_power_of_2(ATTN_SUB + ATTN_WIN - 1)
    j = jnp.arange(period)
    k = jnp.where(j < ATTN_WIN, j, j - period)
    idx = jnp.clip(PAST_ROWS - k, -MAX_REL, MAX_REL) + MAX_REL
    vec = rel_bias[:, idx].astype(F32)
    flat = jnp.tile(vec, (1, ATTN_SUB))[:, :ATTN_SUB * (period - 1)]
    toeplitz = flat.reshape(N_HEADS, ATTN_SUB, period - 1)[:, :, :ATTN_WIN]
    qc = jnp.arange(ATTN_SUB)[:, None] // CHUNK
    kc = jnp.arange(ATTN_WIN)[None, :] // CHUNK
    band = (kc >= qc) & (kc <= qc + N_PAST_CHUNKS)
    return jnp.where(band[None], toeplitz, NEG_INF)


def _attn_prompt(q16, k16, v16, rel_bias, n_seq, seq_len):
    tiles = seq_len // ATTN_Q_TILE
    blk = (N_HEADS, ATTN_Q_TILE, HEAD_DIM)
    cur = pl.BlockSpec(blk, lambda b, i: (0, b * tiles + i, 0))
    past = pl.BlockSpec(blk, lambda b, i: (0, b * tiles + jnp.maximum(i - 1, 0), 0))
    bias = _prompt_bias(rel_bias)
    return pl.pallas_call(
        _attn_prompt_kernel,
        out_shape=jax.ShapeDtypeStruct(q16.shape, BF16),
        grid=(n_seq, tiles),
        in_specs=[cur, past, cur, past, cur,
                  pl.BlockSpec(bias.shape, lambda b, i: (0, 0, 0))],
        out_specs=cur,
        scratch_shapes=[pltpu.VMEM((N_HEADS, 2 * ATTN_Q_TILE, HEAD_DIM), BF16),
                        pltpu.VMEM((N_HEADS, 2 * ATTN_Q_TILE, HEAD_DIM), BF16)],
        compiler_params=_params(("parallel", "arbitrary")),
        name="attn_prompt",
    )(q16, k16, k16, v16, v16, bias)


def _attn_sample_kernel(q_ref, kn_ref, vn_ref, ck_ref, cv_ref, bc_ref, bn_ref, o_ref):
    for h in range(N_HEADS):
        cols = slice(h * HEAD_DIM, (h + 1) * HEAD_DIM)
        q = q_ref[h]
        kc = ck_ref[0, :, cols].astype(BF16)
        vc = cv_ref[0, :, cols].astype(BF16)
        dims = (((1,), (1,)), ((), ()))
        sc = lax.dot_general(q, kc, dims, preferred_element_type=F32) * ATTN_SCALE + bc_ref[h]
        sn = lax.dot_general(q, kn_ref[h], dims, preferred_element_type=F32) * ATTN_SCALE + bn_ref[h]
        m = jnp.maximum(jnp.max(sc, axis=-1, keepdims=True), jnp.max(sn, axis=-1, keepdims=True))
        ec = jnp.exp(sc - m)
        en = jnp.exp(sn - m)
        inv = 1.0 / (jnp.sum(ec, axis=-1, keepdims=True) + jnp.sum(en, axis=-1, keepdims=True))
        o = jnp.dot((ec * inv).astype(BF16), vc, preferred_element_type=F32)
        o = o + jnp.dot((en * inv).astype(BF16), vn_ref[h], preferred_element_type=F32)
        o_ref[h] = o.astype(BF16)


def _attn_sample(q16, k16, v16, cache_k, cache_v, rel_bias, n_seq, seq_len):
    cache_len = cache_k.shape[1]
    ck = cache_k.reshape(n_seq, cache_len, D_MODEL)
    cv = cache_v.reshape(n_seq, cache_len, D_MODEL)
    a = jnp.arange(seq_len)[:, None]
    w = jnp.arange(cache_len + seq_len)[None, :]
    idx = jnp.clip(a - w + cache_len, -MAX_REL, MAX_REL) + MAX_REL
    bias = rel_bias[:, idx].astype(F32)
    bias_c = bias[:, :, :cache_len]
    bias_n = bias[:, :, cache_len:]
    new = pl.BlockSpec((N_HEADS, seq_len, HEAD_DIM), lambda b: (0, b, 0))
    cache = pl.BlockSpec((1, cache_len, D_MODEL), lambda b: (b, 0, 0))
    return pl.pallas_call(
        _attn_sample_kernel,
        out_shape=jax.ShapeDtypeStruct(q16.shape, BF16),
        grid=(n_seq,),
        in_specs=[new, new, new, cache, cache,
                  pl.BlockSpec(bias_c.shape, lambda b: (0, 0, 0)),
                  pl.BlockSpec(bias_n.shape, lambda b: (0, 0, 0))],
        out_specs=new,
        compiler_params=_params(("parallel",)),
        name="attn_sample",
    )(q16, k16, v16, ck, cv, bias_c, bias_n)


def _trunk(x, conv_state, cache_k, cache_v, p, w, tm, tm_ffn):
    n_seq, seq_len, d = x.shape
    h = x.reshape(n_seq * seq_len, d)
    w16 = {}

    def ffn(h, layer, slot):
        key = "ffn%d%d" % (layer, slot)
        h, w16[key] = _ffn(h, p["ffn_norm"], (layer, slot), *w[key], tm_ffn)
        return h

    h = ffn(h, 0, 0)
    h, new_state, w16["mixer"] = _mixer(h, p["mix_norm"], (0,), *w["mixer"][:3], p["conv_w"], (0,),
                                        w["mixer"][3], conv_state[0], seq_len, tm)
    h = ffn(h, 0, 1)
    (k32, v32, k16, v16), w16["kv"] = _kv_proj(h, p["kv_norm"], *w["kv"], p["k_gain"], tm)

    h = ffn(h, 1, 0)
    q16, w16["q"] = _q_proj(h, p["mix_norm"], (1,), *w["q"], p["q_gain"], (0,), tm)
    if cache_k is None:
        att16 = _attn_prompt(q16, k16, v16, p["rel_bias"][0], n_seq, seq_len)
    else:
        att16 = _attn_sample(q16, k16, v16, cache_k, cache_v, p["rel_bias"][0], n_seq, seq_len)
    h, w16["o"] = _o_proj(h, att16, *w["o"], tm)
    h = ffn(h, 1, 1)
    return (h.reshape(n_seq, seq_len, d), new_state[None],
            k32.reshape(n_seq, seq_len, N_HEADS, HEAD_DIM),
            v32.reshape(n_seq, seq_len, N_HEADS, HEAD_DIM), w16)


def kernel(x_prompt, x_sample, state_conv, cache_k, cache_v, ffn_norm, ffn_w_gate, ffn_w_up,
           ffn_w_down, mix_norm, conv_w_in, conv_w, conv_w_out, kv_norm, w_kv, k_gain,
           w_q, q_gain, rel_bias, w_o):
    depth = ffn_norm.shape[0]
    assert depth == 2 and conv_w_in.shape[0] == 1 and w_q.shape[0] == 1
    p = {
        "ffn_norm": ffn_norm.reshape(depth, 2, 1, D_MODEL),
        "mix_norm": mix_norm.reshape(depth, 1, D_MODEL),
        "conv_w": conv_w,
        "kv_norm": kv_norm.reshape(1, D_MODEL),
        "k_gain": k_gain.reshape(1, HEAD_DIM),
        "q_gain": q_gain.reshape(-1, 1, HEAD_DIM),
        "rel_bias": rel_bias,
    }
    mix_blocks = D_MODEL // MIX_COL_TILE
    proj_blocks = D_MODEL // CAST_COL_TILE
    w32 = {"ffn%d%d" % (layer, slot): (_weight(ffn_w_gate, (layer, slot)),
                                       _weight(ffn_w_up, (layer, slot)),
                                       _weight(ffn_w_down, (layer, slot)))
           for layer in range(depth) for slot in range(2)}
    w32["mixer"] = (_weight(conv_w_in, (0,)), _weight(conv_w_in, (0,), mix_blocks),
                    _weight(conv_w_in, (0,), 2 * mix_blocks), _weight(conv_w_out, (0,)))
    w32["kv"] = (_weight(w_kv), _weight(w_kv, (), proj_blocks))
    w32["q"] = (_weight(w_q, (0,)),)
    w32["o"] = (_weight(w_o, (0,)),)

    n_p, t_p, _ = x_prompt.shape
    n_s, t_s, _ = x_sample.shape
    y_s, conv_s, k_s, v_s, w16 = _trunk(x_sample, state_conv, cache_k, cache_v, p, w32,
                                        tm=n_s * t_s, tm_ffn=n_s * t_s)
    conv_zero = jnp.zeros((1, n_p, CONV_WIDTH - 1, D_MODEL), F32)
    y_p, conv_p, k_p, v_p, _ = _trunk(x_prompt, conv_zero, None, None, p, w16,
                                      tm=PROMPT_ROW_TILE, tm_ffn=PROMPT_FFN_ROW_TILE)
    keep = min(PAST_ROWS, t_p)
    return (y_p, y_s, conv_p, k_p[:, -keep:], v_p[:, -keep:], conv_s, k_s, v_s)
```

```python
import collections
import functools

import jax
import jax.numpy as jnp
from jax import lax
from jax.experimental import pallas as pl
from jax.experimental.pallas import tpu as pltpu

F32 = jnp.float32
BF16 = jnp.bfloat16

D_MODEL = 2048
HEAD_DIM = 128
N_HEADS = D_MODEL // HEAD_DIM
CHUNK = 64
N_PAST_CHUNKS = 8
PAST_ROWS = N_PAST_CHUNKS * CHUNK
MAX_REL = 128
CONV_WIDTH = 3
EPS = 1e-6
FFN_RES = 0.5
NEG_INF = -1e30
ATTN_SCALE = HEAD_DIM ** -0.5

SUBLANES = 8
V7X_VMEM_BYTES = 64 * 1024 * 1024
VMEM_LIMIT_BYTES = V7X_VMEM_BYTES - 8 * 1024 * 1024

PROMPT_ROW_TILE = 512
PROMPT_FFN_ROW_TILE = 1024
FFN_COL_TILE = 512
MIX_COL_TILE = 512
KV_COL_TILE = 1024
CAST_COL_TILE = 512
ATTN_Q_TILE = 512
ATTN_SUB = 256
ATTN_WIN = ATTN_SUB + PAST_ROWS
ATTN_HEADS_PER_STEP = 4

Weight = collections.namedtuple("Weight", ["arr", "lead", "col_off"])


def _weight(arr, lead=(), col_off=0):
    return Weight(arr, tuple(lead), col_off)


def _wspec(w, block, index):
    lead = w.lead

    def index_map(*ids):
        r, c = index(*ids)
        return lead + (r, c + w.col_off)

    return pl.BlockSpec((None,) * len(lead) + block, index_map)


def _is_f32(w):
    return w.arr.dtype == F32


def _bf16_weight(w_ref, w16_ref):
    if w16_ref is None:
        return w_ref[...]
    w16 = w_ref[...].astype(BF16)
    w16_ref[...] = w16
    return w16


def _params(semantics):
    return pltpu.CompilerParams(dimension_semantics=semantics,
                                vmem_limit_bytes=VMEM_LIMIT_BYTES)


def _rms_rows(x, g):
    ms = jnp.mean(x * x, axis=-1, keepdims=True)
    return (x * lax.rsqrt(ms + EPS)) * g


def _ffn_kernel(h_ref, g_ref, wg_ref, wu_ref, wd_ref, out_ref, *rest, emit):
    wg16_ref, wu16_ref, wd16_ref = rest[:3] if emit else (None, None, None)
    xn_ref = rest[-1]
    f = pl.program_id(1)

    @pl.when(f == 0)
    def _():
        h = h_ref[...]
        xn_ref[...] = _rms_rows(h, g_ref[...]).astype(BF16)
        out_ref[...] = h

    xn = xn_ref[...]
    gate = jnp.dot(xn, _bf16_weight(wg_ref, wg16_ref), preferred_element_type=F32)
    up = jnp.dot(xn, _bf16_weight(wu_ref, wu16_ref), preferred_element_type=F32)
    act = (gate * jax.nn.sigmoid(gate) * (FFN_RES * up)).astype(BF16)
    out_ref[...] += jnp.dot(act, _bf16_weight(wd_ref, wd16_ref), preferred_element_type=F32)


def _ffn(h, norm_g, g_lead, wg, wu, wd, tm):
    m, d = h.shape
    d_ff = wg.arr.shape[-1]
    tf = FFN_COL_TILE
    emit = _is_f32(wg)
    out_shape = [jax.ShapeDtypeStruct((m, d), F32)]
    out_specs = [pl.BlockSpec((tm, d), lambda i, f: (i, 0))]
    if emit:
        assert m == tm
        out_shape += [jax.ShapeDtypeStruct((d, d_ff), BF16), jax.ShapeDtypeStruct((d, d_ff), BF16),
                      jax.ShapeDtypeStruct((d_ff, d), BF16)]
        out_specs += [pl.BlockSpec((d, tf), lambda i, f: (0, f)),
                      pl.BlockSpec((d, tf), lambda i, f: (0, f)),
                      pl.BlockSpec((tf, d), lambda i, f: (f, 0))]
    res = pl.pallas_call(
        functools.partial(_ffn_kernel, emit=emit),
        out_shape=out_shape,
        grid=(m // tm, d_ff // tf),
        in_specs=[
            pl.BlockSpec((tm, d), lambda i, f: (i, 0)),
            pl.BlockSpec((None,) * len(g_lead) + (1, d), lambda i, f: tuple(g_lead) + (0, 0)),
            _wspec(wg, (d, tf), lambda i, f: (0, f)),
            _wspec(wu, (d, tf), lambda i, f: (0, f)),
            _wspec(wd, (tf, d), lambda i, f: (f, 0)),
        ],
        out_specs=out_specs,
        scratch_shapes=[pltpu.VMEM((tm, d), BF16)],
        compiler_params=_params(("parallel", "arbitrary")),
        name="ffn",
    )(h, norm_g, wg.arr, wu.arr, wd.arr)
    return res[0], tuple(_weight(a) for a in res[1:])


def _mixer_kernel(h_ref, g_ref, wb_ref, wc_ref, wx_ref, cw_ref, wo_ref, p0_ref, p1_ref,
                  out_ref, st_ref, *rest, seq_len, tm, emit):
    wb16_ref, wc16_ref, wx16_ref, wo16_ref = rest[:4] if emit else (None,) * 4
    xn_ref, carry_ref = rest[-2:]
    i = pl.program_id(0)
    j = pl.program_id(1)

    @pl.when(j == 0)
    def _():
        h = h_ref[...]
        xn_ref[...] = _rms_rows(h, g_ref[...]).astype(BF16)
        out_ref[...] = h

    if seq_len >= tm:
        @pl.when((i % (seq_len // tm)) == 0)
        def _():
            carry_ref[j] = p0_ref[0]

    xn = xn_ref[...]
    b = jnp.dot(xn, _bf16_weight(wb_ref, wb16_ref), preferred_element_type=F32)
    c = jnp.dot(xn, _bf16_weight(wc_ref, wc16_ref), preferred_element_type=F32)
    x = jnp.dot(xn, _bf16_weight(wx_ref, wx16_ref), preferred_element_type=F32)
    u = c * x
    tn = u.shape[1]
    row = lax.broadcasted_iota(jnp.int32, (tm, tn), 0)
    if seq_len >= tm:
        prev = carry_ref[j]
        p0 = prev[0:1, :]
        p1 = prev[1:2, :]
        pos = row
        carry_ref[j] = u[tm - 2:tm, :]
        st_ref[0] = u[tm - 2:tm, :]
    else:
        p0 = p0_ref[...]
        p1 = p1_ref[...]
        pos = row % seq_len
        u3 = u.reshape(tm // seq_len, seq_len, tn)
        st_ref[...] = u3[:, seq_len - 2:seq_len, :]
    um1 = jnp.where(pos == 0, p1, pltpu.roll(u, 1, 0))
    um2 = jnp.where(pos == 0, p0, jnp.where(pos == 1, p1, pltpu.roll(u, 2, 0)))
    cw = cw_ref[...]
    conv = cw[0:1, :] * um2 + cw[1:2, :] * um1 + cw[2:3, :] * u
    v = (b * conv).astype(BF16)
    out_ref[...] += jnp.dot(v, _bf16_weight(wo_ref, wo16_ref), preferred_element_type=F32)


def _mixer(h, norm_g, g_lead, wb, wc, wx, conv_w, cw_lead, wo, state, seq_len, tm):
    m, d = h.shape
    n_seq = m // seq_len
    tn = MIX_COL_TILE
    n_j = d // tn
    n_i = m // tm
    emit = _is_f32(wb)
    if seq_len >= tm:
        tiles_per_seq = seq_len // tm
        p0, p1 = state, state
        p_spec = pl.BlockSpec((1, CONV_WIDTH - 1, tn), lambda i, j: (i // tiles_per_seq, 0, j))
        st_shape = jax.ShapeDtypeStruct((n_i, CONV_WIDTH - 1, d), F32)
        st_spec = pl.BlockSpec((1, CONV_WIDTH - 1, tn), lambda i, j: (i, 0, j))
    else:
        assert tm == m and tm % seq_len == 0
        p0 = jnp.repeat(state[:, 0], seq_len, axis=0)
        p1 = jnp.repeat(state[:, 1], seq_len, axis=0)
        p_spec = pl.BlockSpec((tm, tn), lambda i, j: (i, j))
        st_shape = jax.ShapeDtypeStruct((n_seq, CONV_WIDTH - 1, d), F32)
        st_spec = pl.BlockSpec((n_seq, CONV_WIDTH - 1, tn), lambda i, j: (0, 0, j))
    out_shape = [jax.ShapeDtypeStruct((m, d), F32), st_shape]
    out_specs = [pl.BlockSpec((tm, d), lambda i, j: (i, 0)), st_spec]
    if emit:
        assert m == tm
        out_shape += [jax.ShapeDtypeStruct((d, d), BF16)] * 4
        out_specs += [pl.BlockSpec((d, tn), lambda i, j: (0, j))] * 3
        out_specs += [pl.BlockSpec((tn, d), lambda i, j: (j, 0))]
    res = pl.pallas_call(
        functools.partial(_mixer_kernel, seq_len=seq_len, tm=tm, emit=emit),
        out_shape=out_shape,
        grid=(n_i, n_j),
        in_specs=[
            pl.BlockSpec((tm, d), lambda i, j: (i, 0)),
            pl.BlockSpec((None,) * len(g_lead) + (1, d), lambda i, j: tuple(g_lead) + (0, 0)),
            _wspec(wb, (d, tn), lambda i, j: (0, j)),
            _wspec(wc, (d, tn), lambda i, j: (0, j)),
            _wspec(wx, (d, tn), lambda i, j: (0, j)),
            pl.BlockSpec((None,) * len(cw_lead) + (CONV_WIDTH, tn),
                         lambda i, j: tuple(cw_lead) + (0, j)),
            _wspec(wo, (tn, d), lambda i, j: (j, 0)),
            p_spec,
            p_spec,
        ],
        out_specs=out_specs,
        scratch_shapes=[pltpu.VMEM((tm, d), BF16),
                        pltpu.VMEM((n_j, CONV_WIDTH - 1, tn), F32)],
        compiler_params=_params(("arbitrary", "arbitrary")),
        name="conv_mixer",
    )(h, norm_g, wb.arr, wc.arr, wx.arr, conv_w, wo.arr, p0, p1)
    out, st = res[0], res[1]
    if seq_len >= tm:
        st = st[seq_len // tm - 1::seq_len // tm]
    return out, st, tuple(_weight(a) for a in res[2:])


def _kv_kernel(h_ref, g_ref, wk_ref, wv_ref, kg_ref, k32_ref, v32_ref, k16_ref, v16_ref,
               *rest, emit, keep, v_transposed):
    wk16_ref, wv16_ref = rest if emit else (None, None)
    xn = _rms_rows(h_ref[...], g_ref[...]).astype(BF16)
    k = jnp.dot(xn, _bf16_weight(wk_ref, wk16_ref), preferred_element_type=F32)
    v = jnp.dot(xn, _bf16_weight(wv_ref, wv16_ref), preferred_element_type=F32)
    kg = kg_ref[...]
    tm = k.shape[0]
    v32_ref[...] = v[tm - keep:, :]
    for hd in range(k.shape[1] // HEAD_DIM):
        cols = slice(hd * HEAD_DIM, (hd + 1) * HEAD_DIM)
        kh = _rms_rows(k[:, cols], kg)
        k32_ref[:, cols] = kh[tm - keep:, :]
        k16_ref[hd] = kh.astype(BF16)
        if v_transposed:
            v16_ref[hd] = v[:, cols].T.astype(BF16)
        else:
            v16_ref[hd] = v[:, cols].astype(BF16)


def _kv_proj(h, norm_g, wk, wv, k_gain, seq_len, tm, v_transposed):
    m, d = h.shape
    emit = _is_f32(wk)
    tn = CAST_COL_TILE if emit else KV_COL_TILE
    hpb = tn // HEAD_DIM
    if seq_len >= tm:
        keep = min(PAST_ROWS, seq_len)
        assert keep <= tm and seq_len % tm == 0
        tiles_per_seq = seq_len // tm
        kept_rows = (m // seq_len) * keep
        kept_spec = pl.BlockSpec((keep, tn), lambda j, i: (i // tiles_per_seq, j))
    else:
        assert tm % seq_len == 0
        keep, kept_rows = tm, m
        kept_spec = pl.BlockSpec((tm, tn), lambda j, i: (i, j))
    if v_transposed:
        v16_shape = jax.ShapeDtypeStruct((N_HEADS, HEAD_DIM, m), BF16)
        v16_spec = pl.BlockSpec((hpb, HEAD_DIM, tm), lambda j, i: (j, 0, i))
    else:
        v16_shape = jax.ShapeDtypeStruct((N_HEADS, m, HEAD_DIM), BF16)
        v16_spec = pl.BlockSpec((hpb, tm, HEAD_DIM), lambda j, i: (j, i, 0))
    out_shape = [jax.ShapeDtypeStruct((kept_rows, d), F32),
                 jax.ShapeDtypeStruct((kept_rows, d), F32),
                 jax.ShapeDtypeStruct((N_HEADS, m, HEAD_DIM), BF16), v16_shape]
    out_specs = [kept_spec, kept_spec,
                 pl.BlockSpec((hpb, tm, HEAD_DIM), lambda j, i: (j, i, 0)), v16_spec]
    if emit:
        assert m == tm
        out_shape += [jax.ShapeDtypeStruct((d, d), BF16)] * 2
        out_specs += [pl.BlockSpec((d, tn), lambda j, i: (0, j))] * 2
    res = pl.pallas_call(
        functools.partial(_kv_kernel, emit=emit, keep=keep, v_transposed=v_transposed),
        out_shape=out_shape,
        grid=(d // tn, m // tm),
        in_specs=[
            pl.BlockSpec((tm, d), lambda j, i: (i, 0)),
            pl.BlockSpec((1, d), lambda j, i: (0, 0)),
            _wspec(wk, (d, tn), lambda j, i: (0, j)),
            _wspec(wv, (d, tn), lambda j, i: (0, j)),
            pl.BlockSpec((1, HEAD_DIM), lambda j, i: (0, 0)),
        ],
        out_specs=out_specs,
        compiler_params=_params(("arbitrary", "arbitrary")),
        name="kv_proj",
    )(h, norm_g, wk.arr, wv.arr, k_gain)
    return res[:4], tuple(_weight(a) for a in res[4:])


def _q_kernel(h_ref, g_ref, wq_ref, qg_ref, q16_ref, *rest, emit):
    wq16_ref = rest[0] if emit else None
    xn_ref = rest[-1]

    @pl.when(pl.program_id(1) == 0)
    def _():
        xn_ref[...] = _rms_rows(h_ref[...], g_ref[...]).astype(BF16)

    q = jnp.dot(xn_ref[...], _bf16_weight(wq_ref, wq16_ref), preferred_element_type=F32)
    qg = qg_ref[...]
    for hd in range(q.shape[1] // HEAD_DIM):
        cols = slice(hd * HEAD_DIM, (hd + 1) * HEAD_DIM)
        q16_ref[hd] = _rms_rows(q[:, cols], qg).astype(BF16)


def _q_proj(h, norm_g, g_lead, wq, q_gain, qg_lead, tm):
    m, d = h.shape
    emit = _is_f32(wq)
    tn = CAST_COL_TILE if emit else d
    hpb = tn // HEAD_DIM
    out_shape = [jax.ShapeDtypeStruct((N_HEADS, m, HEAD_DIM), BF16)]
    out_specs = [pl.BlockSpec((hpb, tm, HEAD_DIM), lambda i, j: (j, i, 0))]
    if emit:
        assert m == tm
        out_shape += [jax.ShapeDtypeStruct((d, d), BF16)]
        out_specs += [pl.BlockSpec((d, tn), lambda i, j: (0, j))]
    res = pl.pallas_call(
        functools.partial(_q_kernel, emit=emit),
        out_shape=out_shape,
        grid=(m // tm, d // tn),
        in_specs=[
            pl.BlockSpec((tm, d), lambda i, j: (i, 0)),
            pl.BlockSpec((None,) * len(g_lead) + (1, d), lambda i, j: tuple(g_lead) + (0, 0)),
            _wspec(wq, (d, tn), lambda i, j: (0, j)),
            pl.BlockSpec((None,) * len(qg_lead) + (1, HEAD_DIM),
                         lambda i, j: tuple(qg_lead) + (0, 0)),
        ],
        out_specs=out_specs,
        scratch_shapes=[pltpu.VMEM((tm, d), BF16)],
        compiler_params=_params(("parallel", "arbitrary")),
        name="q_proj",
    )(h, norm_g, wq.arr, q_gain)
    return res[0], tuple(_weight(a) for a in res[1:])


def _o_kernel(h_ref, att_ref, wo_ref, out_ref, *rest, emit, att_transposed):
    wo16_ref = rest[0] if emit else None
    wo = _bf16_weight(wo_ref, wo16_ref)
    if att_transposed:
        att_t = att_ref[...].reshape(D_MODEL, att_ref.shape[2])
        y = lax.dot_general(att_t, wo, (((0,), (0,)), ((), ())), preferred_element_type=F32)
    else:
        att2d_ref = rest[-1]

        @pl.when(pl.program_id(1) == 0)
        def _():
            for hd in range(N_HEADS):
                att2d_ref[:, hd * HEAD_DIM:(hd + 1) * HEAD_DIM] = att_ref[hd]

        y = jnp.dot(att2d_ref[...], wo, preferred_element_type=F32)
    out_ref[...] = h_ref[...] + y


def _o_proj(h, att16, wo, tm, att_transposed):
    m, d = h.shape
    emit = _is_f32(wo)
    tn = CAST_COL_TILE if emit else d
    out_shape = [jax.ShapeDtypeStruct((m, d), F32)]
    out_specs = [pl.BlockSpec((tm, tn), lambda i, j: (i, j))]
    if emit:
        assert m == tm
        out_shape += [jax.ShapeDtypeStruct((d, d), BF16)]
        out_specs += [pl.BlockSpec((d, tn), lambda i, j: (0, j))]
    if att_transposed:
        att_spec = pl.BlockSpec((N_HEADS, HEAD_DIM, tm), lambda i, j: (0, 0, i))
        scratch = []
    else:
        att_spec = pl.BlockSpec((N_HEADS, tm, HEAD_DIM), lambda i, j: (0, i, 0))
        scratch = [pltpu.VMEM((tm, d), BF16)]
    res = pl.pallas_call(
        functools.partial(_o_kernel, emit=emit, att_transposed=att_transposed),
        out_shape=out_shape,
        grid=(m // tm, d // tn),
        in_specs=[
            pl.BlockSpec((tm, tn), lambda i, j: (i, j)),
            att_spec,
            _wspec(wo, (d, tn), lambda i, j: (0, j)),
        ],
        out_specs=out_specs,
        scratch_shapes=scratch,
        compiler_params=_params(("parallel", "arbitrary")),
        name="o_proj",
    )(h, att16, wo.arr)
    return res[0], tuple(_weight(a) for a in res[1:])


def _attn_prompt_kernel(q_ref, kp_ref, kc_ref, vp_ref, vc_ref, bias_ref, o_ref,
                        s_refs, p_refs):
    i = pl.program_id(1)
    n_before_start = jnp.where(i == 0, ATTN_Q_TILE, 0)
    n_chunks = ATTN_WIN // CHUNK
    groups = CHUNK // SUBLANES

    subs = ATTN_Q_TILE // ATTN_SUB
    n_units = ATTN_HEADS_PER_STEP * subs

    def scores(h0, u):
        h, lo = h0 + u // subs, (u % subs) * ATTN_SUB
        q = q_ref[h, lo:lo + ATTN_SUB, :]
        n_past = ATTN_Q_TILE - lo
        dims = (((1,), (1,)), ((), ()))
        s_refs[u, 0:n_past, :] = lax.dot_general(kp_ref[h, lo:, :], q, dims,
                                                 preferred_element_type=F32)
        s_refs[u, n_past:, :] = lax.dot_general(kc_ref[h, 0:ATTN_WIN - n_past, :], q, dims,
                                                preferred_element_type=F32)

    def softmax_and_values(h0, u):
        h, lo = h0 + u // subs, (u % subs) * ATTN_SUB
        s_ref, p_ref = s_refs.at[u], p_refs.at[u]
        m8 = jnp.full((SUBLANES, ATTN_SUB), NEG_INF, F32)
        for c in range(n_chunks):
            rows = slice(c * CHUNK, (c + 1) * CHUNK)
            before_start = c * CHUNK < n_before_start - lo
            x = s_ref[rows, :] * ATTN_SCALE + bias_ref[h, rows, :]
            x = jnp.where(before_start, NEG_INF, x)
            s_ref[rows, :] = x
            m8 = jnp.maximum(m8, jnp.max(x.reshape(groups, SUBLANES, ATTN_SUB), axis=0))
        m = jnp.max(m8, axis=0, keepdims=True)
        l8 = jnp.zeros((SUBLANES, ATTN_SUB), F32)
        for c in range(n_chunks):
            rows = slice(c * CHUNK, (c + 1) * CHUNK)
            e = jnp.exp(s_ref[rows, :] - m)
            p_ref[rows, :] = e.astype(BF16)
            l8 = l8 + jnp.sum(e.reshape(groups, SUBLANES, ATTN_SUB), axis=0)
        inv = 1.0 / jnp.sum(l8, axis=0, keepdims=True)
        n_past = ATTN_Q_TILE - lo
        o = jnp.dot(vp_ref[h, :, lo:], p_ref[0:n_past, :], preferred_element_type=F32)
        o = o + jnp.dot(vc_ref[h, :, 0:ATTN_WIN - n_past], p_ref[n_past:, :],
                        preferred_element_type=F32)
        o_ref[h, :, lo:lo + ATTN_SUB] = (o * inv).astype(BF16)

    def step(t, carry):
        h0 = t * ATTN_HEADS_PER_STEP
        scores(h0, 0)
        for u in range(n_units):
            if u + 1 < n_units:
                scores(h0, u + 1)
            softmax_and_values(h0, u)
        return carry

    lax.fori_loop(0, N_HEADS // ATTN_HEADS_PER_STEP, step, 0)


def _prompt_bias(rel_bias):
    period = pl.next_power_of_2(ATTN_SUB + ATTN_WIN - 1)
    j = jnp.arange(period)
    k = jnp.where(j < ATTN_WIN, j, j - period)
    idx = jnp.clip(PAST_ROWS - k, -MAX_REL, MAX_REL) + MAX_REL
    vec = rel_bias[:, idx].astype(F32)
    flat = jnp.tile(vec, (1, ATTN_SUB))[:, :ATTN_SUB * (period - 1)]
    toeplitz = flat.reshape(N_HEADS, ATTN_SUB, period - 1)[:, :, :ATTN_WIN]
    qc = jnp.arange(ATTN_SUB)[:, None] // CHUNK
    kc = jnp.arange(ATTN_WIN)[None, :] // CHUNK
    band = (kc >= qc) & (kc <= qc + N_PAST_CHUNKS)
    return jnp.swapaxes(jnp.where(band[None], toeplitz, NEG_INF), 1, 2)


def _attn_prompt(q16, k16, v16t, rel_bias, n_seq, seq_len):
    tiles = seq_len // ATTN_Q_TILE
    cur = lambda b, i: b * tiles + i
    past = lambda b, i: b * tiles + jnp.maximum(i - 1, 0)
    rows = lambda at: pl.BlockSpec((N_HEADS, ATTN_Q_TILE, HEAD_DIM), lambda b, i: (0, at(b, i), 0))
    cols = lambda at: pl.BlockSpec((N_HEADS, HEAD_DIM, ATTN_Q_TILE), lambda b, i: (0, 0, at(b, i)))
    bias = _prompt_bias(rel_bias)
    n_units = ATTN_HEADS_PER_STEP * (ATTN_Q_TILE // ATTN_SUB)
    return pl.pallas_call(
        _attn_prompt_kernel,
        out_shape=jax.ShapeDtypeStruct(v16t.shape, BF16),
        grid=(n_seq, tiles),
        in_specs=[rows(cur), rows(past), rows(cur), cols(past), cols(cur),
                  pl.BlockSpec(bias.shape, lambda b, i: (0, 0, 0))],
        out_specs=cols(cur),
        scratch_shapes=[pltpu.VMEM((n_units, ATTN_WIN, ATTN_SUB), F32),
                        pltpu.VMEM((n_units, ATTN_WIN, ATTN_SUB), BF16)],
        compiler_params=_params(("parallel", "arbitrary")),
        name="attn_prompt",
    )(q16, k16, k16, v16t, v16t, bias)


def _attn_sample_kernel(q_ref, kn_ref, vn_ref, ck_ref, cv_ref, bc_ref, bn_ref, o_ref):
    cache_len = ck_ref.shape[1] // N_HEADS
    for h in range(N_HEADS):
        q = q_ref[h]
        kc = ck_ref[0, pl.ds(h, cache_len, stride=N_HEADS), :].astype(BF16)
        vc = cv_ref[0, pl.ds(h, cache_len, stride=N_HEADS), :].astype(BF16)
        dims = (((1,), (1,)), ((), ()))
        sc = lax.dot_general(q, kc, dims, preferred_element_type=F32) * ATTN_SCALE + bc_ref[h]
        sn = lax.dot_general(q, kn_ref[h], dims, preferred_element_type=F32) * ATTN_SCALE + bn_ref[h]
        m = jnp.maximum(jnp.max(sc, axis=-1, keepdims=True), jnp.max(sn, axis=-1, keepdims=True))
        ec = jnp.exp(sc - m)
        en = jnp.exp(sn - m)
        inv = 1.0 / (jnp.sum(ec, axis=-1, keepdims=True) + jnp.sum(en, axis=-1, keepdims=True))
        o = jnp.dot((ec * inv).astype(BF16), vc, preferred_element_type=F32)
        o = o + jnp.dot((en * inv).astype(BF16), vn_ref[h], preferred_element_type=F32)
        o_ref[h] = o.astype(BF16)


def _attn_sample(q16, k16, v16, cache_k, cache_v, rel_bias, n_seq, seq_len):
    cache_len = cache_k.shape[1]
    ck = cache_k.reshape(n_seq, cache_len * N_HEADS, HEAD_DIM)
    cv = cache_v.reshape(n_seq, cache_len * N_HEADS, HEAD_DIM)
    a = jnp.arange(seq_len)[:, None]
    w = jnp.arange(cache_len + seq_len)[None, :]
    idx = jnp.clip(a - w + cache_len, -MAX_REL, MAX_REL) + MAX_REL
    bias = rel_bias[:, idx].astype(F32)
    bias_c = bias[:, :, :cache_len]
    bias_n = bias[:, :, cache_len:]
    new = pl.BlockSpec((N_HEADS, seq_len, HEAD_DIM), lambda b: (0, b, 0))
    cache = pl.BlockSpec((1, cache_len * N_HEADS, HEAD_DIM), lambda b: (b, 0, 0))
    return pl.pallas_call(
        _attn_sample_kernel,
        out_shape=jax.ShapeDtypeStruct(q16.shape, BF16),
        grid=(n_seq,),
        in_specs=[new, new, new, cache, cache,
                  pl.BlockSpec(bias_c.shape, lambda b: (0, 0, 0)),
                  pl.BlockSpec(bias_n.shape, lambda b: (0, 0, 0))],
        out_specs=new,
        compiler_params=_params(("parallel",)),
        name="attn_sample",
    )(q16, k16, v16, ck, cv, bias_c, bias_n)


def _trunk(x, conv_state, cache_k, cache_v, p, w, tm, tm_ffn):
    n_seq, seq_len, d = x.shape
    h = x.reshape(n_seq * seq_len, d)
    w16 = {}

    def ffn(h, layer, slot):
        key = "ffn%d%d" % (layer, slot)
        h, w16[key] = _ffn(h, p["ffn_norm"], (layer, slot), *w[key], tm_ffn)
        return h

    h = ffn(h, 0, 0)
    h, new_state, w16["mixer"] = _mixer(h, p["mix_norm"], (0,), *w["mixer"][:3], p["conv_w"], (0,),
                                        w["mixer"][3], conv_state[0], seq_len, tm)
    h = ffn(h, 0, 1)
    prompt = cache_k is None
    (k32, v32, k16, v16), w16["kv"] = _kv_proj(h, p["kv_norm"], *w["kv"], p["k_gain"], seq_len, tm,
                                               v_transposed=prompt)

    h = ffn(h, 1, 0)
    q16, w16["q"] = _q_proj(h, p["mix_norm"], (1,), *w["q"], p["q_gain"], (0,), tm)
    if prompt:
        att16 = _attn_prompt(q16, k16, v16, p["rel_bias"][0], n_seq, seq_len)
    else:
        att16 = _attn_sample(q16, k16, v16, cache_k, cache_v, p["rel_bias"][0], n_seq, seq_len)
    h, w16["o"] = _o_proj(h, att16, *w["o"], tm, att_transposed=prompt)
    h = ffn(h, 1, 1)
    keep = min(PAST_ROWS, seq_len)
    return (h.reshape(n_seq, seq_len, d), new_state[None],
            k32.reshape(n_seq, keep, N_HEADS, HEAD_DIM),
            v32.reshape(n_seq, keep, N_HEADS, HEAD_DIM), w16)


def kernel(x_prompt, x_sample, state_conv, cache_k, cache_v, ffn_norm, ffn_w_gate, ffn_w_up,
           ffn_w_down, mix_norm, conv_w_in, conv_w, conv_w_out, kv_norm, w_kv, k_gain,
           w_q, q_gain, rel_bias, w_o):
    depth = ffn_norm.shape[0]
    assert depth == 2 and conv_w_in.shape[0] == 1 and w_q.shape[0] == 1
    p = {
        "ffn_norm": ffn_norm.reshape(depth, 2, 1, D_MODEL),
        "mix_norm": mix_norm.reshape(depth, 1, D_MODEL),
        "conv_w": conv_w,
        "kv_norm": kv_norm.reshape(1, D_MODEL),
        "k_gain": k_gain.reshape(1, HEAD_DIM),
        "q_gain": q_gain.reshape(-1, 1, HEAD_DIM),
        "rel_bias": rel_bias,
    }
    mix_blocks = D_MODEL // MIX_COL_TILE
    proj_blocks = D_MODEL // CAST_COL_TILE
    w32 = {"ffn%d%d" % (layer, slot): (_weight(ffn_w_gate, (layer, slot)),
                                       _weight(ffn_w_up, (layer, slot)),
                                       _weight(ffn_w_down, (layer, slot)))
           for layer in range(depth) for slot in range(2)}
    w32["mixer"] = (_weight(conv_w_in, (0,)), _weight(conv_w_in, (0,), mix_blocks),
                    _weight(conv_w_in, (0,), 2 * mix_blocks), _weight(conv_w_out, (0,)))
    w32["kv"] = (_weight(w_kv), _weight(w_kv, (), proj_blocks))
    w32["q"] = (_weight(w_q, (0,)),)
    w32["o"] = (_weight(w_o, (0,)),)

    n_p, t_p, _ = x_prompt.shape
    n_s, t_s, _ = x_sample.shape
    y_s, conv_s, k_s, v_s, w16 = _trunk(x_sample, state_conv, cache_k, cache_v, p, w32,
                                        tm=n_s * t_s, tm_ffn=n_s * t_s)
    conv_zero = jnp.zeros((1, n_p, CONV_WIDTH - 1, D_MODEL), F32)
    y_p, conv_p, k_p, v_p, _ = _trunk(x_prompt, conv_zero, None, None, p, w16,
                                      tm=PROMPT_ROW_TILE, tm_ffn=PROMPT_FFN_ROW_TILE)
    return (y_p, y_s, conv_p, k_p, v_p, conv_s, k_s, v_s)
```

```python
import collections
import functools

import jax
import jax.numpy as jnp
from jax import lax
from jax.experimental import pallas as pl
from jax.experimental.pallas import tpu as pltpu

F32 = jnp.float32
BF16 = jnp.bfloat16

D_MODEL = 2048
HEAD_DIM = 128
N_HEADS = D_MODEL // HEAD_DIM
CHUNK = 64
N_PAST_CHUNKS = 8
PAST_ROWS = N_PAST_CHUNKS * CHUNK
MAX_REL = 128
CONV_WIDTH = 3
EPS = 1e-6
FFN_RES = 0.5
NEG_INF = -1e30
ATTN_SCALE = HEAD_DIM ** -0.5

SUBLANES = 8
V7X_VMEM_BYTES = 64 * 1024 * 1024
VMEM_LIMIT_BYTES = V7X_VMEM_BYTES - 8 * 1024 * 1024

PROMPT_ROW_TILE = 512
FFN_ROW_TILE = 1040
FFN_COL_TILE = 512
FFN_CAST_COL_TILE = 256
MIX_COL_TILE = 512
KV_COL_TILE = 1024
CAST_COL_TILE = 512
ATTN_Q_TILE = 512
ATTN_SUB = 256
ATTN_WIN = ATTN_SUB + PAST_ROWS
ATTN_HEADS_PER_STEP = 4

Weight = collections.namedtuple("Weight", ["arr", "lead", "col_off"])
Rows = collections.namedtuple("Rows", ["arr", "first"])
Dest = collections.namedtuple("Dest", ["rows", "first", "alias"])


def _weight(arr, lead=(), col_off=0):
    return Weight(arr, tuple(lead), col_off)


def _wspec(w, block, index):
    lead = w.lead

    def index_map(*ids):
        r, c = index(*ids)
        return lead + (r, c + w.col_off)

    return pl.BlockSpec((None,) * len(lead) + block, index_map)


def _lead_spec(lead, block):
    lead = tuple(lead)
    return pl.BlockSpec((None,) * len(lead) + block, lambda *ids: lead + (0,) * len(block))


def _is_f32(w):
    return w.arr.dtype == F32


def _bf16_weight(w_ref, w16_ref):
    if w16_ref is None:
        return w_ref[...]
    w16 = w_ref[...].astype(BF16)
    w16_ref[...] = w16
    return w16


def _dest_args(dest):
    if dest.alias is None:
        return [], [], {}
    return [pl.BlockSpec(memory_space=pl.ANY)], [dest.alias], {0: 0}


def _params(semantics):
    return pltpu.CompilerParams(dimension_semantics=semantics,
                                vmem_limit_bytes=VMEM_LIMIT_BYTES)


def _rms_rows(x, g):
    ms = jnp.mean(x * x, axis=-1, keepdims=True)
    return (x * lax.rsqrt(ms + EPS)) * g


def _ffn_kernel(*refs, emit, aliased):
    h_ref, g_ref, wg_ref, wu_ref, wd_ref, out_ref, *rest = refs[aliased:]
    wg16_ref, wu16_ref, wd16_ref = rest[:3] if emit else (None, None, None)
    xn_ref = rest[-1]
    f = pl.program_id(1)

    @pl.when(f == 0)
    def _():
        h = h_ref[...]
        xn_ref[...] = _rms_rows(h, g_ref[...]).astype(BF16)
        out_ref[...] = h

    xn = xn_ref[...]
    gate = jnp.dot(xn, _bf16_weight(wg_ref, wg16_ref), preferred_element_type=F32)
    up = jnp.dot(xn, _bf16_weight(wu_ref, wu16_ref), preferred_element_type=F32)
    act = (gate * jax.nn.sigmoid(gate) * (FFN_RES * up)).astype(BF16)
    out_ref[...] += jnp.dot(act, _bf16_weight(wd_ref, wd16_ref), preferred_element_type=F32)


def _ffn(src, n_tiles, dest, norm_g, g_lead, wg, wu, wd, tm):
    d = src.arr.shape[1]
    d_ff = wg.arr.shape[-1]
    emit = _is_f32(wg)
    tf = FFN_CAST_COL_TILE if emit else FFN_COL_TILE
    out_shape = [jax.ShapeDtypeStruct((dest.rows, d), F32)]
    out_specs = [pl.BlockSpec((tm, d), lambda i, f: (dest.first + i, 0))]
    h_mode = {}
    if emit:
        assert n_tiles == 1
        out_shape += [jax.ShapeDtypeStruct((d, d_ff), BF16), jax.ShapeDtypeStruct((d, d_ff), BF16),
                      jax.ShapeDtypeStruct((d_ff, d), BF16)]
        out_specs += [pl.BlockSpec((d, tf), lambda i, f: (0, f)),
                      pl.BlockSpec((d, tf), lambda i, f: (0, f)),
                      pl.BlockSpec((tf, d), lambda i, f: (f, 0))]
        h_mode = dict(pipeline_mode=pl.Buffered(1))
    alias_specs, alias_args, aliases = _dest_args(dest)
    res = pl.pallas_call(
        functools.partial(_ffn_kernel, emit=emit, aliased=len(alias_args)),
        out_shape=out_shape,
        grid=(n_tiles, d_ff // tf),
        in_specs=alias_specs + [
            pl.BlockSpec((tm, d), lambda i, f: (src.first + i, 0), **h_mode),
            _lead_spec(g_lead, (1, d)),
            _wspec(wg, (d, tf), lambda i, f: (0, f)),
            _wspec(wu, (d, tf), lambda i, f: (0, f)),
            _wspec(wd, (tf, d), lambda i, f: (f, 0)),
        ],
        out_specs=out_specs,
        scratch_shapes=[pltpu.VMEM((tm, d), BF16)],
        input_output_aliases=aliases,
        compiler_params=_params(("parallel", "arbitrary")),
        name="ffn",
    )(*alias_args, src.arr, norm_g, wg.arr, wu.arr, wd.arr)
    return res[0], tuple(_weight(a) for a in res[1:])


def _mixer_kernel(*refs, seq_len, tm, emit, aliased):
    (h_ref, g_ref, wb_ref, wc_ref, wx_ref, cw_ref, wo_ref, p0_ref, p1_ref,
     out_ref, st_ref, *rest) = refs[aliased:]
    wb16_ref, wc16_ref, wx16_ref, wo16_ref = rest[:4] if emit else (None,) * 4
    xn_ref, carry_ref = rest[-2:]
    i = pl.program_id(0)
    j = pl.program_id(1)

    @pl.when(j == 0)
    def _():
        h = h_ref[...]
        xn_ref[...] = _rms_rows(h, g_ref[...]).astype(BF16)
        out_ref[...] = h

    if seq_len >= tm:
        @pl.when((i % (seq_len // tm)) == 0)
        def _():
            carry_ref[j] = p0_ref[0]

    xn = xn_ref[...]
    b = jnp.dot(xn, _bf16_weight(wb_ref, wb16_ref), preferred_element_type=F32)
    c = jnp.dot(xn, _bf16_weight(wc_ref, wc16_ref), preferred_element_type=F32)
    x = jnp.dot(xn, _bf16_weight(wx_ref, wx16_ref), preferred_element_type=F32)
    u = c * x
    tn = u.shape[1]
    row = lax.broadcasted_iota(jnp.int32, (tm, tn), 0)
    if seq_len >= tm:
        prev = carry_ref[j]
        p0 = prev[0:1, :]
        p1 = prev[1:2, :]
        pos = row
        carry_ref[j] = u[tm - 2:tm, :]
        st_ref[0] = u[tm - 2:tm, :]
    else:
        p0 = p0_ref[...]
        p1 = p1_ref[...]
        pos = row % seq_len
        u3 = u.reshape(tm // seq_len, seq_len, tn)
        st_ref[...] = u3[:, seq_len - 2:seq_len, :]
    um1 = jnp.where(pos == 0, p1, pltpu.roll(u, 1, 0))
    um2 = jnp.where(pos == 0, p0, jnp.where(pos == 1, p1, pltpu.roll(u, 2, 0)))
    cw = cw_ref[...]
    conv = cw[0:1, :] * um2 + cw[1:2, :] * um1 + cw[2:3, :] * u
    v = (b * conv).astype(BF16)
    out_ref[...] += jnp.dot(v, _bf16_weight(wo_ref, wo16_ref), preferred_element_type=F32)


def _mixer(src, rows, dest, norm_g, g_lead, wb, wc, wx, conv_w, cw_lead, wo, state, seq_len, tm):
    d = src.arr.shape[1]
    n_seq = rows // seq_len
    tn = MIX_COL_TILE
    n_j = d // tn
    n_i = rows // tm
    emit = _is_f32(wb)
    if seq_len >= tm:
        tiles_per_seq = seq_len // tm
        p0, p1 = state, state
        p_spec = pl.BlockSpec((1, CONV_WIDTH - 1, tn), lambda i, j: (i // tiles_per_seq, 0, j))
        st_shape = jax.ShapeDtypeStruct((n_i, CONV_WIDTH - 1, d), F32)
        st_spec = pl.BlockSpec((1, CONV_WIDTH - 1, tn), lambda i, j: (i, 0, j))
    else:
        assert tm == rows and tm % seq_len == 0
        p0 = jnp.repeat(state[:, 0], seq_len, axis=0)
        p1 = jnp.repeat(state[:, 1], seq_len, axis=0)
        p_spec = pl.BlockSpec((tm, tn), lambda i, j: (i, j))
        st_shape = jax.ShapeDtypeStruct((n_seq, CONV_WIDTH - 1, d), F32)
        st_spec = pl.BlockSpec((n_seq, CONV_WIDTH - 1, tn), lambda i, j: (0, 0, j))
    out_shape = [jax.ShapeDtypeStruct((dest.rows, d), F32), st_shape]
    out_specs = [pl.BlockSpec((tm, d), lambda i, j: (dest.first + i, 0)), st_spec]
    if emit:
        assert n_i == 1
        out_shape += [jax.ShapeDtypeStruct((d, d), BF16)] * 4
        out_specs += [pl.BlockSpec((d, tn), lambda i, j: (0, j))] * 3
        out_specs += [pl.BlockSpec((tn, d), lambda i, j: (j, 0))]
    alias_specs, alias_args, aliases = _dest_args(dest)
    res = pl.pallas_call(
        functools.partial(_mixer_kernel, seq_len=seq_len, tm=tm, emit=emit,
                          aliased=len(alias_args)),
        out_shape=out_shape,
        grid=(n_i, n_j),
        in_specs=alias_specs + [
            pl.BlockSpec((tm, d), lambda i, j: (src.first + i, 0)),
            _lead_spec(g_lead, (1, d)),
            _wspec(wb, (d, tn), lambda i, j: (0, j)),
            _wspec(wc, (d, tn), lambda i, j: (0, j)),
            _wspec(wx, (d, tn), lambda i, j: (0, j)),
            pl.BlockSpec((None,) * len(cw_lead) + (CONV_WIDTH, tn),
                         lambda i, j: tuple(cw_lead) + (0, j)),
            _wspec(wo, (tn, d), lambda i, j: (j, 0)),
            p_spec,
            p_spec,
        ],
        out_specs=out_specs,
        scratch_shapes=[pltpu.VMEM((tm, d), BF16),
                        pltpu.VMEM((n_j, CONV_WIDTH - 1, tn), F32)],
        input_output_aliases=aliases,
        compiler_params=_params(("arbitrary", "arbitrary")),
        name="conv_mixer",
    )(*alias_args, src.arr, norm_g, wb.arr, wc.arr, wx.arr, conv_w, wo.arr, p0, p1)
    out, st = res[0], res[1]
    if seq_len >= tm:
        st = st[seq_len // tm - 1::seq_len // tm]
    return out, st, tuple(_weight(a) for a in res[2:])


def _kv_kernel(h_ref, g_ref, wk_ref, wv_ref, kg_ref, k32_ref, v32_ref, k16_ref, v16_ref,
               *rest, emit, keep, v_transposed):
    wk16_ref, wv16_ref = rest if emit else (None, None)
    xn = _rms_rows(h_ref[...], g_ref[...]).astype(BF16)
    k = jnp.dot(xn, _bf16_weight(wk_ref, wk16_ref), preferred_element_type=F32)
    v = jnp.dot(xn, _bf16_weight(wv_ref, wv16_ref), preferred_element_type=F32)
    kg = kg_ref[...]
    tm = k.shape[0]
    v32_ref[...] = v[tm - keep:, :]
    for hd in range(k.shape[1] // HEAD_DIM):
        cols = slice(hd * HEAD_DIM, (hd + 1) * HEAD_DIM)
        kh = _rms_rows(k[:, cols], kg)
        k32_ref[:, cols] = kh[tm - keep:, :]
        k16_ref[hd] = kh.astype(BF16)
        if v_transposed:
            v16_ref[hd] = v[:, cols].T.astype(BF16)
        else:
            v16_ref[hd] = v[:, cols].astype(BF16)


def _kv_proj(src, rows, norm_g, wk, wv, k_gain, seq_len, tm, v_transposed):
    d = src.arr.shape[1]
    emit = _is_f32(wk)
    tn = CAST_COL_TILE if emit else KV_COL_TILE
    hpb = tn // HEAD_DIM
    if seq_len >= tm:
        keep = min(PAST_ROWS, seq_len)
        assert keep <= tm and seq_len % tm == 0
        tiles_per_seq = seq_len // tm
        kept_rows = (rows // seq_len) * keep
        kept_spec = pl.BlockSpec((keep, tn), lambda j, i: (i // tiles_per_seq, j))
    else:
        assert tm % seq_len == 0
        keep, kept_rows = tm, rows
        kept_spec = pl.BlockSpec((tm, tn), lambda j, i: (i, j))
    if v_transposed:
        v16_shape = jax.ShapeDtypeStruct((N_HEADS, HEAD_DIM, rows), BF16)
        v16_spec = pl.BlockSpec((hpb, HEAD_DIM, tm), lambda j, i: (j, 0, i))
    else:
        v16_shape = jax.ShapeDtypeStruct((N_HEADS, rows, HEAD_DIM), BF16)
        v16_spec = pl.BlockSpec((hpb, tm, HEAD_DIM), lambda j, i: (j, i, 0))
    out_shape = [jax.ShapeDtypeStruct((kept_rows, d), F32),
                 jax.ShapeDtypeStruct((kept_rows, d), F32),
                 jax.ShapeDtypeStruct((N_HEADS, rows, HEAD_DIM), BF16), v16_shape]
    out_specs = [kept_spec, kept_spec,
                 pl.BlockSpec((hpb, tm, HEAD_DIM), lambda j, i: (j, i, 0)), v16_spec]
    if emit:
        assert rows == tm
        out_shape += [jax.ShapeDtypeStruct((d, d), BF16)] * 2
        out_specs += [pl.BlockSpec((d, tn), lambda j, i: (0, j))] * 2
    res = pl.pallas_call(
        functools.partial(_kv_kernel, emit=emit, keep=keep, v_transposed=v_transposed),
        out_shape=out_shape,
        grid=(d // tn, rows // tm),
        in_specs=[
            pl.BlockSpec((tm, d), lambda j, i: (src.first + i, 0)),
            pl.BlockSpec((1, d), lambda j, i: (0, 0)),
            _wspec(wk, (d, tn), lambda j, i: (0, j)),
            _wspec(wv, (d, tn), lambda j, i: (0, j)),
            pl.BlockSpec((1, HEAD_DIM), lambda j, i: (0, 0)),
        ],
        out_specs=out_specs,
        compiler_params=_params(("arbitrary", "arbitrary")),
        name="kv_proj",
    )(src.arr, norm_g, wk.arr, wv.arr, k_gain)
    return res[:4], tuple(_weight(a) for a in res[4:])


def _q_kernel(h_ref, g_ref, wq_ref, qg_ref, q16_ref, *rest, emit):
    wq16_ref = rest[0] if emit else None
    xn_ref = rest[-1]

    @pl.when(pl.program_id(1) == 0)
    def _():
        xn_ref[...] = _rms_rows(h_ref[...], g_ref[...]).astype(BF16)

    q = jnp.dot(xn_ref[...], _bf16_weight(wq_ref, wq16_ref), preferred_element_type=F32)
    qg = qg_ref[...]
    for hd in range(q.shape[1] // HEAD_DIM):
        cols = slice(hd * HEAD_DIM, (hd + 1) * HEAD_DIM)
        q16_ref[hd] = _rms_rows(q[:, cols], qg).astype(BF16)


def _q_proj(src, rows, norm_g, g_lead, wq, q_gain, qg_lead, tm):
    d = src.arr.shape[1]
    emit = _is_f32(wq)
    tn = CAST_COL_TILE if emit else d
    hpb = tn // HEAD_DIM
    out_shape = [jax.ShapeDtypeStruct((N_HEADS, rows, HEAD_DIM), BF16)]
    out_specs = [pl.BlockSpec((hpb, tm, HEAD_DIM), lambda i, j: (j, i, 0))]
    if emit:
        assert rows == tm
        out_shape += [jax.ShapeDtypeStruct((d, d), BF16)]
        out_specs += [pl.BlockSpec((d, tn), lambda i, j: (0, j))]
    res = pl.pallas_call(
        functools.partial(_q_kernel, emit=emit),
        out_shape=out_shape,
        grid=(rows // tm, d // tn),
        in_specs=[
            pl.BlockSpec((tm, d), lambda i, j: (src.first + i, 0)),
            _lead_spec(g_lead, (1, d)),
            _wspec(wq, (d, tn), lambda i, j: (0, j)),
            _lead_spec(qg_lead, (1, HEAD_DIM)),
        ],
        out_specs=out_specs,
        scratch_shapes=[pltpu.VMEM((tm, d), BF16)],
        compiler_params=_params(("parallel", "arbitrary")),
        name="q_proj",
    )(src.arr, norm_g, wq.arr, q_gain)
    return res[0], tuple(_weight(a) for a in res[1:])


def _o_kernel(*refs, emit, att_transposed, aliased):
    h_ref, att_ref, wo_ref, out_ref, *rest = refs[aliased:]
    wo16_ref = rest[0] if emit else None
    wo = _bf16_weight(wo_ref, wo16_ref)
    if att_transposed:
        att_t = att_ref[...].reshape(D_MODEL, att_ref.shape[2])
        y = lax.dot_general(att_t, wo, (((0,), (0,)), ((), ())), preferred_element_type=F32)
    else:
        att2d_ref = rest[-1]

        @pl.when(pl.program_id(1) == 0)
        def _():
            for hd in range(N_HEADS):
                att2d_ref[:, hd * HEAD_DIM:(hd + 1) * HEAD_DIM] = att_ref[hd]

        y = jnp.dot(att2d_ref[...], wo, preferred_element_type=F32)
    out_ref[...] = h_ref[...] + y


def _o_proj(src, rows, dest, att16, wo, tm, att_transposed):
    d = src.arr.shape[1]
    emit = _is_f32(wo)
    tn = CAST_COL_TILE if emit else d
    out_shape = [jax.ShapeDtypeStruct((dest.rows, d), F32)]
    out_specs = [pl.BlockSpec((tm, tn), lambda i, j: (dest.first + i, j))]
    if emit:
        assert rows == tm
        out_shape += [jax.ShapeDtypeStruct((d, d), BF16)]
        out_specs += [pl.BlockSpec((d, tn), lambda i, j: (0, j))]
    if att_transposed:
        att_spec = pl.BlockSpec((N_HEADS, HEAD_DIM, tm), lambda i, j: (0, 0, i))
        scratch = []
    else:
        att_spec = pl.BlockSpec((N_HEADS, tm, HEAD_DIM), lambda i, j: (0, i, 0))
        scratch = [pltpu.VMEM((tm, d), BF16)]
    alias_specs, alias_args, aliases = _dest_args(dest)
    res = pl.pallas_call(
        functools.partial(_o_kernel, emit=emit, att_transposed=att_transposed,
                          aliased=len(alias_args)),
        out_shape=out_shape,
        grid=(rows // tm, d // tn),
        in_specs=alias_specs + [
            pl.BlockSpec((tm, tn), lambda i, j: (src.first + i, j)),
            att_spec,
            _wspec(wo, (d, tn), lambda i, j: (0, j)),
        ],
        out_specs=out_specs,
        scratch_shapes=scratch,
        input_output_aliases=aliases,
        compiler_params=_params(("parallel", "arbitrary")),
        name="o_proj",
    )(*alias_args, src.arr, att16, wo.arr)
    return res[0], tuple(_weight(a) for a in res[1:])


def _attn_prompt_kernel(q_ref, kp_ref, kc_ref, vp_ref, vc_ref, bias_ref, o_ref,
                        s_refs, p_refs):
    i = pl.program_id(1)
    n_before_start = jnp.where(i == 0, ATTN_Q_TILE, 0)
    n_chunks = ATTN_WIN // CHUNK
    groups = CHUNK // SUBLANES

    subs = ATTN_Q_TILE // ATTN_SUB
    n_units = ATTN_HEADS_PER_STEP * subs

    def scores(h0, u):
        h, lo = h0 + u // subs, (u % subs) * ATTN_SUB
        q = q_ref[h, lo:lo + ATTN_SUB, :]
        n_past = ATTN_Q_TILE - lo
        dims = (((1,), (1,)), ((), ()))
        s_refs[u, 0:n_past, :] = lax.dot_general(kp_ref[h, lo:, :], q, dims,
                                                 preferred_element_type=F32)
        s_refs[u, n_past:, :] = lax.dot_general(kc_ref[h, 0:ATTN_WIN - n_past, :], q, dims,
                                                preferred_element_type=F32)

    def softmax_and_values(h0, u):
        h, lo = h0 + u // subs, (u % subs) * ATTN_SUB
        s_ref, p_ref = s_refs.at[u], p_refs.at[u]
        m8 = jnp.full((SUBLANES, ATTN_SUB), NEG_INF, F32)
        for c in range(n_chunks):
            rows = slice(c * CHUNK, (c + 1) * CHUNK)
            before_start = c * CHUNK < n_before_start - lo
            x = s_ref[rows, :] * ATTN_SCALE + bias_ref[h, rows, :]
            x = jnp.where(before_start, NEG_INF, x)
            s_ref[rows, :] = x
            m8 = jnp.maximum(m8, jnp.max(x.reshape(groups, SUBLANES, ATTN_SUB), axis=0))
        m = jnp.max(m8, axis=0, keepdims=True)
        l8 = jnp.zeros((SUBLANES, ATTN_SUB), F32)
        for c in range(n_chunks):
            rows = slice(c * CHUNK, (c + 1) * CHUNK)
            e = jnp.exp(s_ref[rows, :] - m)
            p_ref[rows, :] = e.astype(BF16)
            l8 = l8 + jnp.sum(e.reshape(groups, SUBLANES, ATTN_SUB), axis=0)
        inv = 1.0 / jnp.sum(l8, axis=0, keepdims=True)
        n_past = ATTN_Q_TILE - lo
        o = jnp.dot(vp_ref[h, :, lo:], p_ref[0:n_past, :], preferred_element_type=F32)
        o = o + jnp.dot(vc_ref[h, :, 0:ATTN_WIN - n_past], p_ref[n_past:, :],
                        preferred_element_type=F32)
        o_ref[h, :, lo:lo + ATTN_SUB] = (o * inv).astype(BF16)

    def step(t, carry):
        h0 = t * ATTN_HEADS_PER_STEP
        scores(h0, 0)
        for u in range(n_units):
            if u + 1 < n_units:
                scores(h0, u + 1)
            softmax_and_values(h0, u)
        return carry

    lax.fori_loop(0, N_HEADS // ATTN_HEADS_PER_STEP, step, 0)


def _prompt_bias(rel_bias):
    period = pl.next_power_of_2(ATTN_SUB + ATTN_WIN - 1)
    j = jnp.arange(period)
    k = jnp.where(j < ATTN_WIN, j, j - period)
    idx = jnp.clip(PAST_ROWS - k, -MAX_REL, MAX_REL) + MAX_REL
    vec = rel_bias[:, idx].astype(F32)
    flat = jnp.tile(vec, (1, ATTN_SUB))[:, :ATTN_SUB * (period - 1)]
    toeplitz = flat.reshape(N_HEADS, ATTN_SUB, period - 1)[:, :, :ATTN_WIN]
    qc = jnp.arange(ATTN_SUB)[:, None] // CHUNK
    kc = jnp.arange(ATTN_WIN)[None, :] // CHUNK
    band = (kc >= qc) & (kc <= qc + N_PAST_CHUNKS)
    return jnp.swapaxes(jnp.where(band[None], toeplitz, NEG_INF), 1, 2)


def _attn_prompt(q16, k16, v16t, rel_bias, n_seq, seq_len):
    tiles = seq_len // ATTN_Q_TILE
    cur = lambda b, i: b * tiles + i
    past = lambda b, i: b * tiles + jnp.maximum(i - 1, 0)
    rows = lambda at: pl.BlockSpec((N_HEADS, ATTN_Q_TILE, HEAD_DIM), lambda b, i: (0, at(b, i), 0))
    cols = lambda at: pl.BlockSpec((N_HEADS, HEAD_DIM, ATTN_Q_TILE), lambda b, i: (0, 0, at(b, i)))
    bias = _prompt_bias(rel_bias)
    n_units = ATTN_HEADS_PER_STEP * (ATTN_Q_TILE // ATTN_SUB)
    return pl.pallas_call(
        _attn_prompt_kernel,
        out_shape=jax.ShapeDtypeStruct(v16t.shape, BF16),
        grid=(n_seq, tiles),
        in_specs=[rows(cur), rows(past), rows(cur), cols(past), cols(cur),
                  pl.BlockSpec(bias.shape, lambda b, i: (0, 0, 0))],
        out_specs=cols(cur),
        scratch_shapes=[pltpu.VMEM((n_units, ATTN_WIN, ATTN_SUB), F32),
                        pltpu.VMEM((n_units, ATTN_WIN, ATTN_SUB), BF16)],
        compiler_params=_params(("parallel", "arbitrary")),
        name="attn_prompt",
    )(q16, k16, k16, v16t, v16t, bias)


def _attn_sample_kernel(q_ref, kn_ref, vn_ref, ck_ref, cv_ref, bc_ref, bn_ref, o_ref):
    cache_len = ck_ref.shape[1] // N_HEADS
    for h in range(N_HEADS):
        q = q_ref[h]
        kc = ck_ref[0, pl.ds(h, cache_len, stride=N_HEADS), :].astype(BF16)
        vc = cv_ref[0, pl.ds(h, cache_len, stride=N_HEADS), :].astype(BF16)
        dims = (((1,), (1,)), ((), ()))
        sc = lax.dot_general(q, kc, dims, preferred_element_type=F32) * ATTN_SCALE + bc_ref[h]
        sn = lax.dot_general(q, kn_ref[h], dims, preferred_element_type=F32) * ATTN_SCALE + bn_ref[h]
        m = jnp.maximum(jnp.max(sc, axis=-1, keepdims=True), jnp.max(sn, axis=-1, keepdims=True))
        ec = jnp.exp(sc - m)
        en = jnp.exp(sn - m)
        inv = 1.0 / (jnp.sum(ec, axis=-1, keepdims=True) + jnp.sum(en, axis=-1, keepdims=True))
        o = jnp.dot((ec * inv).astype(BF16), vc, preferred_element_type=F32)
        o = o + jnp.dot((en * inv).astype(BF16), vn_ref[h], preferred_element_type=F32)
        o_ref[h] = o.astype(BF16)


def _attn_sample(q16, k16, v16, cache_k, cache_v, rel_bias, n_seq, seq_len):
    cache_len = cache_k.shape[1]
    ck = cache_k.reshape(n_seq, cache_len * N_HEADS, HEAD_DIM)
    cv = cache_v.reshape(n_seq, cache_len * N_HEADS, HEAD_DIM)
    a = jnp.arange(seq_len)[:, None]
    w = jnp.arange(cache_len + seq_len)[None, :]
    idx = jnp.clip(a - w + cache_len, -MAX_REL, MAX_REL) + MAX_REL
    bias = rel_bias[:, idx].astype(F32)
    bias_c = bias[:, :, :cache_len]
    bias_n = bias[:, :, cache_len:]
    new = pl.BlockSpec((N_HEADS, seq_len, HEAD_DIM), lambda b: (0, b, 0))
    cache = pl.BlockSpec((1, cache_len * N_HEADS, HEAD_DIM), lambda b: (b, 0, 0))
    return pl.pallas_call(
        _attn_sample_kernel,
        out_shape=jax.ShapeDtypeStruct(q16.shape, BF16),
        grid=(n_seq,),
        in_specs=[new, new, new, cache, cache,
                  pl.BlockSpec(bias_c.shape, lambda b: (0, 0, 0)),
                  pl.BlockSpec(bias_n.shape, lambda b: (0, 0, 0))],
        out_specs=new,
        compiler_params=_params(("parallel",)),
        name="attn_sample",
    )(q16, k16, v16, ck, cv, bias_c, bias_n)


def kernel(x_prompt, x_sample, state_conv, cache_k, cache_v, ffn_norm, ffn_w_gate, ffn_w_up,
           ffn_w_down, mix_norm, conv_w_in, conv_w, conv_w_out, kv_norm, w_kv, k_gain,
           w_q, q_gain, rel_bias, w_o):
    depth = ffn_norm.shape[0]
    assert depth == 2 and conv_w_in.shape[0] == 1 and w_q.shape[0] == 1
    d = D_MODEL
    n_p, t_p, _ = x_prompt.shape
    n_s, t_s, _ = x_sample.shape
    m_p, m_s = n_p * t_p, n_s * t_s
    m_all = m_p + m_s
    tm_p, tm_f = PROMPT_ROW_TILE, FFN_ROW_TILE
    assert m_all % tm_f == 0 and m_s < tm_f and m_p % m_s == 0 and m_p % tm_p == 0
    ffn_tiles = m_all // tm_f
    s_first = m_p // m_s

    ffn_norm4 = ffn_norm.reshape(depth, 2, 1, d)
    mix_norm3 = mix_norm.reshape(depth, 1, d)
    kv_norm2 = kv_norm.reshape(1, d)
    k_gain2 = k_gain.reshape(1, HEAD_DIM)
    q_gain3 = q_gain.reshape(-1, 1, HEAD_DIM)

    def ffn(tail, main, layer, slot):
        w32 = [_weight(w, (layer, slot)) for w in (ffn_w_gate, ffn_w_up, ffn_w_down)]
        h, w16 = _ffn(tail, 1, Dest(m_all, ffn_tiles - 1, None), ffn_norm4, (layer, slot),
                      *w32, tm_f)
        h, _ = _ffn(main, ffn_tiles - 1, Dest(m_all, 0, h), ffn_norm4, (layer, slot),
                    *w16, tm_f)
        return h

    def both(h):
        return Rows(h, ffn_tiles - 1), Rows(h, 0)

    xp = x_prompt.reshape(m_p, d)
    x_tail = jnp.concatenate([xp[m_p - (tm_f - m_s):], x_sample.reshape(m_s, d)], axis=0)
    h = ffn(Rows(x_tail, 0), Rows(xp, 0), 0, 0)

    mix_blocks = d // MIX_COL_TILE
    w_mix32 = (_weight(conv_w_in, (0,)), _weight(conv_w_in, (0,), mix_blocks),
               _weight(conv_w_in, (0,), 2 * mix_blocks))
    h2, conv_s, w_mix16 = _mixer(Rows(h, s_first), m_s, Dest(m_all, s_first, None), mix_norm3, (0,),
                                 *w_mix32, conv_w, (0,), _weight(conv_w_out, (0,)),
                                 state_conv[0], t_s, m_s)
    conv_zero = jnp.zeros((n_p, CONV_WIDTH - 1, d), F32)
    h, conv_p, _ = _mixer(Rows(h, 0), m_p, Dest(m_all, 0, h2), mix_norm3, (0,),
                          *w_mix16[:3], conv_w, (0,), w_mix16[3], conv_zero, t_p, tm_p)

    h = ffn(*both(h), 0, 1)

    proj_blocks = d // CAST_COL_TILE
    (k_s, v_s, k16_s, v16_s), w_kv16 = _kv_proj(
        Rows(h, s_first), m_s, kv_norm2, _weight(w_kv), _weight(w_kv, (), proj_blocks), k_gain2,
        t_s, m_s, v_transposed=False)
    (k_p, v_p, k16_p, v16t_p), _ = _kv_proj(
        Rows(h, 0), m_p, kv_norm2, *w_kv16, k_gain2, t_p, tm_p, v_transposed=True)

    h = ffn(*both(h), 1, 0)

    q16_s, w_q16 = _q_proj(Rows(h, s_first), m_s, mix_norm3, (1,), _weight(w_q, (0,)),
                           q_gain3, (0,), m_s)
    q16_p, _ = _q_proj(Rows(h, 0), m_p, mix_norm3, (1,), *w_q16, q_gain3, (0,), tm_p)
    att_s = _attn_sample(q16_s, k16_s, v16_s, cache_k, cache_v, rel_bias[0], n_s, t_s)
    att_p = _attn_prompt(q16_p, k16_p, v16t_p, rel_bias[0], n_p, t_p)
    h2, w_o16 = _o_proj(Rows(h, s_first), m_s, Dest(m_all, s_first, None), att_s,
                        _weight(w_o, (0,)), m_s, att_transposed=False)
    h, _ = _o_proj(Rows(h, 0), m_p, Dest(m_all, 0, h2), att_p, *w_o16, tm_p, att_transposed=True)

    h = ffn(*both(h), 1, 1)

    keep_p, keep_s = min(PAST_ROWS, t_p), min(PAST_ROWS, t_s)
    heads = (N_HEADS, HEAD_DIM)
    return (h[:m_p].reshape(n_p, t_p, d), h[m_p:].reshape(n_s, t_s, d), conv_p[None],
            k_p.reshape(n_p, keep_p, *heads), v_p.reshape(n_p, keep_p, *heads), conv_s[None],
            k_s.reshape(n_s, keep_s, *heads), v_s.reshape(n_s, keep_s, *heads))
```

```python
import collections
import functools

import jax
import jax.numpy as jnp
from jax import lax
from jax.experimental import pallas as pl
from jax.experimental.pallas import tpu as pltpu

F32 = jnp.float32
BF16 = jnp.bfloat16

D_MODEL = 2048
HEAD_DIM = 128
N_HEADS = D_MODEL // HEAD_DIM
CHUNK = 64
N_PAST_CHUNKS = 8
PAST_ROWS = N_PAST_CHUNKS * CHUNK
MAX_REL = 128
CONV_WIDTH = 3
EPS = 1e-6
FFN_RES = 0.5
NEG_INF = -1e30
ATTN_SCALE = HEAD_DIM ** -0.5
LOG2_E = 1.4426950408889634

SUBLANES = 8
V7X_VMEM_BYTES = 64 * 1024 * 1024
VMEM_LIMIT_BYTES = V7X_VMEM_BYTES - 8 * 1024 * 1024

PROMPT_ROW_TILE = 512
FFN_ROW_TILE = 1040
FFN_COL_TILE = 512
FFN_CAST_COL_TILE = 256
MIX_COL_TILE = 512
PROMPT_MIX_ROW_TILE = 1024
PROMPT_MIX_PARTS = 2
PROMPT_MIX_COL_TILE = 256
KV_COL_TILE = 1024
CAST_COL_TILE = 512
ATTN_Q_TILE = 512
ATTN_SUB = 256
ATTN_WIN = ATTN_SUB + PAST_ROWS
ATTN_HEADS_PER_STEP = 4

Weight = collections.namedtuple("Weight", ["arr", "lead", "col_off"])
Rows = collections.namedtuple("Rows", ["arr", "first"])
Dest = collections.namedtuple("Dest", ["rows", "first", "alias"])


def _weight(arr, lead=(), col_off=0):
    return Weight(arr, tuple(lead), col_off)


def _wspec(w, block, index):
    lead = w.lead

    def index_map(*ids):
        r, c = index(*ids)
        return lead + (r, c + w.col_off)

    return pl.BlockSpec((None,) * len(lead) + block, index_map)


def _lead_spec(lead, block):
    lead = tuple(lead)
    return pl.BlockSpec((None,) * len(lead) + block, lambda *ids: lead + (0,) * len(block))


def _is_f32(w):
    return w.arr.dtype == F32


def _bf16_weight(w_ref, w16_ref):
    if w16_ref is None:
        return w_ref[...]
    w16 = w_ref[...].astype(BF16)
    w16_ref[...] = w16
    return w16


def _dest_args(dest):
    if dest.alias is None:
        return [], [], {}
    return [pl.BlockSpec(memory_space=pl.ANY)], [dest.alias], {0: 0}


def _params(semantics):
    return pltpu.CompilerParams(dimension_semantics=semantics,
                                vmem_limit_bytes=VMEM_LIMIT_BYTES)


def _rms_rows(x, g):
    ms = jnp.mean(x * x, axis=-1, keepdims=True)
    return (x * lax.rsqrt(ms + EPS)) * g


def _ffn_kernel(*refs, emit, aliased, n_split):
    h_ref, g_ref, wg_ref, wu_ref, wd_ref, out_ref, *rest = refs[aliased:]
    split_ref = rest.pop(0) if n_split else None
    wg16_ref, wu16_ref, wd16_ref = rest[:3] if emit else (None, None, None)
    xn_ref = rest[-1]
    f = pl.program_id(1)

    @pl.when(f == 0)
    def _():
        h = h_ref[...]
        xn_ref[...] = _rms_rows(h, g_ref[...]).astype(BF16)
        out_ref[...] = h

    xn = xn_ref[...]
    gate = jnp.dot(xn, _bf16_weight(wg_ref, wg16_ref), preferred_element_type=F32)
    up = jnp.dot(xn, _bf16_weight(wu_ref, wu16_ref), preferred_element_type=F32)
    act = (gate * jax.nn.sigmoid(gate) * (FFN_RES * up)).astype(BF16)
    out_ref[...] += jnp.dot(act, _bf16_weight(wd_ref, wd16_ref), preferred_element_type=F32)

    if n_split:
        @pl.when(f == pl.num_programs(1) - 1)
        def _():
            split_ref[...] = out_ref[out_ref.shape[0] - n_split:, :]


def _ffn(src, n_tiles, dest, norm_g, g_lead, wg, wu, wd, tm, n_split=0):
    d = src.arr.shape[1]
    d_ff = wg.arr.shape[-1]
    emit = _is_f32(wg)
    tf = FFN_CAST_COL_TILE if emit else FFN_COL_TILE
    out_shape = [jax.ShapeDtypeStruct((dest.rows, d), F32)]
    out_specs = [pl.BlockSpec((tm, d), lambda i, f: (dest.first + i, 0))]
    if n_split:
        assert n_tiles == 1
        out_shape += [jax.ShapeDtypeStruct((n_split, d), F32)]
        out_specs += [pl.BlockSpec((n_split, d), lambda i, f: (0, 0))]
    h_mode = {}
    if emit:
        assert n_tiles == 1
        out_shape += [jax.ShapeDtypeStruct((d, d_ff), BF16), jax.ShapeDtypeStruct((d, d_ff), BF16),
                      jax.ShapeDtypeStruct((d_ff, d), BF16)]
        out_specs += [pl.BlockSpec((d, tf), lambda i, f: (0, f)),
                      pl.BlockSpec((d, tf), lambda i, f: (0, f)),
                      pl.BlockSpec((tf, d), lambda i, f: (f, 0))]
        h_mode = dict(pipeline_mode=pl.Buffered(1))
    alias_specs, alias_args, aliases = _dest_args(dest)
    res = pl.pallas_call(
        functools.partial(_ffn_kernel, emit=emit, aliased=len(alias_args), n_split=n_split),
        out_shape=out_shape,
        grid=(n_tiles, d_ff // tf),
        in_specs=alias_specs + [
            pl.BlockSpec((tm, d), lambda i, f: (src.first + i, 0), **h_mode),
            _lead_spec(g_lead, (1, d)),
            _wspec(wg, (d, tf), lambda i, f: (0, f)),
            _wspec(wu, (d, tf), lambda i, f: (0, f)),
            _wspec(wd, (tf, d), lambda i, f: (f, 0)),
        ],
        out_specs=out_specs,
        scratch_shapes=[pltpu.VMEM((tm, d), BF16)],
        input_output_aliases=aliases,
        compiler_params=_params(("parallel", "arbitrary")),
        name="ffn",
    )(*alias_args, src.arr, norm_g, wg.arr, wu.arr, wd.arr)
    n_act = 2 if n_split else 1
    return tuple(res[:n_act]) + (tuple(_weight(a) for a in res[n_act:]),)


def _mixer_kernel(*refs, seq_len, tm, emit, aliased, n_parts):
    (h_ref, g_ref, wb_ref, wc_ref, wx_ref, cw_ref, wo_ref, p0_ref, p1_ref,
     out_ref, st_ref, *rest) = refs[aliased:]
    wb16_ref, wc16_ref, wx16_ref, wo16_ref = rest[:4] if emit else (None,) * 4
    xn_ref, carry_ref = rest[-2:]
    i = pl.program_id(0)
    j = pl.program_id(1)

    @pl.when(j == 0)
    def _():
        h = h_ref[...]
        xn_ref[...] = _rms_rows(h, g_ref[...]).astype(BF16)
        out_ref[...] = h

    if seq_len >= tm:
        @pl.when((i % (seq_len // tm)) == 0)
        def _():
            carry_ref[j] = p0_ref[0]

    wb, wc, wx, wo = (_bf16_weight(wb_ref, wb16_ref), _bf16_weight(wc_ref, wc16_ref),
                      _bf16_weight(wx_ref, wx16_ref), _bf16_weight(wo_ref, wo16_ref))
    cw = cw_ref[...]
    tn = cw.shape[1]
    hm = tm // n_parts
    bs, us = [], []
    for part in range(n_parts):
        xn = xn_ref[part * hm:(part + 1) * hm, :]
        bs.append(jnp.dot(xn, wb, preferred_element_type=F32))
        c = jnp.dot(xn, wc, preferred_element_type=F32)
        x = jnp.dot(xn, wx, preferred_element_type=F32)
        us.append(c * x)
    row = lax.broadcasted_iota(jnp.int32, (hm, tn), 0)
    for part in range(n_parts):
        u = us[part]
        if seq_len >= tm:
            prev = carry_ref[j] if part == 0 else us[part - 1][hm - 2:hm, :]
            p0 = prev[0:1, :]
            p1 = prev[1:2, :]
            pos = row
        else:
            p0 = p0_ref[...]
            p1 = p1_ref[...]
            pos = row % seq_len
        um1 = jnp.where(pos == 0, p1, pltpu.roll(u, 1, 0))
        um2 = jnp.where(pos == 0, p0, jnp.where(pos == 1, p1, pltpu.roll(u, 2, 0)))
        conv = cw[0:1, :] * um2 + cw[1:2, :] * um1 + cw[2:3, :] * u
        v = (bs[part] * conv).astype(BF16)
        out_ref[part * hm:(part + 1) * hm, :] += jnp.dot(v, wo, preferred_element_type=F32)
    if seq_len >= tm:
        carry_ref[j] = us[-1][hm - 2:hm, :]
        st_ref[0] = us[-1][hm - 2:hm, :]
    else:
        st_ref[...] = us[0].reshape(tm // seq_len, seq_len, tn)[:, seq_len - 2:seq_len, :]


def _mixer(src, rows, dest, norm_g, g_lead, wb, wc, wx, conv_w, cw_lead, wo, state, seq_len, tm,
           tn, n_parts=1):
    d = src.arr.shape[1]
    n_seq = rows // seq_len
    n_j = d // tn
    n_i = rows // tm
    emit = _is_f32(wb)
    if seq_len >= tm:
        tiles_per_seq = seq_len // tm
        p0, p1 = state, state
        p_spec = pl.BlockSpec((1, CONV_WIDTH - 1, tn), lambda i, j: (i // tiles_per_seq, 0, j))
        st_shape = jax.ShapeDtypeStruct((n_i, CONV_WIDTH - 1, d), F32)
        st_spec = pl.BlockSpec((1, CONV_WIDTH - 1, tn), lambda i, j: (i, 0, j))
    else:
        assert tm == rows and tm % seq_len == 0 and n_parts == 1
        p0 = jnp.repeat(state[:, 0], seq_len, axis=0)
        p1 = jnp.repeat(state[:, 1], seq_len, axis=0)
        p_spec = pl.BlockSpec((tm, tn), lambda i, j: (i, j))
        st_shape = jax.ShapeDtypeStruct((n_seq, CONV_WIDTH - 1, d), F32)
        st_spec = pl.BlockSpec((n_seq, CONV_WIDTH - 1, tn), lambda i, j: (0, 0, j))
    out_shape = [jax.ShapeDtypeStruct((dest.rows, d), F32), st_shape]
    out_specs = [pl.BlockSpec((tm, d), lambda i, j: (dest.first + i, 0)), st_spec]
    if emit:
        assert n_i == 1
        out_shape += [jax.ShapeDtypeStruct((d, d), BF16)] * 4
        out_specs += [pl.BlockSpec((d, tn), lambda i, j: (0, j))] * 3
        out_specs += [pl.BlockSpec((tn, d), lambda i, j: (j, 0))]
    alias_specs, alias_args, aliases = _dest_args(dest)
    res = pl.pallas_call(
        functools.partial(_mixer_kernel, seq_len=seq_len, tm=tm, emit=emit,
                          aliased=len(alias_args), n_parts=n_parts),
        out_shape=out_shape,
        grid=(n_i, n_j),
        in_specs=alias_specs + [
            pl.BlockSpec((tm, d), lambda i, j: (src.first + i, 0)),
            _lead_spec(g_lead, (1, d)),
            _wspec(wb, (d, tn), lambda i, j: (0, j)),
            _wspec(wc, (d, tn), lambda i, j: (0, j)),
            _wspec(wx, (d, tn), lambda i, j: (0, j)),
            pl.BlockSpec((None,) * len(cw_lead) + (CONV_WIDTH, tn),
                         lambda i, j: tuple(cw_lead) + (0, j)),
            _wspec(wo, (tn, d), lambda i, j: (j, 0)),
            p_spec,
            p_spec,
        ],
        out_specs=out_specs,
        scratch_shapes=[pltpu.VMEM((tm, d), BF16),
                        pltpu.VMEM((n_j, CONV_WIDTH - 1, tn), F32)],
        input_output_aliases=aliases,
        compiler_params=_params(("arbitrary", "arbitrary")),
        name="conv_mixer",
    )(*alias_args, src.arr, norm_g, wb.arr, wc.arr, wx.arr, conv_w, wo.arr, p0, p1)
    out, st = res[0], res[1]
    if seq_len >= tm:
        st = st[seq_len // tm - 1::seq_len // tm]
    return out, st, tuple(_weight(a) for a in res[2:])


def _kv_kernel(h_ref, g_ref, wk_ref, wv_ref, kg_ref, k32_ref, v32_ref, k16_ref, v16_ref,
               *rest, emit, keep, v_transposed):
    wk16_ref, wv16_ref = rest if emit else (None, None)
    xn = _rms_rows(h_ref[...], g_ref[...]).astype(BF16)
    k = jnp.dot(xn, _bf16_weight(wk_ref, wk16_ref), preferred_element_type=F32)
    v = jnp.dot(xn, _bf16_weight(wv_ref, wv16_ref), preferred_element_type=F32)
    kg = kg_ref[...]
    tm = k.shape[0]
    v32_ref[...] = v[tm - keep:, :]
    for hd in range(k.shape[1] // HEAD_DIM):
        cols = slice(hd * HEAD_DIM, (hd + 1) * HEAD_DIM)
        kh = _rms_rows(k[:, cols], kg)
        k32_ref[:, cols] = kh[tm - keep:, :]
        k16_ref[hd] = kh.astype(BF16)
        if v_transposed:
            v16_ref[hd] = v[:, cols].T.astype(BF16)
        else:
            v16_ref[hd] = v[:, cols].astype(BF16)


def _kv_proj(src, rows, norm_g, wk, wv, k_gain, seq_len, tm, v_transposed):
    d = src.arr.shape[1]
    emit = _is_f32(wk)
    tn = CAST_COL_TILE if emit else KV_COL_TILE
    hpb = tn // HEAD_DIM
    if seq_len >= tm:
        keep = min(PAST_ROWS, seq_len)
        assert keep <= tm and seq_len % tm == 0
        tiles_per_seq = seq_len // tm
        kept_rows = (rows // seq_len) * keep
        kept_spec = pl.BlockSpec((keep, tn), lambda j, i: (i // tiles_per_seq, j))
    else:
        assert tm % seq_len == 0
        keep, kept_rows = tm, rows
        kept_spec = pl.BlockSpec((tm, tn), lambda j, i: (i, j))
    if v_transposed:
        v16_shape = jax.ShapeDtypeStruct((N_HEADS, HEAD_DIM, rows), BF16)
        v16_spec = pl.BlockSpec((hpb, HEAD_DIM, tm), lambda j, i: (j, 0, i))
    else:
        v16_shape = jax.ShapeDtypeStruct((N_HEADS, rows, HEAD_DIM), BF16)
        v16_spec = pl.BlockSpec((hpb, tm, HEAD_DIM), lambda j, i: (j, i, 0))
    out_shape = [jax.ShapeDtypeStruct((kept_rows, d), F32),
                 jax.ShapeDtypeStruct((kept_rows, d), F32),
                 jax.ShapeDtypeStruct((N_HEADS, rows, HEAD_DIM), BF16), v16_shape]
    out_specs = [kept_spec, kept_spec,
                 pl.BlockSpec((hpb, tm, HEAD_DIM), lambda j, i: (j, i, 0)), v16_spec]
    if emit:
        assert rows == tm
        out_shape += [jax.ShapeDtypeStruct((d, d), BF16)] * 2
        out_specs += [pl.BlockSpec((d, tn), lambda j, i: (0, j))] * 2
    res = pl.pallas_call(
        functools.partial(_kv_kernel, emit=emit, keep=keep, v_transposed=v_transposed),
        out_shape=out_shape,
        grid=(d // tn, rows // tm),
        in_specs=[
            pl.BlockSpec((tm, d), lambda j, i: (src.first + i, 0)),
            pl.BlockSpec((1, d), lambda j, i: (0, 0)),
            _wspec(wk, (d, tn), lambda j, i: (0, j)),
            _wspec(wv, (d, tn), lambda j, i: (0, j)),
            pl.BlockSpec((1, HEAD_DIM), lambda j, i: (0, 0)),
        ],
        out_specs=out_specs,
        compiler_params=_params(("arbitrary", "arbitrary")),
        name="kv_proj",
    )(src.arr, norm_g, wk.arr, wv.arr, k_gain)
    return res[:4], tuple(_weight(a) for a in res[4:])


def _q_kernel(h_ref, g_ref, wq_ref, qg_ref, q16_ref, *rest, emit):
    wq16_ref = rest[0] if emit else None
    xn_ref = rest[-1]

    @pl.when(pl.program_id(1) == 0)
    def _():
        xn_ref[...] = _rms_rows(h_ref[...], g_ref[...]).astype(BF16)

    q = jnp.dot(xn_ref[...], _bf16_weight(wq_ref, wq16_ref), preferred_element_type=F32)
    qg = qg_ref[...]
    for hd in range(q.shape[1] // HEAD_DIM):
        cols = slice(hd * HEAD_DIM, (hd + 1) * HEAD_DIM)
        q16_ref[hd] = _rms_rows(q[:, cols], qg).astype(BF16)


def _q_proj(src, rows, norm_g, g_lead, wq, q_gain, qg_lead, tm):
    d = src.arr.shape[1]
    emit = _is_f32(wq)
    tn = CAST_COL_TILE if emit else d
    hpb = tn // HEAD_DIM
    out_shape = [jax.ShapeDtypeStruct((N_HEADS, rows, HEAD_DIM), BF16)]
    out_specs = [pl.BlockSpec((hpb, tm, HEAD_DIM), lambda i, j: (j, i, 0))]
    if emit:
        assert rows == tm
        out_shape += [jax.ShapeDtypeStruct((d, d), BF16)]
        out_specs += [pl.BlockSpec((d, tn), lambda i, j: (0, j))]
    res = pl.pallas_call(
        functools.partial(_q_kernel, emit=emit),
        out_shape=out_shape,
        grid=(rows // tm, d // tn),
        in_specs=[
            pl.BlockSpec((tm, d), lambda i, j: (src.first + i, 0)),
            _lead_spec(g_lead, (1, d)),
            _wspec(wq, (d, tn), lambda i, j: (0, j)),
            _lead_spec(qg_lead, (1, HEAD_DIM)),
        ],
        out_specs=out_specs,
        scratch_shapes=[pltpu.VMEM((tm, d), BF16)],
        compiler_params=_params(("parallel", "arbitrary")),
        name="q_proj",
    )(src.arr, norm_g, wq.arr, q_gain)
    return res[0], tuple(_weight(a) for a in res[1:])


def _o_kernel(*refs, emit, att_transposed, aliased):
    h_ref, att_ref, wo_ref, out_ref, *rest = refs[aliased:]
    wo16_ref = rest[0] if emit else None
    wo = _bf16_weight(wo_ref, wo16_ref)
    if att_transposed:
        att_t = att_ref[...].reshape(D_MODEL, att_ref.shape[2])
        y = lax.dot_general(att_t, wo, (((0,), (0,)), ((), ())), preferred_element_type=F32)
    else:
        att2d_ref = rest[-1]

        @pl.when(pl.program_id(1) == 0)
        def _():
            for hd in range(N_HEADS):
                att2d_ref[:, hd * HEAD_DIM:(hd + 1) * HEAD_DIM] = att_ref[hd]

        y = jnp.dot(att2d_ref[...], wo, preferred_element_type=F32)
    out_ref[...] = h_ref[...] + y


def _o_proj(src, rows, dest, att16, wo, tm, att_transposed):
    d = src.arr.shape[1]
    emit = _is_f32(wo)
    tn = CAST_COL_TILE if emit else d
    out_shape = [jax.ShapeDtypeStruct((dest.rows, d), F32)]
    out_specs = [pl.BlockSpec((tm, tn), lambda i, j: (dest.first + i, j))]
    if emit:
        assert rows == tm
        out_shape += [jax.ShapeDtypeStruct((d, d), BF16)]
        out_specs += [pl.BlockSpec((d, tn), lambda i, j: (0, j))]
    if att_transposed:
        att_spec = pl.BlockSpec((N_HEADS, HEAD_DIM, tm), lambda i, j: (0, 0, i))
        scratch = []
    else:
        att_spec = pl.BlockSpec((N_HEADS, tm, HEAD_DIM), lambda i, j: (0, i, 0))
        scratch = [pltpu.VMEM((tm, d), BF16)]
    alias_specs, alias_args, aliases = _dest_args(dest)
    res = pl.pallas_call(
        functools.partial(_o_kernel, emit=emit, att_transposed=att_transposed,
                          aliased=len(alias_args)),
        out_shape=out_shape,
        grid=(rows // tm, d // tn),
        in_specs=alias_specs + [
            pl.BlockSpec((tm, tn), lambda i, j: (src.first + i, j)),
            att_spec,
            _wspec(wo, (d, tn), lambda i, j: (0, j)),
        ],
        out_specs=out_specs,
        scratch_shapes=scratch,
        input_output_aliases=aliases,
        compiler_params=_params(("parallel", "arbitrary")),
        name="o_proj",
    )(*alias_args, src.arr, att16, wo.arr)
    return res[0], tuple(_weight(a) for a in res[1:])


def _attn_prompt_kernel(q_ref, kp_ref, kc_ref, vp_ref, vc_ref, bias_ref, o_ref,
                        s_refs, p_refs):
    i = pl.program_id(1)
    n_before_start = jnp.where(i == 0, ATTN_Q_TILE, 0)
    n_chunks = ATTN_WIN // CHUNK
    groups = CHUNK // SUBLANES

    subs = ATTN_Q_TILE // ATTN_SUB
    n_units = ATTN_HEADS_PER_STEP * subs

    def scores(h0, u):
        h, lo = h0 + u // subs, (u % subs) * ATTN_SUB
        q = q_ref[h, lo:lo + ATTN_SUB, :]
        n_past = ATTN_Q_TILE - lo
        dims = (((1,), (1,)), ((), ()))
        s_refs[u, 0:n_past, :] = lax.dot_general(kp_ref[h, lo:, :], q, dims,
                                                 preferred_element_type=F32)
        s_refs[u, n_past:, :] = lax.dot_general(kc_ref[h, 0:ATTN_WIN - n_past, :], q, dims,
                                                preferred_element_type=F32)

    def softmax_and_values(h0, u):
        h, lo = h0 + u // subs, (u % subs) * ATTN_SUB
        s_ref, p_ref = s_refs.at[u], p_refs.at[u]
        m8 = jnp.full((SUBLANES, ATTN_SUB), NEG_INF, F32)
        for c in range(n_chunks):
            rows = slice(c * CHUNK, (c + 1) * CHUNK)
            before_start = c * CHUNK < n_before_start - lo
            x = s_ref[rows, :] * (ATTN_SCALE * LOG2_E) + bias_ref[h, rows, :]
            x = jnp.where(before_start, NEG_INF, x)
            s_ref[rows, :] = x
            m8 = jnp.maximum(m8, jnp.max(x.reshape(groups, SUBLANES, ATTN_SUB), axis=0))
        m = jnp.max(m8, axis=0, keepdims=True)
        l8 = jnp.zeros((SUBLANES, ATTN_SUB), F32)
        for c in range(n_chunks):
            rows = slice(c * CHUNK, (c + 1) * CHUNK)
            e = jnp.exp2(s_ref[rows, :] - m)
            p_ref[rows, :] = e.astype(BF16)
            l8 = l8 + jnp.sum(e.reshape(groups, SUBLANES, ATTN_SUB), axis=0)
        inv = 1.0 / jnp.sum(l8, axis=0, keepdims=True)
        n_past = ATTN_Q_TILE - lo
        o = jnp.dot(vp_ref[h, :, lo:], p_ref[0:n_past, :], preferred_element_type=F32)
        o = o + jnp.dot(vc_ref[h, :, 0:ATTN_WIN - n_past], p_ref[n_past:, :],
                        preferred_element_type=F32)
        o_ref[h, :, lo:lo + ATTN_SUB] = (o * inv).astype(BF16)

    def step(t, carry):
        h0 = t * ATTN_HEADS_PER_STEP
        scores(h0, 0)
        for u in range(n_units):
            if u + 1 < n_units:
                scores(h0, u + 1)
            softmax_and_values(h0, u)
        return carry

    lax.fori_loop(0, N_HEADS // ATTN_HEADS_PER_STEP, step, 0)


def _prompt_bias(rel_bias):
    period = pl.next_power_of_2(ATTN_SUB + ATTN_WIN - 1)
    j = jnp.arange(period)
    k = jnp.where(j < ATTN_WIN, j, j - period)
    idx = jnp.clip(PAST_ROWS - k, -MAX_REL, MAX_REL) + MAX_REL
    vec = rel_bias[:, idx].astype(F32)
    flat = jnp.tile(vec, (1, ATTN_SUB))[:, :ATTN_SUB * (period - 1)]
    toeplitz = flat.reshape(N_HEADS, ATTN_SUB, period - 1)[:, :, :ATTN_WIN]
    qc = jnp.arange(ATTN_SUB)[:, None] // CHUNK
    kc = jnp.arange(ATTN_WIN)[None, :] // CHUNK
    band = (kc >= qc) & (kc <= qc + N_PAST_CHUNKS)
    return jnp.swapaxes(jnp.where(band[None], toeplitz * LOG2_E, NEG_INF), 1, 2)


def _attn_prompt(q16, k16, v16t, rel_bias, n_seq, seq_len):
    tiles = seq_len // ATTN_Q_TILE
    cur = lambda b, i: b * tiles + i
    past = lambda b, i: b * tiles + jnp.maximum(i - 1, 0)
    rows = lambda at: pl.BlockSpec((N_HEADS, ATTN_Q_TILE, HEAD_DIM), lambda b, i: (0, at(b, i), 0))
    cols = lambda at: pl.BlockSpec((N_HEADS, HEAD_DIM, ATTN_Q_TILE), lambda b, i: (0, 0, at(b, i)))
    bias = _prompt_bias(rel_bias)
    n_units = ATTN_HEADS_PER_STEP * (ATTN_Q_TILE // ATTN_SUB)
    return pl.pallas_call(
        _attn_prompt_kernel,
        out_shape=jax.ShapeDtypeStruct(v16t.shape, BF16),
        grid=(n_seq, tiles),
        in_specs=[rows(cur), rows(past), rows(cur), cols(past), cols(cur),
                  pl.BlockSpec(bias.shape, lambda b, i: (0, 0, 0))],
        out_specs=cols(cur),
        scratch_shapes=[pltpu.VMEM((n_units, ATTN_WIN, ATTN_SUB), F32),
                        pltpu.VMEM((n_units, ATTN_WIN, ATTN_SUB), BF16)],
        compiler_params=_params(("parallel", "arbitrary")),
        name="attn_prompt",
    )(q16, k16, k16, v16t, v16t, bias)


def _attn_sample_kernel(q_ref, kn_ref, vn_ref, ck_ref, cv_ref, bc_ref, bn_ref, o_ref):
    cache_len = ck_ref.shape[1] // N_HEADS
    for h in range(N_HEADS):
        q = q_ref[h]
        kc = ck_ref[0, pl.ds(h, cache_len, stride=N_HEADS), :].astype(BF16)
        vc = cv_ref[0, pl.ds(h, cache_len, stride=N_HEADS), :].astype(BF16)
        dims = (((1,), (1,)), ((), ()))
        sc = lax.dot_general(q, kc, dims, preferred_element_type=F32) * ATTN_SCALE + bc_ref[h]
        sn = lax.dot_general(q, kn_ref[h], dims, preferred_element_type=F32) * ATTN_SCALE + bn_ref[h]
        m = jnp.maximum(jnp.max(sc, axis=-1, keepdims=True), jnp.max(sn, axis=-1, keepdims=True))
        ec = jnp.exp(sc - m)
        en = jnp.exp(sn - m)
        inv = 1.0 / (jnp.sum(ec, axis=-1, keepdims=True) + jnp.sum(en, axis=-1, keepdims=True))
        o = jnp.dot((ec * inv).astype(BF16), vc, preferred_element_type=F32)
        o = o + jnp.dot((en * inv).astype(BF16), vn_ref[h], preferred_element_type=F32)
        o_ref[h] = o.astype(BF16)


def _attn_sample(q16, k16, v16, cache_k, cache_v, rel_bias, n_seq, seq_len):
    cache_len = cache_k.shape[1]
    ck = cache_k.reshape(n_seq, cache_len * N_HEADS, HEAD_DIM)
    cv = cache_v.reshape(n_seq, cache_len * N_HEADS, HEAD_DIM)
    a = jnp.arange(seq_len)[:, None]
    w = jnp.arange(cache_len + seq_len)[None, :]
    idx = jnp.clip(a - w + cache_len, -MAX_REL, MAX_REL) + MAX_REL
    bias = rel_bias[:, idx].astype(F32)
    bias_c = bias[:, :, :cache_len]
    bias_n = bias[:, :, cache_len:]
    new = pl.BlockSpec((N_HEADS, seq_len, HEAD_DIM), lambda b: (0, b, 0))
    cache = pl.BlockSpec((1, cache_len * N_HEADS, HEAD_DIM), lambda b: (b, 0, 0))
    return pl.pallas_call(
        _attn_sample_kernel,
        out_shape=jax.ShapeDtypeStruct(q16.shape, BF16),
        grid=(n_seq,),
        in_specs=[new, new, new, cache, cache,
                  pl.BlockSpec(bias_c.shape, lambda b: (0, 0, 0)),
                  pl.BlockSpec(bias_n.shape, lambda b: (0, 0, 0))],
        out_specs=new,
        compiler_params=_params(("parallel",)),
        name="attn_sample",
    )(q16, k16, v16, ck, cv, bias_c, bias_n)


def kernel(x_prompt, x_sample, state_conv, cache_k, cache_v, ffn_norm, ffn_w_gate, ffn_w_up,
           ffn_w_down, mix_norm, conv_w_in, conv_w, conv_w_out, kv_norm, w_kv, k_gain,
           w_q, q_gain, rel_bias, w_o):
    depth = ffn_norm.shape[0]
    assert depth == 2 and conv_w_in.shape[0] == 1 and w_q.shape[0] == 1
    d = D_MODEL
    n_p, t_p, _ = x_prompt.shape
    n_s, t_s, _ = x_sample.shape
    m_p, m_s = n_p * t_p, n_s * t_s
    m_all = m_p + m_s
    tm_p, tm_f = PROMPT_ROW_TILE, FFN_ROW_TILE
    assert m_all % tm_f == 0 and m_s < tm_f and m_p % m_s == 0 and m_p % tm_p == 0
    ffn_tiles = m_all // tm_f
    s_first = m_p // m_s

    ffn_norm4 = ffn_norm.reshape(depth, 2, 1, d)
    mix_norm3 = mix_norm.reshape(depth, 1, d)
    kv_norm2 = kv_norm.reshape(1, d)
    k_gain2 = k_gain.reshape(1, HEAD_DIM)
    q_gain3 = q_gain.reshape(-1, 1, HEAD_DIM)

    def ffn(tail, main, layer, slot, final=False):
        w32 = [_weight(w, (layer, slot)) for w in (ffn_w_gate, ffn_w_up, ffn_w_down)]
        rows = m_p if final else m_all
        *ys, w16 = _ffn(tail, 1, Dest(rows, ffn_tiles - 1, None), ffn_norm4, (layer, slot),
                        *w32, tm_f, n_split=m_s if final else 0)
        h, _ = _ffn(main, ffn_tiles - 1, Dest(rows, 0, ys[0]), ffn_norm4, (layer, slot),
                    *w16, tm_f)
        return (h, ys[1]) if final else h

    def both(h):
        return Rows(h, ffn_tiles - 1), Rows(h, 0)

    xp = x_prompt.reshape(m_p, d)
    x_tail = jnp.concatenate([xp[m_p - (tm_f - m_s):], x_sample.reshape(m_s, d)], axis=0)
    h = ffn(Rows(x_tail, 0), Rows(xp, 0), 0, 0)

    mix_blocks = d // MIX_COL_TILE
    w_mix32 = (_weight(conv_w_in, (0,)), _weight(conv_w_in, (0,), mix_blocks),
               _weight(conv_w_in, (0,), 2 * mix_blocks))
    h2, conv_s, w_mix16 = _mixer(Rows(h, s_first), m_s, Dest(m_all, s_first, None), mix_norm3, (0,),
                                 *w_mix32, conv_w, (0,), _weight(conv_w_out, (0,)),
                                 state_conv[0], t_s, m_s, MIX_COL_TILE)
    conv_zero = jnp.zeros((n_p, CONV_WIDTH - 1, d), F32)
    h, conv_p, _ = _mixer(Rows(h, 0), m_p, Dest(m_all, 0, h2), mix_norm3, (0,),
                          *w_mix16[:3], conv_w, (0,), w_mix16[3], conv_zero, t_p,
                          PROMPT_MIX_ROW_TILE, PROMPT_MIX_COL_TILE, PROMPT_MIX_PARTS)

    h = ffn(*both(h), 0, 1)

    proj_blocks = d // CAST_COL_TILE
    (k_s, v_s, k16_s, v16_s), w_kv16 = _kv_proj(
        Rows(h, s_first), m_s, kv_norm2, _weight(w_kv), _weight(w_kv, (), proj_blocks), k_gain2,
        t_s, m_s, v_transposed=False)
    (k_p, v_p, k16_p, v16t_p), _ = _kv_proj(
        Rows(h, 0), m_p, kv_norm2, *w_kv16, k_gain2, t_p, tm_p, v_transposed=True)

    h = ffn(*both(h), 1, 0)

    q16_s, w_q16 = _q_proj(Rows(h, s_first), m_s, mix_norm3, (1,), _weight(w_q, (0,)),
                           q_gain3, (0,), m_s)
    q16_p, _ = _q_proj(Rows(h, 0), m_p, mix_norm3, (1,), *w_q16, q_gain3, (0,), tm_p)
    att_s = _attn_sample(q16_s, k16_s, v16_s, cache_k, cache_v, rel_bias[0], n_s, t_s)
    att_p = _attn_prompt(q16_p, k16_p, v16t_p, rel_bias[0], n_p, t_p)
    h2, w_o16 = _o_proj(Rows(h, s_first), m_s, Dest(m_all, s_first, None), att_s,
                        _weight(w_o, (0,)), m_s, att_transposed=False)
    h, _ = _o_proj(Rows(h, 0), m_p, Dest(m_all, 0, h2), att_p, *w_o16, tm_p, att_transposed=True)

    y_p, y_s = ffn(*both(h), 1, 1, final=True)

    keep_p, keep_s = min(PAST_ROWS, t_p), min(PAST_ROWS, t_s)
    heads = (N_HEADS, HEAD_DIM)
    return (y_p.reshape(n_p, t_p, d), y_s.reshape(n_s, t_s, d), conv_p[None],
            k_p.reshape(n_p, keep_p, *heads), v_p.reshape(n_p, keep_p, *heads), conv_s[None],
            k_s.reshape(n_s, keep_s, *heads), v_s.reshape(n_s, keep_s, *heads))
```

```python
import collections
import functools

import jax
import jax.numpy as jnp
from jax import lax
from jax.experimental import pallas as pl
from jax.experimental.pallas import tpu as pltpu

F32 = jnp.float32
BF16 = jnp.bfloat16

D_MODEL = 2048
HEAD_DIM = 128
N_HEADS = D_MODEL // HEAD_DIM
CHUNK = 64
N_PAST_CHUNKS = 8
PAST_ROWS = N_PAST_CHUNKS * CHUNK
MAX_REL = 128
CONV_WIDTH = 3
EPS = 1e-6
FFN_RES = 0.5
NEG_INF = -1e30
ATTN_SCALE = HEAD_DIM ** -0.5
LOG2_E = 1.4426950408889634

SUBLANES = 8
V7X_VMEM_BYTES = 64 * 1024 * 1024
VMEM_LIMIT_BYTES = V7X_VMEM_BYTES - 8 * 1024 * 1024

PROMPT_ROW_TILE = 512
FFN_ROW_TILE = 1040
FFN_COL_TILE = 512
FFN_CAST_COL_TILE = 256
MIX_COL_TILE = 512
PROMPT_MIX_ROW_TILE = 1024
PROMPT_MIX_PARTS = 2
PROMPT_MIX_COL_TILE = 256
KV_COL_TILE = 1024
CAST_COL_TILE = 512
ATTN_Q_TILE = 512
ATTN_SUB = 256
ATTN_WIN = ATTN_SUB + PAST_ROWS
ATTN_HEADS_PER_STEP = 4

Weight = collections.namedtuple("Weight", ["arr", "lead", "col_off"])
Rows = collections.namedtuple("Rows", ["arr", "first"])
Dest = collections.namedtuple("Dest", ["rows", "first", "alias"])


def _weight(arr, lead=(), col_off=0):
    return Weight(arr, tuple(lead), col_off)


def _wspec(w, block, index):
    lead = w.lead

    def index_map(*ids):
        r, c = index(*ids)
        return lead + (r, c + w.col_off)

    return pl.BlockSpec((None,) * len(lead) + block, index_map)


def _lead_spec(lead, block):
    lead = tuple(lead)
    return pl.BlockSpec((None,) * len(lead) + block, lambda *ids: lead + (0,) * len(block))


def _is_f32(w):
    return w.arr.dtype == F32


def _bf16_weight(w_ref, w16_ref):
    if w16_ref is None:
        return w_ref[...]
    w16 = w_ref[...].astype(BF16)
    w16_ref[...] = w16
    return w16


def _dest_args(dest):
    if dest.alias is None:
        return [], [], {}
    return [pl.BlockSpec(memory_space=pl.ANY)], [dest.alias], {0: 0}


def _params(semantics):
    return pltpu.CompilerParams(dimension_semantics=semantics,
                                vmem_limit_bytes=VMEM_LIMIT_BYTES)


def _rms_rows(x, g):
    ms = jnp.mean(x * x, axis=-1, keepdims=True)
    return (x * lax.rsqrt(ms + EPS)) * g


def _ffn_kernel(*refs, emit, aliased, n_split):
    h_ref, g_ref, wg_ref, wu_ref, wd_ref, out_ref, *rest = refs[aliased:]
    split_ref = rest.pop(0) if n_split else None
    wg16_ref, wu16_ref, wd16_ref = rest[:3] if emit else (None, None, None)
    xn_ref = rest[-1]
    f = pl.program_id(1)

    @pl.when(f == 0)
    def _():
        h = h_ref[...]
        xn_ref[...] = _rms_rows(h, g_ref[...]).astype(BF16)
        out_ref[...] = h

    xn = xn_ref[...]
    gate = jnp.dot(xn, _bf16_weight(wg_ref, wg16_ref), preferred_element_type=F32)
    up = jnp.dot(xn, _bf16_weight(wu_ref, wu16_ref), preferred_element_type=F32)
    act = (gate * jax.nn.sigmoid(gate) * (FFN_RES * up)).astype(BF16)
    out_ref[...] += jnp.dot(act, _bf16_weight(wd_ref, wd16_ref), preferred_element_type=F32)

    if n_split:
        @pl.when(f == pl.num_programs(1) - 1)
        def _():
            split_ref[...] = out_ref[out_ref.shape[0] - n_split:, :]


def _ffn(src, n_tiles, dest, norm_g, g_lead, wg, wu, wd, tm, n_split=0):
    d = src.arr.shape[1]
    d_ff = wg.arr.shape[-1]
    emit = _is_f32(wg)
    tf = FFN_CAST_COL_TILE if emit else FFN_COL_TILE
    out_shape = [jax.ShapeDtypeStruct((dest.rows, d), F32)]
    out_specs = [pl.BlockSpec((tm, d), lambda i, f: (dest.first + i, 0))]
    if n_split:
        assert n_tiles == 1
        out_shape += [jax.ShapeDtypeStruct((n_split, d), F32)]
        out_specs += [pl.BlockSpec((n_split, d), lambda i, f: (0, 0))]
    h_mode = {}
    if emit:
        assert n_tiles == 1
        out_shape += [jax.ShapeDtypeStruct((d, d_ff), BF16), jax.ShapeDtypeStruct((d, d_ff), BF16),
                      jax.ShapeDtypeStruct((d_ff, d), BF16)]
        out_specs += [pl.BlockSpec((d, tf), lambda i, f: (0, f)),
                      pl.BlockSpec((d, tf), lambda i, f: (0, f)),
                      pl.BlockSpec((tf, d), lambda i, f: (f, 0))]
        h_mode = dict(pipeline_mode=pl.Buffered(1))
    alias_specs, alias_args, aliases = _dest_args(dest)
    res = pl.pallas_call(
        functools.partial(_ffn_kernel, emit=emit, aliased=len(alias_args), n_split=n_split),
        out_shape=out_shape,
        grid=(n_tiles, d_ff // tf),
        in_specs=alias_specs + [
            pl.BlockSpec((tm, d), lambda i, f: (src.first + i, 0), **h_mode),
            _lead_spec(g_lead, (1, d)),
            _wspec(wg, (d, tf), lambda i, f: (0, f)),
            _wspec(wu, (d, tf), lambda i, f: (0, f)),
            _wspec(wd, (tf, d), lambda i, f: (f, 0)),
        ],
        out_specs=out_specs,
        scratch_shapes=[pltpu.VMEM((tm, d), BF16)],
        input_output_aliases=aliases,
        compiler_params=_params(("parallel", "arbitrary")),
        name="ffn",
    )(*alias_args, src.arr, norm_g, wg.arr, wu.arr, wd.arr)
    n_act = 2 if n_split else 1
    return tuple(res[:n_act]) + (tuple(_weight(a) for a in res[n_act:]),)


def _mixer_kernel(*refs, seq_len, tm, emit, aliased, n_parts):
    (h_ref, g_ref, wb_ref, wc_ref, wx_ref, cw_ref, wo_ref, p0_ref, p1_ref,
     out_ref, st_ref, *rest) = refs[aliased:]
    wb16_ref, wc16_ref, wx16_ref, wo16_ref = rest[:4] if emit else (None,) * 4
    xn_ref, carry_ref = rest[-2:]
    i = pl.program_id(0)
    j = pl.program_id(1)

    @pl.when(j == 0)
    def _():
        h = h_ref[...]
        xn_ref[...] = _rms_rows(h, g_ref[...]).astype(BF16)
        out_ref[...] = h

    if seq_len >= tm:
        @pl.when((i % (seq_len // tm)) == 0)
        def _():
            carry_ref[j] = p0_ref[0]

    wb, wc, wx, wo = (_bf16_weight(wb_ref, wb16_ref), _bf16_weight(wc_ref, wc16_ref),
                      _bf16_weight(wx_ref, wx16_ref), _bf16_weight(wo_ref, wo16_ref))
    cw = cw_ref[...]
    tn = cw.shape[1]
    hm = tm // n_parts
    bs, us = [], []
    for part in range(n_parts):
        xn = xn_ref[part * hm:(part + 1) * hm, :]
        bs.append(jnp.dot(xn, wb, preferred_element_type=F32))
        c = jnp.dot(xn, wc, preferred_element_type=F32)
        x = jnp.dot(xn, wx, preferred_element_type=F32)
        us.append(c * x)
    row = lax.broadcasted_iota(jnp.int32, (hm, tn), 0)
    for part in range(n_parts):
        u = us[part]
        if seq_len >= tm:
            prev = carry_ref[j] if part == 0 else us[part - 1][hm - 2:hm, :]
            p0 = prev[0:1, :]
            p1 = prev[1:2, :]
            pos = row
        else:
            p0 = p0_ref[...]
            p1 = p1_ref[...]
            pos = row % seq_len
        um1 = jnp.where(pos == 0, p1, pltpu.roll(u, 1, 0))
        um2 = jnp.where(pos == 0, p0, jnp.where(pos == 1, p1, pltpu.roll(u, 2, 0)))
        conv = cw[0:1, :] * um2 + cw[1:2, :] * um1 + cw[2:3, :] * u
        v = (bs[part] * conv).astype(BF16)
        out_ref[part * hm:(part + 1) * hm, :] += jnp.dot(v, wo, preferred_element_type=F32)
    if seq_len >= tm:
        carry_ref[j] = us[-1][hm - 2:hm, :]
        st_ref[0] = us[-1][hm - 2:hm, :]
    else:
        st_ref[...] = us[0].reshape(tm // seq_len, seq_len, tn)[:, seq_len - 2:seq_len, :]


def _mixer(src, rows, dest, norm_g, g_lead, wb, wc, wx, conv_w, cw_lead, wo, state, seq_len, tm,
           tn, n_parts=1):
    d = src.arr.shape[1]
    n_seq = rows // seq_len
    n_j = d // tn
    n_i = rows // tm
    emit = _is_f32(wb)
    if seq_len >= tm:
        tiles_per_seq = seq_len // tm
        p0, p1 = state, state
        p_spec = pl.BlockSpec((1, CONV_WIDTH - 1, tn), lambda i, j: (i // tiles_per_seq, 0, j))
        st_shape = jax.ShapeDtypeStruct((n_i, CONV_WIDTH - 1, d), F32)
        st_spec = pl.BlockSpec((1, CONV_WIDTH - 1, tn), lambda i, j: (i, 0, j))
    else:
        assert tm == rows and tm % seq_len == 0 and n_parts == 1
        p0 = jnp.repeat(state[:, 0], seq_len, axis=0)
        p1 = jnp.repeat(state[:, 1], seq_len, axis=0)
        p_spec = pl.BlockSpec((tm, tn), lambda i, j: (i, j))
        st_shape = jax.ShapeDtypeStruct((n_seq, CONV_WIDTH - 1, d), F32)
        st_spec = pl.BlockSpec((n_seq, CONV_WIDTH - 1, tn), lambda i, j: (0, 0, j))
    out_shape = [jax.ShapeDtypeStruct((dest.rows, d), F32), st_shape]
    out_specs = [pl.BlockSpec((tm, d), lambda i, j: (dest.first + i, 0)), st_spec]
    if emit:
        assert n_i == 1
        out_shape += [jax.ShapeDtypeStruct((d, d), BF16)] * 4
        out_specs += [pl.BlockSpec((d, tn), lambda i, j: (0, j))] * 3
        out_specs += [pl.BlockSpec((tn, d), lambda i, j: (j, 0))]
    alias_specs, alias_args, aliases = _dest_args(dest)
    res = pl.pallas_call(
        functools.partial(_mixer_kernel, seq_len=seq_len, tm=tm, emit=emit,
                          aliased=len(alias_args), n_parts=n_parts),
        out_shape=out_shape,
        grid=(n_i, n_j),
        in_specs=alias_specs + [
            pl.BlockSpec((tm, d), lambda i, j: (src.first + i, 0)),
            _lead_spec(g_lead, (1, d)),
            _wspec(wb, (d, tn), lambda i, j: (0, j)),
            _wspec(wc, (d, tn), lambda i, j: (0, j)),
            _wspec(wx, (d, tn), lambda i, j: (0, j)),
            pl.BlockSpec((None,) * len(cw_lead) + (CONV_WIDTH, tn),
                         lambda i, j: tuple(cw_lead) + (0, j)),
            _wspec(wo, (tn, d), lambda i, j: (j, 0)),
            p_spec,
            p_spec,
        ],
        out_specs=out_specs,
        scratch_shapes=[pltpu.VMEM((tm, d), BF16),
                        pltpu.VMEM((n_j, CONV_WIDTH - 1, tn), F32)],
        input_output_aliases=aliases,
        compiler_params=_params(("arbitrary", "arbitrary")),
        name="conv_mixer",
    )(*alias_args, src.arr, norm_g, wb.arr, wc.arr, wx.arr, conv_w, wo.arr, p0, p1)
    out, st = res[0], res[1]
    if seq_len >= tm:
        st = st[seq_len // tm - 1::seq_len // tm]
    return out, st, tuple(_weight(a) for a in res[2:])


def _kv_kernel(h_ref, g_ref, wk_ref, wv_ref, kg_ref, k32_ref, v32_ref, k16_ref, v16_ref,
               *rest, emit, keep, v_transposed):
    wk16_ref, wv16_ref = rest if emit else (None, None)
    xn = _rms_rows(h_ref[...], g_ref[...]).astype(BF16)
    k = jnp.dot(xn, _bf16_weight(wk_ref, wk16_ref), preferred_element_type=F32)
    v = jnp.dot(xn, _bf16_weight(wv_ref, wv16_ref), preferred_element_type=F32)
    kg = kg_ref[...]
    tm = k.shape[0]
    v32_ref[...] = v[tm - keep:, :]
    for hd in range(k.shape[1] // HEAD_DIM):
        cols = slice(hd * HEAD_DIM, (hd + 1) * HEAD_DIM)
        kh = _rms_rows(k[:, cols], kg)
        k32_ref[:, cols] = kh[tm - keep:, :]
        k16_ref[hd] = kh.astype(BF16)
        if v_transposed:
            v16_ref[hd] = v[:, cols].T.astype(BF16)
        else:
            v16_ref[hd] = v[:, cols].astype(BF16)


def _kv_proj(src, rows, norm_g, wk, wv, k_gain, seq_len, tm, v_transposed):
    d = src.arr.shape[1]
    emit = _is_f32(wk)
    tn = CAST_COL_TILE if emit else KV_COL_TILE
    hpb = tn // HEAD_DIM
    if seq_len >= tm:
        keep = min(PAST_ROWS, seq_len)
        assert keep <= tm and seq_len % tm == 0
        tiles_per_seq = seq_len // tm
        kept_rows = (rows // seq_len) * keep
        kept_spec = pl.BlockSpec((keep, tn), lambda j, i: (i // tiles_per_seq, j))
    else:
        assert tm % seq_len == 0
        keep, kept_rows = tm, rows
        kept_spec = pl.BlockSpec((tm, tn), lambda j, i: (i, j))
    if v_transposed:
        v16_shape = jax.ShapeDtypeStruct((N_HEADS, HEAD_DIM, rows), BF16)
        v16_spec = pl.BlockSpec((hpb, HEAD_DIM, tm), lambda j, i: (j, 0, i))
    else:
        v16_shape = jax.ShapeDtypeStruct((N_HEADS, rows, HEAD_DIM), BF16)
        v16_spec = pl.BlockSpec((hpb, tm, HEAD_DIM), lambda j, i: (j, i, 0))
    out_shape = [jax.ShapeDtypeStruct((kept_rows, d), F32),
                 jax.ShapeDtypeStruct((kept_rows, d), F32),
                 jax.ShapeDtypeStruct((N_HEADS, rows, HEAD_DIM), BF16), v16_shape]
    out_specs = [kept_spec, kept_spec,
                 pl.BlockSpec((hpb, tm, HEAD_DIM), lambda j, i: (j, i, 0)), v16_spec]
    if emit:
        assert rows == tm
        out_shape += [jax.ShapeDtypeStruct((d, d), BF16)] * 2
        out_specs += [pl.BlockSpec((d, tn), lambda j, i: (0, j))] * 2
    res = pl.pallas_call(
        functools.partial(_kv_kernel, emit=emit, keep=keep, v_transposed=v_transposed),
        out_shape=out_shape,
        grid=(d // tn, rows // tm),
        in_specs=[
            pl.BlockSpec((tm, d), lambda j, i: (src.first + i, 0)),
            pl.BlockSpec((1, d), lambda j, i: (0, 0)),
            _wspec(wk, (d, tn), lambda j, i: (0, j)),
            _wspec(wv, (d, tn), lambda j, i: (0, j)),
            pl.BlockSpec((1, HEAD_DIM), lambda j, i: (0, 0)),
        ],
        out_specs=out_specs,
        compiler_params=_params(("arbitrary", "arbitrary")),
        name="kv_proj",
    )(src.arr, norm_g, wk.arr, wv.arr, k_gain)
    return res[:4], tuple(_weight(a) for a in res[4:])


def _q_kernel(h_ref, g_ref, wq_ref, qg_ref, q16_ref, *rest, emit):
    wq16_ref = rest[0] if emit else None
    xn_ref = rest[-1]

    @pl.when(pl.program_id(1) == 0)
    def _():
        xn_ref[...] = _rms_rows(h_ref[...], g_ref[...]).astype(BF16)

    q = jnp.dot(xn_ref[...], _bf16_weight(wq_ref, wq16_ref), preferred_element_type=F32)
    qg = qg_ref[...]
    for hd in range(q.shape[1] // HEAD_DIM):
        cols = slice(hd * HEAD_DIM, (hd + 1) * HEAD_DIM)
        q16_ref[hd] = _rms_rows(q[:, cols], qg).astype(BF16)


def _q_proj(src, rows, norm_g, g_lead, wq, q_gain, qg_lead, tm):
    d = src.arr.shape[1]
    emit = _is_f32(wq)
    tn = CAST_COL_TILE if emit else d
    hpb = tn // HEAD_DIM
    out_shape = [jax.ShapeDtypeStruct((N_HEADS, rows, HEAD_DIM), BF16)]
    out_specs = [pl.BlockSpec((hpb, tm, HEAD_DIM), lambda i, j: (j, i, 0))]
    if emit:
        assert rows == tm
        out_shape += [jax.ShapeDtypeStruct((d, d), BF16)]
        out_specs += [pl.BlockSpec((d, tn), lambda i, j: (0, j))]
    res = pl.pallas_call(
        functools.partial(_q_kernel, emit=emit),
        out_shape=out_shape,
        grid=(rows // tm, d // tn),
        in_specs=[
            pl.BlockSpec((tm, d), lambda i, j: (src.first + i, 0)),
            _lead_spec(g_lead, (1, d)),
            _wspec(wq, (d, tn), lambda i, j: (0, j)),
            _lead_spec(qg_lead, (1, HEAD_DIM)),
        ],
        out_specs=out_specs,
        scratch_shapes=[pltpu.VMEM((tm, d), BF16)],
        compiler_params=_params(("parallel", "arbitrary")),
        name="q_proj",
    )(src.arr, norm_g, wq.arr, q_gain)
    return res[0], tuple(_weight(a) for a in res[1:])


def _o_kernel(*refs, emit, att_transposed, aliased):
    h_ref, att_ref, wo_ref, out_ref, *rest = refs[aliased:]
    wo16_ref = rest[0] if emit else None
    wo = _bf16_weight(wo_ref, wo16_ref)
    if att_transposed:
        att_t = att_ref[...].reshape(D_MODEL, att_ref.shape[2])
        y = lax.dot_general(att_t, wo, (((0,), (0,)), ((), ())), preferred_element_type=F32)
    else:
        att2d_ref = rest[-1]

        @pl.when(pl.program_id(1) == 0)
        def _():
            for hd in range(N_HEADS):
                att2d_ref[:, hd * HEAD_DIM:(hd + 1) * HEAD_DIM] = att_ref[hd]

        y = jnp.dot(att2d_ref[...], wo, preferred_element_type=F32)
    out_ref[...] = h_ref[...] + y


def _o_proj(src, rows, dest, att16, wo, tm, att_transposed):
    d = src.arr.shape[1]
    emit = _is_f32(wo)
    tn = CAST_COL_TILE if emit else d
    out_shape = [jax.ShapeDtypeStruct((dest.rows, d), F32)]
    out_specs = [pl.BlockSpec((tm, tn), lambda i, j: (dest.first + i, j))]
    if emit:
        assert rows == tm
        out_shape += [jax.ShapeDtypeStruct((d, d), BF16)]
        out_specs += [pl.BlockSpec((d, tn), lambda i, j: (0, j))]
    if att_transposed:
        att_spec = pl.BlockSpec((N_HEADS, HEAD_DIM, tm), lambda i, j: (0, 0, i))
        scratch = []
    else:
        att_spec = pl.BlockSpec((N_HEADS, tm, HEAD_DIM), lambda i, j: (0, i, 0))
        scratch = [pltpu.VMEM((tm, d), BF16)]
    alias_specs, alias_args, aliases = _dest_args(dest)
    res = pl.pallas_call(
        functools.partial(_o_kernel, emit=emit, att_transposed=att_transposed,
                          aliased=len(alias_args)),
        out_shape=out_shape,
        grid=(rows // tm, d // tn),
        in_specs=alias_specs + [
            pl.BlockSpec((tm, tn), lambda i, j: (src.first + i, j)),
            att_spec,
            _wspec(wo, (d, tn), lambda i, j: (0, j)),
        ],
        out_specs=out_specs,
        scratch_shapes=scratch,
        input_output_aliases=aliases,
        compiler_params=_params(("parallel", "arbitrary")),
        name="o_proj",
    )(*alias_args, src.arr, att16, wo.arr)
    return res[0], tuple(_weight(a) for a in res[1:])


def _build_prompt_bias(near_ref, far_ref, bias_ref):
    assert ATTN_WIN == 3 * ATTN_SUB and PAST_ROWS - MAX_REL == ATTN_WIN // 2
    row = lax.broadcasted_iota(jnp.int32, (ATTN_SUB, ATTN_SUB), 0)
    col = lax.broadcasted_iota(jnp.int32, (ATTN_SUB, ATTN_SUB), 1)
    for h in range(N_HEADS):
        circ = jnp.broadcast_to(near_ref[h], (ATTN_SUB, ATTN_SUB))
        for bit in range(ATTN_SUB.bit_length() - 1):
            circ = jnp.where((row >> bit) & 1 == 1, pltpu.roll(circ, 1 << bit, 1), circ)
        far = jnp.broadcast_to(far_ref[h], (ATTN_SUB, ATTN_SUB))
        for t in range(ATTN_WIN // ATTN_SUB):
            key = row + t * ATTN_SUB
            kc, qc = key // CHUNK, col // CHUNK
            band = (kc >= qc) & (kc <= qc + N_PAST_CHUNKS)
            val = jnp.where(key - col <= ATTN_WIN // 2, far, circ)
            bias_ref[h, t * ATTN_SUB:(t + 1) * ATTN_SUB, :] = jnp.where(band, val, NEG_INF)


def _attn_prompt_kernel(q_ref, kp_ref, kc_ref, vp_ref, vc_ref, near_ref, far_ref, o_ref,
                        bias_ref, s_refs, p_refs):
    i = pl.program_id(1)

    @pl.when((pl.program_id(0) == 0) & (i == 0))
    def _():
        _build_prompt_bias(near_ref, far_ref, bias_ref)

    n_before_start = jnp.where(i == 0, ATTN_Q_TILE, 0)
    n_chunks = ATTN_WIN // CHUNK
    groups = CHUNK // SUBLANES

    subs = ATTN_Q_TILE // ATTN_SUB
    n_units = ATTN_HEADS_PER_STEP * subs

    def scores(h0, u):
        h, lo = h0 + u // subs, (u % subs) * ATTN_SUB
        q = q_ref[h, lo:lo + ATTN_SUB, :]
        n_past = ATTN_Q_TILE - lo
        dims = (((1,), (1,)), ((), ()))
        s_refs[u, 0:n_past, :] = lax.dot_general(kp_ref[h, lo:, :], q, dims,
                                                 preferred_element_type=F32)
        s_refs[u, n_past:, :] = lax.dot_general(kc_ref[h, 0:ATTN_WIN - n_past, :], q, dims,
                                                preferred_element_type=F32)

    def softmax_and_values(h0, u):
        h, lo = h0 + u // subs, (u % subs) * ATTN_SUB
        s_ref, p_ref = s_refs.at[u], p_refs.at[u]
        m8 = jnp.full((SUBLANES, ATTN_SUB), NEG_INF, F32)
        for c in range(n_chunks):
            rows = slice(c * CHUNK, (c + 1) * CHUNK)
            before_start = c * CHUNK < n_before_start - lo
            x = s_ref[rows, :] * (ATTN_SCALE * LOG2_E) + bias_ref[h, rows, :]
            x = jnp.where(before_start, NEG_INF, x)
            s_ref[rows, :] = x
            m8 = jnp.maximum(m8, jnp.max(x.reshape(groups, SUBLANES, ATTN_SUB), axis=0))
        m = jnp.max(m8, axis=0, keepdims=True)
        l8 = jnp.zeros((SUBLANES, ATTN_SUB), F32)
        for c in range(n_chunks):
            rows = slice(c * CHUNK, (c + 1) * CHUNK)
            e = jnp.exp2(s_ref[rows, :] - m)
            p_ref[rows, :] = e.astype(BF16)
            l8 = l8 + jnp.sum(e.reshape(groups, SUBLANES, ATTN_SUB), axis=0)
        inv = 1.0 / jnp.sum(l8, axis=0, keepdims=True)
        n_past = ATTN_Q_TILE - lo
        o = jnp.dot(vp_ref[h, :, lo:], p_ref[0:n_past, :], preferred_element_type=F32)
        o = o + jnp.dot(vc_ref[h, :, 0:ATTN_WIN - n_past], p_ref[n_past:, :],
                        preferred_element_type=F32)
        o_ref[h, :, lo:lo + ATTN_SUB] = (o * inv).astype(BF16)

    def step(t, carry):
        h0 = t * ATTN_HEADS_PER_STEP
        scores(h0, 0)
        for u in range(n_units):
            if u + 1 < n_units:
                scores(h0, u + 1)
            softmax_and_values(h0, u)
        return carry

    lax.fori_loop(0, N_HEADS // ATTN_HEADS_PER_STEP, step, 0)


def _prompt_bias_rows(rel_bias):
    assert ATTN_SUB == 2 * MAX_REL
    k_mod = (-jnp.arange(ATTN_SUB)) % ATTN_SUB
    idx = 2 * MAX_REL - (k_mod - ATTN_WIN // 2) % ATTN_SUB
    near = rel_bias[:, None, idx].astype(F32) * LOG2_E
    far = jnp.broadcast_to(rel_bias[:, None, 2 * MAX_REL:].astype(F32) * LOG2_E, near.shape)
    return near, far


def _attn_prompt(q16, k16, v16t, rel_bias, n_seq, seq_len):
    tiles = seq_len // ATTN_Q_TILE
    cur = lambda b, i: b * tiles + i
    past = lambda b, i: b * tiles + jnp.maximum(i - 1, 0)
    rows = lambda at: pl.BlockSpec((N_HEADS, ATTN_Q_TILE, HEAD_DIM), lambda b, i: (0, at(b, i), 0))
    cols = lambda at: pl.BlockSpec((N_HEADS, HEAD_DIM, ATTN_Q_TILE), lambda b, i: (0, 0, at(b, i)))
    near, far = _prompt_bias_rows(rel_bias)
    bias_row = pl.BlockSpec(near.shape, lambda b, i: (0, 0, 0))
    n_units = ATTN_HEADS_PER_STEP * (ATTN_Q_TILE // ATTN_SUB)
    return pl.pallas_call(
        _attn_prompt_kernel,
        out_shape=jax.ShapeDtypeStruct(v16t.shape, BF16),
        grid=(n_seq, tiles),
        in_specs=[rows(cur), rows(past), rows(cur), cols(past), cols(cur), bias_row, bias_row],
        out_specs=cols(cur),
        scratch_shapes=[pltpu.VMEM((N_HEADS, ATTN_WIN, ATTN_SUB), F32),
                        pltpu.VMEM((n_units, ATTN_WIN, ATTN_SUB), F32),
                        pltpu.VMEM((n_units, ATTN_WIN, ATTN_SUB), BF16)],
        compiler_params=_params(("arbitrary", "arbitrary")),
        name="attn_prompt",
    )(q16, k16, k16, v16t, v16t, near, far)


def _attn_sample_kernel(q_ref, kn_ref, vn_ref, ck_ref, cv_ref, bc_ref, bn_ref, o_ref):
    cache_len = ck_ref.shape[1] // N_HEADS
    for h in range(N_HEADS):
        q = q_ref[h]
        kc = ck_ref[0, pl.ds(h, cache_len, stride=N_HEADS), :].astype(BF16)
        vc = cv_ref[0, pl.ds(h, cache_len, stride=N_HEADS), :].astype(BF16)
        dims = (((1,), (1,)), ((), ()))
        sc = lax.dot_general(q, kc, dims, preferred_element_type=F32) * ATTN_SCALE + bc_ref[h]
        sn = lax.dot_general(q, kn_ref[h], dims, preferred_element_type=F32) * ATTN_SCALE + bn_ref[h]
        m = jnp.maximum(jnp.max(sc, axis=-1, keepdims=True), jnp.max(sn, axis=-1, keepdims=True))
        ec = jnp.exp(sc - m)
        en = jnp.exp(sn - m)
        inv = 1.0 / (jnp.sum(ec, axis=-1, keepdims=True) + jnp.sum(en, axis=-1, keepdims=True))
        o = jnp.dot((ec * inv).astype(BF16), vc, preferred_element_type=F32)
        o = o + jnp.dot((en * inv).astype(BF16), vn_ref[h], preferred_element_type=F32)
        o_ref[h] = o.astype(BF16)


def _attn_sample(q16, k16, v16, cache_k, cache_v, rel_bias, n_seq, seq_len):
    cache_len = cache_k.shape[1]
    ck = cache_k.reshape(n_seq, cache_len * N_HEADS, HEAD_DIM)
    cv = cache_v.reshape(n_seq, cache_len * N_HEADS, HEAD_DIM)
    n_far = max(cache_len - MAX_REL, 0)
    a = jnp.arange(seq_len)[:, None]
    w = jnp.arange(n_far, cache_len + seq_len)[None, :]
    idx = jnp.clip(a - w + cache_len, -MAX_REL, MAX_REL) + MAX_REL
    near = rel_bias[:, idx].astype(F32)
    far = jnp.broadcast_to(rel_bias[:, None, 2 * MAX_REL:].astype(F32), (N_HEADS, seq_len, n_far))
    bias_c = jnp.concatenate([far, near[:, :, :cache_len - n_far]], axis=2)
    bias_n = near[:, :, cache_len - n_far:]
    new = pl.BlockSpec((N_HEADS, seq_len, HEAD_DIM), lambda b: (0, b, 0))
    cache = pl.BlockSpec((1, cache_len * N_HEADS, HEAD_DIM), lambda b: (b, 0, 0))
    return pl.pallas_call(
        _attn_sample_kernel,
        out_shape=jax.ShapeDtypeStruct(q16.shape, BF16),
        grid=(n_seq,),
        in_specs=[new, new, new, cache, cache,
                  pl.BlockSpec(bias_c.shape, lambda b: (0, 0, 0)),
                  pl.BlockSpec(bias_n.shape, lambda b: (0, 0, 0))],
        out_specs=new,
        compiler_params=_params(("parallel",)),
        name="attn_sample",
    )(q16, k16, v16, ck, cv, bias_c, bias_n)


def kernel(x_prompt, x_sample, state_conv, cache_k, cache_v, ffn_norm, ffn_w_gate, ffn_w_up,
           ffn_w_down, mix_norm, conv_w_in, conv_w, conv_w_out, kv_norm, w_kv, k_gain,
           w_q, q_gain, rel_bias, w_o):
    depth = ffn_norm.shape[0]
    assert depth == 2 and conv_w_in.shape[0] == 1 and w_q.shape[0] == 1
    d = D_MODEL
    n_p, t_p, _ = x_prompt.shape
    n_s, t_s, _ = x_sample.shape
    m_p, m_s = n_p * t_p, n_s * t_s
    m_all = m_p + m_s
    tm_p, tm_f = PROMPT_ROW_TILE, FFN_ROW_TILE
    assert m_all % tm_f == 0 and m_s < tm_f and m_p % m_s == 0 and m_p % tm_p == 0
    ffn_tiles = m_all // tm_f
    s_first = m_p // m_s

    ffn_norm4 = ffn_norm.reshape(depth, 2, 1, d)
    mix_norm3 = mix_norm.reshape(depth, 1, d)
    kv_norm2 = kv_norm.reshape(1, d)
    k_gain2 = k_gain.reshape(1, HEAD_DIM)
    q_gain3 = q_gain.reshape(-1, 1, HEAD_DIM)

    def ffn(tail, main, layer, slot, final=False):
        w32 = [_weight(w, (layer, slot)) for w in (ffn_w_gate, ffn_w_up, ffn_w_down)]
        rows = m_p if final else m_all
        *ys, w16 = _ffn(tail, 1, Dest(rows, ffn_tiles - 1, None), ffn_norm4, (layer, slot),
                        *w32, tm_f, n_split=m_s if final else 0)
        h, _ = _ffn(main, ffn_tiles - 1, Dest(rows, 0, ys[0]), ffn_norm4, (layer, slot),
                    *w16, tm_f)
        return (h, ys[1]) if final else h

    def both(h):
        return Rows(h, ffn_tiles - 1), Rows(h, 0)

    xp = x_prompt.reshape(m_p, d)
    x_tail = jnp.concatenate([xp[m_p - (tm_f - m_s):], x_sample.reshape(m_s, d)], axis=0)
    h = ffn(Rows(x_tail, 0), Rows(xp, 0), 0, 0)

    mix_blocks = d // MIX_COL_TILE
    w_mix32 = (_weight(conv_w_in, (0,)), _weight(conv_w_in, (0,), mix_blocks),
               _weight(conv_w_in, (0,), 2 * mix_blocks))
    h2, conv_s, w_mix16 = _mixer(Rows(h, s_first), m_s, Dest(m_all, s_first, None), mix_norm3, (0,),
                                 *w_mix32, conv_w, (0,), _weight(conv_w_out, (0,)),
                                 state_conv[0], t_s, m_s, MIX_COL_TILE)
    conv_zero = jnp.zeros((n_p, CONV_WIDTH - 1, d), F32)
    h, conv_p, _ = _mixer(Rows(h, 0), m_p, Dest(m_all, 0, h2), mix_norm3, (0,),
                          *w_mix16[:3], conv_w, (0,), w_mix16[3], conv_zero, t_p,
                          PROMPT_MIX_ROW_TILE, PROMPT_MIX_COL_TILE, PROMPT_MIX_PARTS)

    h = ffn(*both(h), 0, 1)

    proj_blocks = d // CAST_COL_TILE
    (k_s, v_s, k16_s, v16_s), w_kv16 = _kv_proj(
        Rows(h, s_first), m_s, kv_norm2, _weight(w_kv), _weight(w_kv, (), proj_blocks), k_gain2,
        t_s, m_s, v_transposed=False)
    (k_p, v_p, k16_p, v16t_p), _ = _kv_proj(
        Rows(h, 0), m_p, kv_norm2, *w_kv16, k_gain2, t_p, tm_p, v_transposed=True)

    h = ffn(*both(h), 1, 0)

    q16_s, w_q16 = _q_proj(Rows(h, s_first), m_s, mix_norm3, (1,), _weight(w_q, (0,)),
                           q_gain3, (0,), m_s)
    q16_p, _ = _q_proj(Rows(h, 0), m_p, mix_norm3, (1,), *w_q16, q_gain3, (0,), tm_p)
    att_s = _attn_sample(q16_s, k16_s, v16_s, cache_k, cache_v, rel_bias[0], n_s, t_s)
    att_p = _attn_prompt(q16_p, k16_p, v16t_p, rel_bias[0], n_p, t_p)
    h2, w_o16 = _o_proj(Rows(h, s_first), m_s, Dest(m_all, s_first, None), att_s,
                        _weight(w_o, (0,)), m_s, att_transposed=False)
    h, _ = _o_proj(Rows(h, 0), m_p, Dest(m_all, 0, h2), att_p, *w_o16, tm_p, att_transposed=True)

    y_p, y_s = ffn(*both(h), 1, 1, final=True)

    keep_p, keep_s = min(PAST_ROWS, t_p), min(PAST_ROWS, t_s)
    heads = (N_HEADS, HEAD_DIM)
    return (y_p.reshape(n_p, t_p, d), y_s.reshape(n_s, t_s, d), conv_p[None],
            k_p.reshape(n_p, keep_p, *heads), v_p.reshape(n_p, keep_p, *heads), conv_s[None],
            k_s.reshape(n_s, keep_s, *heads), v_s.reshape(n_s, keep_s, *heads))
```

```python
import collections
import functools

import jax
import jax.numpy as jnp
from jax import lax
from jax.experimental import pallas as pl
from jax.experimental.pallas import tpu as pltpu

F32 = jnp.float32
BF16 = jnp.bfloat16

D_MODEL = 2048
HEAD_DIM = 128
N_HEADS = D_MODEL // HEAD_DIM
CHUNK = 64
N_PAST_CHUNKS = 8
PAST_ROWS = N_PAST_CHUNKS * CHUNK
MAX_REL = 128
CONV_WIDTH = 3
EPS = 1e-6
FFN_RES = 0.5
NEG_INF = -1e30
ATTN_SCALE = HEAD_DIM ** -0.5
LOG2_E = 1.4426950408889634

SUBLANES = 8
LANES = 128
V7X_VMEM_BYTES = 64 * 1024 * 1024
VMEM_LIMIT_BYTES = V7X_VMEM_BYTES - 8 * 1024 * 1024

PROMPT_ROW_TILE = 512
FFN_ROW_TILE = 1040
FFN_COL_TILE = 512
FFN_CAST_COL_TILE = 256
MIX_COL_TILE = 512
PROMPT_MIX_ROW_TILE = 1024
PROMPT_MIX_PARTS = 2
PROMPT_MIX_COL_TILE = 256
KV_COL_TILE = 1024
CAST_COL_TILE = 512
ATTN_Q_TILE = 512
ATTN_SUB = 256
ATTN_WIN = ATTN_SUB + PAST_ROWS
ATTN_HEADS_PER_STEP = 4

Weight = collections.namedtuple("Weight", ["arr", "lead", "col_off", "col_tiled"])
Rows = collections.namedtuple("Rows", ["arr", "first"])
Dest = collections.namedtuple("Dest", ["rows", "first", "alias"])


def _weight(arr, lead=(), col_off=0, col_tiled=False):
    return Weight(arr, tuple(lead), col_off, col_tiled)


def _wspec(w, block, index):
    lead = w.lead

    def index_map(*ids):
        r, c = index(*ids)
        return lead + (r, c + w.col_off)

    return pl.BlockSpec((None,) * len(lead) + block, index_map)


def _col_spec(w, rows, tn, col):
    if w.col_tiled:
        assert w.arr.shape[1:] == (rows, tn)
        return pl.BlockSpec((None, rows, tn), lambda *ids: (col(*ids), 0, 0))
    return _wspec(w, (rows, tn), lambda *ids: (0, col(*ids)))


def _lead_spec(lead, block):
    lead = tuple(lead)
    return pl.BlockSpec((None,) * len(lead) + block, lambda *ids: lead + (0,) * len(block))


def _is_f32(w):
    return w.arr.dtype == F32


def _bf16_weight(w_ref, w16_ref):
    if w16_ref is None:
        return w_ref[...]
    w16 = w_ref[...].astype(BF16)
    w16_ref[...] = w16
    return w16


def _dest_args(dest):
    if dest.alias is None:
        return [], [], {}
    return [pl.BlockSpec(memory_space=pl.ANY)], [dest.alias], {0: 0}


def _params(semantics):
    return pltpu.CompilerParams(dimension_semantics=semantics,
                                vmem_limit_bytes=VMEM_LIMIT_BYTES)


def _rms_rows(x, g):
    ms = jnp.mean(x * x, axis=-1, keepdims=True)
    return (x * lax.rsqrt(ms + EPS)) * g


def _ffn_kernel(*refs, emit, aliased, n_split):
    h_ref, g_ref, wg_ref, wu_ref, wd_ref, out_ref, *rest = refs[aliased:]
    split_ref = rest.pop(0) if n_split else None
    wg16_ref, wu16_ref, wd16_ref = rest[:3] if emit else (None, None, None)
    xn_ref = rest[-1]
    f = pl.program_id(1)

    @pl.when(f == 0)
    def _():
        h = h_ref[...]
        xn_ref[...] = _rms_rows(h, g_ref[...]).astype(BF16)
        out_ref[...] = h

    xn = xn_ref[...]
    gate = jnp.dot(xn, _bf16_weight(wg_ref, wg16_ref), preferred_element_type=F32)
    up = jnp.dot(xn, _bf16_weight(wu_ref, wu16_ref), preferred_element_type=F32)
    act = (gate * jax.nn.sigmoid(gate) * (FFN_RES * up)).astype(BF16)
    out_ref[...] += jnp.dot(act, _bf16_weight(wd_ref, wd16_ref), preferred_element_type=F32)

    if n_split:
        @pl.when(f == pl.num_programs(1) - 1)
        def _():
            split_ref[...] = out_ref[out_ref.shape[0] - n_split:, :]


def _ffn(src, n_tiles, dest, norm_g, g_lead, wg, wu, wd, tm, n_split=0):
    d = src.arr.shape[1]
    d_ff = wd.arr.shape[-2]
    emit = _is_f32(wg)
    tf = FFN_CAST_COL_TILE if emit else FFN_COL_TILE
    out_shape = [jax.ShapeDtypeStruct((dest.rows, d), F32)]
    out_specs = [pl.BlockSpec((tm, d), lambda i, f: (dest.first + i, 0))]
    if n_split:
        assert n_tiles == 1
        out_shape += [jax.ShapeDtypeStruct((n_split, d), F32)]
        out_specs += [pl.BlockSpec((n_split, d), lambda i, f: (0, 0))]
    h_mode = {}
    if emit:
        assert n_tiles == 1
        per = FFN_COL_TILE // tf
        tiled = jax.ShapeDtypeStruct((d_ff // FFN_COL_TILE, d, FFN_COL_TILE), BF16)
        out_shape += [tiled, tiled, jax.ShapeDtypeStruct((d_ff, d), BF16)]
        out_specs += [pl.BlockSpec((None, d, tf), lambda i, f: (f // per, 0, f % per)),
                      pl.BlockSpec((None, d, tf), lambda i, f: (f // per, 0, f % per)),
                      pl.BlockSpec((tf, d), lambda i, f: (f, 0))]
        h_mode = dict(pipeline_mode=pl.Buffered(1))
    alias_specs, alias_args, aliases = _dest_args(dest)
    res = pl.pallas_call(
        functools.partial(_ffn_kernel, emit=emit, aliased=len(alias_args), n_split=n_split),
        out_shape=out_shape,
        grid=(n_tiles, d_ff // tf),
        in_specs=alias_specs + [
            pl.BlockSpec((tm, d), lambda i, f: (src.first + i, 0), **h_mode),
            _lead_spec(g_lead, (1, d)),
            _col_spec(wg, d, tf, lambda i, f: f),
            _col_spec(wu, d, tf, lambda i, f: f),
            _wspec(wd, (tf, d), lambda i, f: (f, 0)),
        ],
        out_specs=out_specs,
        scratch_shapes=[pltpu.VMEM((tm, d), BF16)],
        input_output_aliases=aliases,
        compiler_params=_params(("parallel", "arbitrary")),
        name="ffn",
    )(*alias_args, src.arr, norm_g, wg.arr, wu.arr, wd.arr)
    n_act = 2 if n_split else 1
    w16 = tuple(_weight(a, col_tiled=a.ndim == 3) for a in res[n_act:])
    return tuple(res[:n_act]) + (w16,)


def _mixer_kernel(*refs, seq_len, tm, emit, aliased, n_parts):
    (h_ref, g_ref, wb_ref, wc_ref, wx_ref, cw_ref, wo_ref, p0_ref, p1_ref,
     out_ref, st_ref, *rest) = refs[aliased:]
    wb16_ref, wc16_ref, wx16_ref, wo16_ref = rest[:4] if emit else (None,) * 4
    xn_ref, carry_ref = rest[-2:]
    i = pl.program_id(0)
    j = pl.program_id(1)

    @pl.when(j == 0)
    def _():
        h = h_ref[...]
        xn_ref[...] = _rms_rows(h, g_ref[...]).astype(BF16)
        out_ref[...] = h

    if seq_len >= tm:
        @pl.when((i % (seq_len // tm)) == 0)
        def _():
            carry_ref[j] = p0_ref[0]

    wb, wc, wx, wo = (_bf16_weight(wb_ref, wb16_ref), _bf16_weight(wc_ref, wc16_ref),
                      _bf16_weight(wx_ref, wx16_ref), _bf16_weight(wo_ref, wo16_ref))
    cw = cw_ref[...]
    tn = cw.shape[1]
    hm = tm // n_parts
    bs, us = [], []
    for part in range(n_parts):
        xn = xn_ref[part * hm:(part + 1) * hm, :]
        bs.append(jnp.dot(xn, wb, preferred_element_type=F32))
        c = jnp.dot(xn, wc, preferred_element_type=F32)
        x = jnp.dot(xn, wx, preferred_element_type=F32)
        us.append(c * x)
    row = lax.broadcasted_iota(jnp.int32, (hm, tn), 0)
    for part in range(n_parts):
        u = us[part]
        if seq_len >= tm:
            prev = carry_ref[j] if part == 0 else us[part - 1][hm - 2:hm, :]
            p0 = prev[0:1, :]
            p1 = prev[1:2, :]
            pos = row
        else:
            p0 = p0_ref[...]
            p1 = p1_ref[...]
            pos = row % seq_len
        um1 = jnp.where(pos == 0, p1, pltpu.roll(u, 1, 0))
        um2 = jnp.where(pos == 0, p0, jnp.where(pos == 1, p1, pltpu.roll(u, 2, 0)))
        conv = cw[0:1, :] * um2 + cw[1:2, :] * um1 + cw[2:3, :] * u
        v = (bs[part] * conv).astype(BF16)
        out_ref[part * hm:(part + 1) * hm, :] += jnp.dot(v, wo, preferred_element_type=F32)
    if seq_len >= tm:
        carry_ref[j] = us[-1][hm - 2:hm, :]
        st_ref[0] = us[-1][hm - 2:hm, :]
    else:
        st_ref[...] = us[0].reshape(tm // seq_len, seq_len, tn)[:, seq_len - 2:seq_len, :]


def _mixer(src, rows, dest, norm_g, g_lead, wb, wc, wx, conv_w, cw_lead, wo, state, seq_len, tm,
           tn, n_parts=1):
    d = src.arr.shape[1]
    n_seq = rows // seq_len
    n_j = d // tn
    n_i = rows // tm
    emit = _is_f32(wb)
    if seq_len >= tm:
        tiles_per_seq = seq_len // tm
        p0, p1 = state, state
        p_spec = pl.BlockSpec((1, CONV_WIDTH - 1, tn), lambda i, j: (i // tiles_per_seq, 0, j))
        st_shape = jax.ShapeDtypeStruct((n_i, CONV_WIDTH - 1, d), F32)
        st_spec = pl.BlockSpec((1, CONV_WIDTH - 1, tn), lambda i, j: (i, 0, j))
    else:
        assert tm == rows and tm % seq_len == 0 and n_parts == 1
        p0 = jnp.repeat(state[:, 0], seq_len, axis=0)
        p1 = jnp.repeat(state[:, 1], seq_len, axis=0)
        p_spec = pl.BlockSpec((tm, tn), lambda i, j: (i, j))
        st_shape = jax.ShapeDtypeStruct((n_seq, CONV_WIDTH - 1, d), F32)
        st_spec = pl.BlockSpec((n_seq, CONV_WIDTH - 1, tn), lambda i, j: (0, 0, j))
    out_shape = [jax.ShapeDtypeStruct((dest.rows, d), F32), st_shape]
    out_specs = [pl.BlockSpec((tm, d), lambda i, j: (dest.first + i, 0)), st_spec]
    if emit:
        assert n_i == 1
        out_shape += [jax.ShapeDtypeStruct((d, d), BF16)] * 4
        out_specs += [pl.BlockSpec((d, tn), lambda i, j: (0, j))] * 3
        out_specs += [pl.BlockSpec((tn, d), lambda i, j: (j, 0))]
    alias_specs, alias_args, aliases = _dest_args(dest)
    res = pl.pallas_call(
        functools.partial(_mixer_kernel, seq_len=seq_len, tm=tm, emit=emit,
                          aliased=len(alias_args), n_parts=n_parts),
        out_shape=out_shape,
        grid=(n_i, n_j),
        in_specs=alias_specs + [
            pl.BlockSpec((tm, d), lambda i, j: (src.first + i, 0)),
            _lead_spec(g_lead, (1, d)),
            _wspec(wb, (d, tn), lambda i, j: (0, j)),
            _wspec(wc, (d, tn), lambda i, j: (0, j)),
            _wspec(wx, (d, tn), lambda i, j: (0, j)),
            pl.BlockSpec((None,) * len(cw_lead) + (CONV_WIDTH, tn),
                         lambda i, j: tuple(cw_lead) + (0, j)),
            _wspec(wo, (tn, d), lambda i, j: (j, 0)),
            p_spec,
            p_spec,
        ],
        out_specs=out_specs,
        scratch_shapes=[pltpu.VMEM((tm, d), BF16),
                        pltpu.VMEM((n_j, CONV_WIDTH - 1, tn), F32)],
        input_output_aliases=aliases,
        compiler_params=_params(("arbitrary", "arbitrary")),
        name="conv_mixer",
    )(*alias_args, src.arr, norm_g, wb.arr, wc.arr, wx.arr, conv_w, wo.arr, p0, p1)
    out, st = res[0], res[1]
    if seq_len >= tm:
        st = st[seq_len // tm - 1::seq_len // tm]
    return out, st, tuple(_weight(a) for a in res[2:])


def _kv_kernel(h_ref, g_ref, wk_ref, wv_ref, kg_ref, k32_ref, v32_ref, k16_ref, v16_ref,
               *rest, emit, keep, v_transposed):
    wk16_ref, wv16_ref = rest if emit else (None, None)
    xn = _rms_rows(h_ref[...], g_ref[...]).astype(BF16)
    k = jnp.dot(xn, _bf16_weight(wk_ref, wk16_ref), preferred_element_type=F32)
    v = jnp.dot(xn, _bf16_weight(wv_ref, wv16_ref), preferred_element_type=F32)
    kg = kg_ref[...]
    tm = k.shape[0]
    v32_ref[...] = v[tm - keep:, :]
    for hd in range(k.shape[1] // HEAD_DIM):
        cols = slice(hd * HEAD_DIM, (hd + 1) * HEAD_DIM)
        kh = _rms_rows(k[:, cols], kg)
        k32_ref[:, cols] = kh[tm - keep:, :]
        k16_ref[hd] = kh.astype(BF16)
        if v_transposed:
            v16_ref[hd] = v[:, cols].T.astype(BF16)
        else:
            v16_ref[hd] = v[:, cols].astype(BF16)


def _kv_proj(src, rows, norm_g, wk, wv, k_gain, seq_len, tm, v_transposed):
    d = src.arr.shape[1]
    emit = _is_f32(wk)
    tn = CAST_COL_TILE if emit else KV_COL_TILE
    hpb = tn // HEAD_DIM
    if seq_len >= tm:
        keep = min(PAST_ROWS, seq_len)
        assert keep <= tm and seq_len % tm == 0
        tiles_per_seq = seq_len // tm
        kept_rows = (rows // seq_len) * keep
        kept_spec = pl.BlockSpec((keep, tn), lambda j, i: (i // tiles_per_seq, j))
    else:
        assert tm % seq_len == 0
        keep, kept_rows = tm, rows
        kept_spec = pl.BlockSpec((tm, tn), lambda j, i: (i, j))
    if v_transposed:
        v16_shape = jax.ShapeDtypeStruct((N_HEADS, HEAD_DIM, rows), BF16)
        v16_spec = pl.BlockSpec((hpb, HEAD_DIM, tm), lambda j, i: (j, 0, i))
    else:
        v16_shape = jax.ShapeDtypeStruct((N_HEADS, rows, HEAD_DIM), BF16)
        v16_spec = pl.BlockSpec((hpb, tm, HEAD_DIM), lambda j, i: (j, i, 0))
    out_shape = [jax.ShapeDtypeStruct((kept_rows, d), F32),
                 jax.ShapeDtypeStruct((kept_rows, d), F32),
                 jax.ShapeDtypeStruct((N_HEADS, rows, HEAD_DIM), BF16), v16_shape]
    out_specs = [kept_spec, kept_spec,
                 pl.BlockSpec((hpb, tm, HEAD_DIM), lambda j, i: (j, i, 0)), v16_spec]
    if emit:
        assert rows == tm
        out_shape += [jax.ShapeDtypeStruct((d, d), BF16)] * 2
        out_specs += [pl.BlockSpec((d, tn), lambda j, i: (0, j))] * 2
    res = pl.pallas_call(
        functools.partial(_kv_kernel, emit=emit, keep=keep, v_transposed=v_transposed),
        out_shape=out_shape,
        grid=(d // tn, rows // tm),
        in_specs=[
            pl.BlockSpec((tm, d), lambda j, i: (src.first + i, 0)),
            pl.BlockSpec((1, d), lambda j, i: (0, 0)),
            _wspec(wk, (d, tn), lambda j, i: (0, j)),
            _wspec(wv, (d, tn), lambda j, i: (0, j)),
            pl.BlockSpec((1, HEAD_DIM), lambda j, i: (0, 0)),
        ],
        out_specs=out_specs,
        compiler_params=_params(("arbitrary", "arbitrary")),
        name="kv_proj",
    )(src.arr, norm_g, wk.arr, wv.arr, k_gain)
    return res[:4], tuple(_weight(a) for a in res[4:])


def _q_kernel(h_ref, g_ref, wq_ref, qg_ref, q16_ref, *rest, emit):
    wq16_ref = rest[0] if emit else None
    xn_ref = rest[-1]

    @pl.when(pl.program_id(1) == 0)
    def _():
        xn_ref[...] = _rms_rows(h_ref[...], g_ref[...]).astype(BF16)

    q = jnp.dot(xn_ref[...], _bf16_weight(wq_ref, wq16_ref), preferred_element_type=F32)
    qg = qg_ref[...]
    for hd in range(q.shape[1] // HEAD_DIM):
        cols = slice(hd * HEAD_DIM, (hd + 1) * HEAD_DIM)
        q16_ref[hd] = _rms_rows(q[:, cols], qg).astype(BF16)


def _q_proj(src, rows, norm_g, g_lead, wq, q_gain, qg_lead, tm):
    d = src.arr.shape[1]
    emit = _is_f32(wq)
    tn = CAST_COL_TILE if emit else d
    hpb = tn // HEAD_DIM
    out_shape = [jax.ShapeDtypeStruct((N_HEADS, rows, HEAD_DIM), BF16)]
    out_specs = [pl.BlockSpec((hpb, tm, HEAD_DIM), lambda i, j: (j, i, 0))]
    if emit:
        assert rows == tm
        out_shape += [jax.ShapeDtypeStruct((d, d), BF16)]
        out_specs += [pl.BlockSpec((d, tn), lambda i, j: (0, j))]
    res = pl.pallas_call(
        functools.partial(_q_kernel, emit=emit),
        out_shape=out_shape,
        grid=(rows // tm, d // tn),
        in_specs=[
            pl.BlockSpec((tm, d), lambda i, j: (src.first + i, 0)),
            _lead_spec(g_lead, (1, d)),
            _wspec(wq, (d, tn), lambda i, j: (0, j)),
            _lead_spec(qg_lead, (1, HEAD_DIM)),
        ],
        out_specs=out_specs,
        scratch_shapes=[pltpu.VMEM((tm, d), BF16)],
        compiler_params=_params(("parallel", "arbitrary")),
        name="q_proj",
    )(src.arr, norm_g, wq.arr, q_gain)
    return res[0], tuple(_weight(a) for a in res[1:])


def _o_kernel(*refs, emit, att_transposed, aliased):
    h_ref, att_ref, wo_ref, out_ref, *rest = refs[aliased:]
    wo16_ref = rest[0] if emit else None
    wo = _bf16_weight(wo_ref, wo16_ref)
    if att_transposed:
        att_t = att_ref[...].reshape(D_MODEL, att_ref.shape[2])
        y = lax.dot_general(att_t, wo, (((0,), (0,)), ((), ())), preferred_element_type=F32)
    else:
        att2d_ref = rest[-1]

        @pl.when(pl.program_id(1) == 0)
        def _():
            for hd in range(N_HEADS):
                att2d_ref[:, hd * HEAD_DIM:(hd + 1) * HEAD_DIM] = att_ref[hd]

        y = jnp.dot(att2d_ref[...], wo, preferred_element_type=F32)
    out_ref[...] = h_ref[...] + y


def _o_proj(src, rows, dest, att16, wo, tm, att_transposed):
    d = src.arr.shape[1]
    emit = _is_f32(wo)
    tn = CAST_COL_TILE if emit else d
    out_shape = [jax.ShapeDtypeStruct((dest.rows, d), F32)]
    out_specs = [pl.BlockSpec((tm, tn), lambda i, j: (dest.first + i, j))]
    if emit:
        assert rows == tm
        out_shape += [jax.ShapeDtypeStruct((d, d), BF16)]
        out_specs += [pl.BlockSpec((d, tn), lambda i, j: (0, j))]
    if att_transposed:
        att_spec = pl.BlockSpec((N_HEADS, HEAD_DIM, tm), lambda i, j: (0, 0, i))
        scratch = []
    else:
        att_spec = pl.BlockSpec((N_HEADS, tm, HEAD_DIM), lambda i, j: (0, i, 0))
        scratch = [pltpu.VMEM((tm, d), BF16)]
    alias_specs, alias_args, aliases = _dest_args(dest)
    res = pl.pallas_call(
        functools.partial(_o_kernel, emit=emit, att_transposed=att_transposed,
                          aliased=len(alias_args)),
        out_shape=out_shape,
        grid=(rows // tm, d // tn),
        in_specs=alias_specs + [
            pl.BlockSpec((tm, tn), lambda i, j: (src.first + i, j)),
            att_spec,
            _wspec(wo, (d, tn), lambda i, j: (0, j)),
        ],
        out_specs=out_specs,
        scratch_shapes=scratch,
        input_output_aliases=aliases,
        compiler_params=_params(("parallel", "arbitrary")),
        name="o_proj",
    )(*alias_args, src.arr, att16, wo.arr)
    return res[0], tuple(_weight(a) for a in res[1:])


def _build_prompt_bias(near_ref, far_ref, bias_ref):
    assert ATTN_WIN == 3 * ATTN_SUB and PAST_ROWS - MAX_REL == ATTN_WIN // 2
    row = lax.broadcasted_iota(jnp.int32, (ATTN_SUB, ATTN_SUB), 0)
    col = lax.broadcasted_iota(jnp.int32, (ATTN_SUB, ATTN_SUB), 1)
    for h in range(N_HEADS):
        circ = jnp.broadcast_to(near_ref[h], (ATTN_SUB, ATTN_SUB))
        for bit in range(ATTN_SUB.bit_length() - 1):
            circ = jnp.where((row >> bit) & 1 == 1, pltpu.roll(circ, 1 << bit, 1), circ)
        far = jnp.broadcast_to(far_ref[h], (ATTN_SUB, ATTN_SUB))
        for t in range(ATTN_WIN // ATTN_SUB):
            key = row + t * ATTN_SUB
            kc, qc = key // CHUNK, col // CHUNK
            band = (kc >= qc) & (kc <= qc + N_PAST_CHUNKS)
            val = jnp.where(key - col <= ATTN_WIN // 2, far, circ)
            bias_ref[h, t * ATTN_SUB:(t + 1) * ATTN_SUB, :] = jnp.where(band, val, NEG_INF)


def _attn_prompt_kernel(q_ref, kp_ref, kc_ref, vp_ref, vc_ref, near_ref, far_ref, o_ref,
                        bias_ref, s_refs, p_refs):
    i = pl.program_id(1)

    @pl.when((pl.program_id(0) == 0) & (i == 0))
    def _():
        _build_prompt_bias(near_ref, far_ref, bias_ref)
        p_refs[...] = jnp.zeros(p_refs.shape, BF16)

    n_before_start = jnp.where(i == 0, ATTN_Q_TILE, 0)
    groups = CHUNK // SUBLANES
    lane_tiles = ATTN_SUB // LANES
    live_blocks = [(c, lt) for c in range(ATTN_WIN // CHUNK) for lt in range(lane_tiles)
                   if lt * LANES // CHUNK <= c <= ((lt + 1) * LANES - 1) // CHUNK + N_PAST_CHUNKS]

    subs = ATTN_Q_TILE // ATTN_SUB
    n_units = ATTN_HEADS_PER_STEP * subs

    def scores(h0, u):
        h, lo = h0 + u // subs, (u % subs) * ATTN_SUB
        q = q_ref[h, lo:lo + ATTN_SUB, :]
        n_past = ATTN_Q_TILE - lo
        dims = (((1,), (1,)), ((), ()))
        s_refs[u, 0:n_past, :] = lax.dot_general(kp_ref[h, lo:, :], q, dims,
                                                 preferred_element_type=F32)
        s_refs[u, n_past:, :] = lax.dot_general(kc_ref[h, 0:ATTN_WIN - n_past, :], q, dims,
                                                preferred_element_type=F32)

    def softmax_and_values(h0, u):
        h, lo = h0 + u // subs, (u % subs) * ATTN_SUB
        s_ref, p_ref = s_refs.at[u], p_refs.at[u]
        m8 = [jnp.full((SUBLANES, LANES), NEG_INF, F32) for _ in range(lane_tiles)]
        for c, lt in live_blocks:
            rows, cols = slice(c * CHUNK, (c + 1) * CHUNK), slice(lt * LANES, (lt + 1) * LANES)
            before_start = c * CHUNK < n_before_start - lo
            x = s_ref[rows, cols] * (ATTN_SCALE * LOG2_E) + bias_ref[h, rows, cols]
            x = jnp.where(before_start, NEG_INF, x)
            s_ref[rows, cols] = x
            m8[lt] = jnp.maximum(m8[lt], jnp.max(x.reshape(groups, SUBLANES, LANES), axis=0))
        m = [jnp.max(v, axis=0, keepdims=True) for v in m8]
        l8 = [jnp.zeros((SUBLANES, LANES), F32) for _ in range(lane_tiles)]
        for c, lt in live_blocks:
            rows, cols = slice(c * CHUNK, (c + 1) * CHUNK), slice(lt * LANES, (lt + 1) * LANES)
            e = jnp.exp2(s_ref[rows, cols] - m[lt])
            p_ref[rows, cols] = e.astype(BF16)
            l8[lt] = l8[lt] + jnp.sum(e.reshape(groups, SUBLANES, LANES), axis=0)
        inv = 1.0 / jnp.concatenate([jnp.sum(v, axis=0, keepdims=True) for v in l8], axis=1)
        n_past = ATTN_Q_TILE - lo
        o = jnp.dot(vp_ref[h, :, lo:], p_ref[0:n_past, :], preferred_element_type=F32)
        o = o + jnp.dot(vc_ref[h, :, 0:ATTN_WIN - n_past], p_ref[n_past:, :],
                        preferred_element_type=F32)
        o_ref[h, :, lo:lo + ATTN_SUB] = (o * inv).astype(BF16)

    def step(t, carry):
        h0 = t * ATTN_HEADS_PER_STEP
        scores(h0, 0)
        for u in range(n_units):
            if u + 1 < n_units:
                scores(h0, u + 1)
            softmax_and_values(h0, u)
        return carry

    lax.fori_loop(0, N_HEADS // ATTN_HEADS_PER_STEP, step, 0)


def _prompt_bias_rows(rel_bias):
    assert ATTN_SUB == 2 * MAX_REL
    k_mod = (-jnp.arange(ATTN_SUB)) % ATTN_SUB
    idx = 2 * MAX_REL - (k_mod - ATTN_WIN // 2) % ATTN_SUB
    near = rel_bias[:, None, idx].astype(F32) * LOG2_E
    far = jnp.broadcast_to(rel_bias[:, None, 2 * MAX_REL:].astype(F32) * LOG2_E, near.shape)
    return near, far


def _attn_prompt(q16, k16, v16t, rel_bias, n_seq, seq_len):
    tiles = seq_len // ATTN_Q_TILE
    cur = lambda b, i: b * tiles + i
    past = lambda b, i: b * tiles + jnp.maximum(i - 1, 0)
    rows = lambda at: pl.BlockSpec((N_HEADS, ATTN_Q_TILE, HEAD_DIM), lambda b, i: (0, at(b, i), 0))
    cols = lambda at: pl.BlockSpec((N_HEADS, HEAD_DIM, ATTN_Q_TILE), lambda b, i: (0, 0, at(b, i)))
    near, far = _prompt_bias_rows(rel_bias)
    bias_row = pl.BlockSpec(near.shape, lambda b, i: (0, 0, 0))
    n_units = ATTN_HEADS_PER_STEP * (ATTN_Q_TILE // ATTN_SUB)
    return pl.pallas_call(
        _attn_prompt_kernel,
        out_shape=jax.ShapeDtypeStruct(v16t.shape, BF16),
        grid=(n_seq, tiles),
        in_specs=[rows(cur), rows(past), rows(cur), cols(past), cols(cur), bias_row, bias_row],
        out_specs=cols(cur),
        scratch_shapes=[pltpu.VMEM((N_HEADS, ATTN_WIN, ATTN_SUB), F32),
                        pltpu.VMEM((n_units, ATTN_WIN, ATTN_SUB), F32),
                        pltpu.VMEM((n_units, ATTN_WIN, ATTN_SUB), BF16)],
        compiler_params=_params(("arbitrary", "arbitrary")),
        name="attn_prompt",
    )(q16, k16, k16, v16t, v16t, near, far)


def _attn_sample_kernel(q_ref, kn_ref, vn_ref, ck_ref, cv_ref, bc_ref, bn_ref, o_ref):
    cache_len = ck_ref.shape[1] // N_HEADS
    for h in range(N_HEADS):
        q = q_ref[h]
        kc = ck_ref[0, pl.ds(h, cache_len, stride=N_HEADS), :].astype(BF16)
        vc = cv_ref[0, pl.ds(h, cache_len, stride=N_HEADS), :].astype(BF16)
        dims = (((1,), (1,)), ((), ()))
        sc = lax.dot_general(q, kc, dims, preferred_element_type=F32) * ATTN_SCALE + bc_ref[h]
        sn = lax.dot_general(q, kn_ref[h], dims, preferred_element_type=F32) * ATTN_SCALE + bn_ref[h]
        m = jnp.maximum(jnp.max(sc, axis=-1, keepdims=True), jnp.max(sn, axis=-1, keepdims=True))
        ec = jnp.exp(sc - m)
        en = jnp.exp(sn - m)
        inv = 1.0 / (jnp.sum(ec, axis=-1, keepdims=True) + jnp.sum(en, axis=-1, keepdims=True))
        o = jnp.dot((ec * inv).astype(BF16), vc, preferred_element_type=F32)
        o = o + jnp.dot((en * inv).astype(BF16), vn_ref[h], preferred_element_type=F32)
        o_ref[h] = o.astype(BF16)


def _attn_sample(q16, k16, v16, cache_k, cache_v, rel_bias, n_seq, seq_len):
    cache_len = cache_k.shape[1]
    ck = cache_k.reshape(n_seq, cache_len * N_HEADS, HEAD_DIM)
    cv = cache_v.reshape(n_seq, cache_len * N_HEADS, HEAD_DIM)
    n_far = max(cache_len - MAX_REL, 0)
    a = jnp.arange(seq_len)[:, None]
    w = jnp.arange(n_far, cache_len + seq_len)[None, :]
    idx = jnp.clip(a - w + cache_len, -MAX_REL, MAX_REL) + MAX_REL
    near = rel_bias[:, idx].astype(F32)
    far = jnp.broadcast_to(rel_bias[:, None, 2 * MAX_REL:].astype(F32), (N_HEADS, seq_len, n_far))
    bias_c = jnp.concatenate([far, near[:, :, :cache_len - n_far]], axis=2)
    bias_n = near[:, :, cache_len - n_far:]
    new = pl.BlockSpec((N_HEADS, seq_len, HEAD_DIM), lambda b: (0, b, 0))
    cache = pl.BlockSpec((1, cache_len * N_HEADS, HEAD_DIM), lambda b: (b, 0, 0))
    return pl.pallas_call(
        _attn_sample_kernel,
        out_shape=jax.ShapeDtypeStruct(q16.shape, BF16),
        grid=(n_seq,),
        in_specs=[new, new, new, cache, cache,
                  pl.BlockSpec(bias_c.shape, lambda b: (0, 0, 0)),
                  pl.BlockSpec(bias_n.shape, lambda b: (0, 0, 0))],
        out_specs=new,
        compiler_params=_params(("parallel",)),
        name="attn_sample",
    )(q16, k16, v16, ck, cv, bias_c, bias_n)


def kernel(x_prompt, x_sample, state_conv, cache_k, cache_v, ffn_norm, ffn_w_gate, ffn_w_up,
           ffn_w_down, mix_norm, conv_w_in, conv_w, conv_w_out, kv_norm, w_kv, k_gain,
           w_q, q_gain, rel_bias, w_o):
    depth = ffn_norm.shape[0]
    assert depth == 2 and conv_w_in.shape[0] == 1 and w_q.shape[0] == 1
    d = D_MODEL
    n_p, t_p, _ = x_prompt.shape
    n_s, t_s, _ = x_sample.shape
    m_p, m_s = n_p * t_p, n_s * t_s
    m_all = m_p + m_s
    tm_p, tm_f = PROMPT_ROW_TILE, FFN_ROW_TILE
    assert m_all % tm_f == 0 and m_s < tm_f and m_p % m_s == 0 and m_p % tm_p == 0
    ffn_tiles = m_all // tm_f
    s_first = m_p // m_s

    ffn_norm4 = ffn_norm.reshape(depth, 2, 1, d)
    mix_norm3 = mix_norm.reshape(depth, 1, d)
    kv_norm2 = kv_norm.reshape(1, d)
    k_gain2 = k_gain.reshape(1, HEAD_DIM)
    q_gain3 = q_gain.reshape(-1, 1, HEAD_DIM)

    def ffn(tail, main, layer, slot, final=False):
        w32 = [_weight(w, (layer, slot)) for w in (ffn_w_gate, ffn_w_up, ffn_w_down)]
        rows = m_p if final else m_all
        *ys, w16 = _ffn(tail, 1, Dest(rows, ffn_tiles - 1, None), ffn_norm4, (layer, slot),
                        *w32, tm_f, n_split=m_s if final else 0)
        h, _ = _ffn(main, ffn_tiles - 1, Dest(rows, 0, ys[0]), ffn_norm4, (layer, slot),
                    *w16, tm_f)
        return (h, ys[1]) if final else h

    def both(h):
        return Rows(h, ffn_tiles - 1), Rows(h, 0)

    xp = x_prompt.reshape(m_p, d)
    x_tail = jnp.concatenate([xp[m_p - (tm_f - m_s):], x_sample.reshape(m_s, d)], axis=0)
    h = ffn(Rows(x_tail, 0), Rows(xp, 0), 0, 0)

    mix_blocks = d // MIX_COL_TILE
    w_mix32 = (_weight(conv_w_in, (0,)), _weight(conv_w_in, (0,), mix_blocks),
               _weight(conv_w_in, (0,), 2 * mix_blocks))
    h2, conv_s, w_mix16 = _mixer(Rows(h, s_first), m_s, Dest(m_all, s_first, None), mix_norm3, (0,),
                                 *w_mix32, conv_w, (0,), _weight(conv_w_out, (0,)),
                                 state_conv[0], t_s, m_s, MIX_COL_TILE)
    conv_zero = jnp.zeros((n_p, CONV_WIDTH - 1, d), F32)
    h, conv_p, _ = _mixer(Rows(h, 0), m_p, Dest(m_all, 0, h2), mix_norm3, (0,),
                          *w_mix16[:3], conv_w, (0,), w_mix16[3], conv_zero, t_p,
                          PROMPT_MIX_ROW_TILE, PROMPT_MIX_COL_TILE, PROMPT_MIX_PARTS)

    h = ffn(*both(h), 0, 1)

    proj_blocks = d // CAST_COL_TILE
    (k_s, v_s, k16_s, v16_s), w_kv16 = _kv_proj(
        Rows(h, s_first), m_s, kv_norm2, _weight(w_kv), _weight(w_kv, (), proj_blocks), k_gain2,
        t_s, m_s, v_transposed=False)
    (k_p, v_p, k16_p, v16t_p), _ = _kv_proj(
        Rows(h, 0), m_p, kv_norm2, *w_kv16, k_gain2, t_p, tm_p, v_transposed=True)

    h = ffn(*both(h), 1, 0)

    q16_s, w_q16 = _q_proj(Rows(h, s_first), m_s, mix_norm3, (1,), _weight(w_q, (0,)),
                           q_gain3, (0,), m_s)
    q16_p, _ = _q_proj(Rows(h, 0), m_p, mix_norm3, (1,), *w_q16, q_gain3, (0,), tm_p)
    att_s = _attn_sample(q16_s, k16_s, v16_s, cache_k, cache_v, rel_bias[0], n_s, t_s)
    att_p = _attn_prompt(q16_p, k16_p, v16t_p, rel_bias[0], n_p, t_p)
    h2, w_o16 = _o_proj(Rows(h, s_first), m_s, Dest(m_all, s_first, None), att_s,
                        _weight(w_o, (0,)), m_s, att_transposed=False)
    h, _ = _o_proj(Rows(h, 0), m_p, Dest(m_all, 0, h2), att_p, *w_o16, tm_p, att_transposed=True)

    y_p, y_s = ffn(*both(h), 1, 1, final=True)

    keep_p, keep_s = min(PAST_ROWS, t_p), min(PAST_ROWS, t_s)
    heads = (N_HEADS, HEAD_DIM)
    return (y_p.reshape(n_p, t_p, d), y_s.reshape(n_s, t_s, d), conv_p[None],
            k_p.reshape(n_p, keep_p, *heads), v_p.reshape(n_p, keep_p, *heads), conv_s[None],
            k_s.reshape(n_s, keep_s, *heads), v_s.reshape(n_s, keep_s, *heads))
```

```python
import collections
import functools

import jax
import jax.numpy as jnp
from jax import lax
from jax.experimental import pallas as pl
from jax.experimental.pallas import tpu as pltpu

F32 = jnp.float32
BF16 = jnp.bfloat16

D_MODEL = 2048
HEAD_DIM = 128
N_HEADS = D_MODEL // HEAD_DIM
CHUNK = 64
N_PAST_CHUNKS = 8
PAST_ROWS = N_PAST_CHUNKS * CHUNK
MAX_REL = 128
CONV_WIDTH = 3
EPS = 1e-6
FFN_RES = 0.5
NEG_INF = -1e30
ATTN_SCALE = HEAD_DIM ** -0.5
LOG2_E = 1.4426950408889634

SUBLANES = 8
LANES = 128
V7X_VMEM_BYTES = 64 * 1024 * 1024
VMEM_LIMIT_BYTES = V7X_VMEM_BYTES - 8 * 1024 * 1024

PROMPT_ROW_TILE = 512
FFN_ROW_TILE = 1040
FFN_COL_TILE = 512
FFN_CAST_COL_TILE = 256
MIX_COL_TILE = 512
PROMPT_MIX_ROW_TILE = 1024
PROMPT_MIX_PARTS = 2
PROMPT_MIX_COL_TILE = 256
KV_COL_TILE = 1024
CAST_COL_TILE = 512
ATTN_Q_TILE = 512
ATTN_SUB = 256
ATTN_WIN = ATTN_SUB + PAST_ROWS
ATTN_HEADS_PER_STEP = 4

Weight = collections.namedtuple("Weight", ["arr", "lead", "col_off", "col_tiled"])
Rows = collections.namedtuple("Rows", ["arr", "first"])
Dest = collections.namedtuple("Dest", ["rows", "first", "alias"])
IN_PLACE = "in place"


def _weight(arr, lead=(), col_off=0, col_tiled=False):
    return Weight(arr, tuple(lead), col_off, col_tiled)


def _wspec(w, block, index):
    lead = w.lead

    def index_map(*ids):
        r, c = index(*ids)
        return lead + (r, c + w.col_off)

    return pl.BlockSpec((None,) * len(lead) + block, index_map)


def _col_spec(w, rows, tn, col):
    if w.col_tiled:
        assert w.arr.shape[1:] == (rows, tn)
        return pl.BlockSpec((None, rows, tn), lambda *ids: (col(*ids), 0, 0))
    return _wspec(w, (rows, tn), lambda *ids: (0, col(*ids)))


def _lead_spec(lead, block):
    lead = tuple(lead)
    return pl.BlockSpec((None,) * len(lead) + block, lambda *ids: lead + (0,) * len(block))


def _is_f32(w):
    return w.arr.dtype == F32


def _bf16_weight(w_ref, w16_ref):
    if w16_ref is None:
        return w_ref[...]
    w16 = w_ref[...].astype(BF16)
    w16_ref[...] = w16
    return w16


def _dest_args(src, dest):
    if dest.alias is None:
        return [], [], {}
    if dest.alias is IN_PLACE:
        assert src.arr.shape[0] == dest.rows and src.first == dest.first
        return [], [], {0: 0}
    return [pl.BlockSpec(memory_space=pl.ANY)], [dest.alias], {0: 0}


def _params(semantics):
    return pltpu.CompilerParams(dimension_semantics=semantics,
                                vmem_limit_bytes=VMEM_LIMIT_BYTES)


def _rms_rows(x, g):
    ms = jnp.mean(x * x, axis=-1, keepdims=True)
    return (x * lax.rsqrt(ms + EPS)) * g


def _ffn_kernel(*refs, emit, aliased, n_split):
    h_ref, g_ref, wg_ref, wu_ref, wd_ref, out_ref, *rest = refs[aliased:]
    split_ref = rest.pop(0) if n_split else None
    wg16_ref, wu16_ref, wd16_ref = rest[:3] if emit else (None, None, None)
    xn_ref = rest[-1]
    f = pl.program_id(1)

    @pl.when(f == 0)
    def _():
        h = h_ref[...]
        xn_ref[...] = _rms_rows(h, g_ref[...]).astype(BF16)
        out_ref[...] = h

    xn = xn_ref[...]
    gate = jnp.dot(xn, _bf16_weight(wg_ref, wg16_ref), preferred_element_type=F32)
    up = jnp.dot(xn, _bf16_weight(wu_ref, wu16_ref), preferred_element_type=F32)
    act = (gate * jax.nn.sigmoid(gate) * (FFN_RES * up)).astype(BF16)
    out_ref[...] += jnp.dot(act, _bf16_weight(wd_ref, wd16_ref), preferred_element_type=F32)

    if n_split:
        @pl.when(f == pl.num_programs(1) - 1)
        def _():
            split_ref[...] = out_ref[out_ref.shape[0] - n_split:, :]


def _ffn(src, n_tiles, dest, norm_g, g_lead, wg, wu, wd, tm, n_split=0):
    d = src.arr.shape[1]
    d_ff = wd.arr.shape[-2]
    emit = _is_f32(wg)
    tf = FFN_CAST_COL_TILE if emit else FFN_COL_TILE
    out_shape = [jax.ShapeDtypeStruct((dest.rows, d), F32)]
    out_specs = [pl.BlockSpec((tm, d), lambda i, f: (dest.first + i, 0))]
    if n_split:
        assert n_tiles == 1
        out_shape += [jax.ShapeDtypeStruct((n_split, d), F32)]
        out_specs += [pl.BlockSpec((n_split, d), lambda i, f: (0, 0))]
    h_mode = {}
    if emit:
        assert n_tiles == 1
        per = FFN_COL_TILE // tf
        tiled = jax.ShapeDtypeStruct((d_ff // FFN_COL_TILE, d, FFN_COL_TILE), BF16)
        out_shape += [tiled, tiled, jax.ShapeDtypeStruct((d_ff, d), BF16)]
        out_specs += [pl.BlockSpec((None, d, tf), lambda i, f: (f // per, 0, f % per)),
                      pl.BlockSpec((None, d, tf), lambda i, f: (f // per, 0, f % per)),
                      pl.BlockSpec((tf, d), lambda i, f: (f, 0))]
        h_mode = dict(pipeline_mode=pl.Buffered(1))
    alias_specs, alias_args, aliases = _dest_args(src, dest)
    res = pl.pallas_call(
        functools.partial(_ffn_kernel, emit=emit, aliased=len(alias_args), n_split=n_split),
        out_shape=out_shape,
        grid=(n_tiles, d_ff // tf),
        in_specs=alias_specs + [
            pl.BlockSpec((tm, d), lambda i, f: (src.first + i, 0), **h_mode),
            _lead_spec(g_lead, (1, d)),
            _col_spec(wg, d, tf, lambda i, f: f),
            _col_spec(wu, d, tf, lambda i, f: f),
            _wspec(wd, (tf, d), lambda i, f: (f, 0)),
        ],
        out_specs=out_specs,
        scratch_shapes=[pltpu.VMEM((tm, d), BF16)],
        input_output_aliases=aliases,
        compiler_params=_params(("parallel", "arbitrary")),
        name="ffn",
    )(*alias_args, src.arr, norm_g, wg.arr, wu.arr, wd.arr)
    n_act = 2 if n_split else 1
    w16 = tuple(_weight(a, col_tiled=a.ndim == 3) for a in res[n_act:])
    return tuple(res[:n_act]) + (w16,)


def _mixer_kernel(*refs, seq_len, tm, emit, aliased, n_parts):
    (h_ref, g_ref, wb_ref, wc_ref, wx_ref, cw_ref, wo_ref, p0_ref, p1_ref,
     out_ref, st_ref, *rest) = refs[aliased:]
    wb16_ref, wc16_ref, wx16_ref, wo16_ref = rest[:4] if emit else (None,) * 4
    xn_ref, carry_ref = rest[-2:]
    i = pl.program_id(0)
    j = pl.program_id(1)

    @pl.when(j == 0)
    def _():
        h = h_ref[...]
        xn_ref[...] = _rms_rows(h, g_ref[...]).astype(BF16)
        out_ref[...] = h

    if seq_len >= tm:
        @pl.when((i % (seq_len // tm)) == 0)
        def _():
            carry_ref[j] = p0_ref[0]

    wb, wc, wx, wo = (_bf16_weight(wb_ref, wb16_ref), _bf16_weight(wc_ref, wc16_ref),
                      _bf16_weight(wx_ref, wx16_ref), _bf16_weight(wo_ref, wo16_ref))
    cw = cw_ref[...]
    tn = cw.shape[1]
    hm = tm // n_parts
    bs, us = [], []
    for part in range(n_parts):
        xn = xn_ref[part * hm:(part + 1) * hm, :]
        bs.append(jnp.dot(xn, wb, preferred_element_type=F32))
        c = jnp.dot(xn, wc, preferred_element_type=F32)
        x = jnp.dot(xn, wx, preferred_element_type=F32)
        us.append(c * x)
    row = lax.broadcasted_iota(jnp.int32, (hm, tn), 0)
    for part in range(n_parts):
        u = us[part]
        if seq_len >= tm:
            prev = carry_ref[j] if part == 0 else us[part - 1][hm - 2:hm, :]
            p0 = prev[0:1, :]
            p1 = prev[1:2, :]
            pos = row
        else:
            p0 = p0_ref[...]
            p1 = p1_ref[...]
            pos = row % seq_len
        um1 = jnp.where(pos == 0, p1, pltpu.roll(u, 1, 0))
        um2 = jnp.where(pos == 0, p0, jnp.where(pos == 1, p1, pltpu.roll(u, 2, 0)))
        conv = cw[0:1, :] * um2 + cw[1:2, :] * um1 + cw[2:3, :] * u
        v = (bs[part] * conv).astype(BF16)
        out_ref[part * hm:(part + 1) * hm, :] += jnp.dot(v, wo, preferred_element_type=F32)
    if seq_len >= tm:
        carry_ref[j] = us[-1][hm - 2:hm, :]
        st_ref[0] = us[-1][hm - 2:hm, :]
    else:
        st_ref[...] = us[0].reshape(tm // seq_len, seq_len, tn)[:, seq_len - 2:seq_len, :]


def _mixer(src, rows, dest, norm_g, g_lead, wb, wc, wx, conv_w, cw_lead, wo, state, seq_len, tm,
           tn, n_parts=1):
    d = src.arr.shape[1]
    n_seq = rows // seq_len
    n_j = d // tn
    n_i = rows // tm
    emit = _is_f32(wb)
    if seq_len >= tm:
        tiles_per_seq = seq_len // tm
        p0, p1 = state, state
        p_spec = pl.BlockSpec((1, CONV_WIDTH - 1, tn), lambda i, j: (i // tiles_per_seq, 0, j))
        st_shape = jax.ShapeDtypeStruct((n_i, CONV_WIDTH - 1, d), F32)
        st_spec = pl.BlockSpec((1, CONV_WIDTH - 1, tn), lambda i, j: (i, 0, j))
    else:
        assert tm == rows and tm % seq_len == 0 and n_parts == 1
        p0 = jnp.repeat(state[:, 0], seq_len, axis=0)
        p1 = jnp.repeat(state[:, 1], seq_len, axis=0)
        p_spec = pl.BlockSpec((tm, tn), lambda i, j: (i, j))
        st_shape = jax.ShapeDtypeStruct((n_seq, CONV_WIDTH - 1, d), F32)
        st_spec = pl.BlockSpec((n_seq, CONV_WIDTH - 1, tn), lambda i, j: (0, 0, j))
    out_shape = [jax.ShapeDtypeStruct((dest.rows, d), F32), st_shape]
    out_specs = [pl.BlockSpec((tm, d), lambda i, j: (dest.first + i, 0)), st_spec]
    if emit:
        assert n_i == 1
        out_shape += [jax.ShapeDtypeStruct((d, d), BF16)] * 4
        out_specs += [pl.BlockSpec((d, tn), lambda i, j: (0, j))] * 3
        out_specs += [pl.BlockSpec((tn, d), lambda i, j: (j, 0))]
    alias_specs, alias_args, aliases = _dest_args(src, dest)
    res = pl.pallas_call(
        functools.partial(_mixer_kernel, seq_len=seq_len, tm=tm, emit=emit,
                          aliased=len(alias_args), n_parts=n_parts),
        out_shape=out_shape,
        grid=(n_i, n_j),
        in_specs=alias_specs + [
            pl.BlockSpec((tm, d), lambda i, j: (src.first + i, 0)),
            _lead_spec(g_lead, (1, d)),
            _wspec(wb, (d, tn), lambda i, j: (0, j)),
            _wspec(wc, (d, tn), lambda i, j: (0, j)),
            _wspec(wx, (d, tn), lambda i, j: (0, j)),
            pl.BlockSpec((None,) * len(cw_lead) + (CONV_WIDTH, tn),
                         lambda i, j: tuple(cw_lead) + (0, j)),
            _wspec(wo, (tn, d), lambda i, j: (j, 0)),
            p_spec,
            p_spec,
        ],
        out_specs=out_specs,
        scratch_shapes=[pltpu.VMEM((tm, d), BF16),
                        pltpu.VMEM((n_j, CONV_WIDTH - 1, tn), F32)],
        input_output_aliases=aliases,
        compiler_params=_params(("arbitrary", "arbitrary")),
        name="conv_mixer",
    )(*alias_args, src.arr, norm_g, wb.arr, wc.arr, wx.arr, conv_w, wo.arr, p0, p1)
    out, st = res[0], res[1]
    if seq_len >= tm:
        st = st[seq_len // tm - 1::seq_len // tm]
    return out, st, tuple(_weight(a) for a in res[2:])


def _kv_kernel(h_ref, g_ref, wk_ref, wv_ref, kg_ref, k32_ref, v32_ref, k16_ref, v16_ref,
               *rest, emit, keep, v_transposed):
    wk16_ref, wv16_ref = rest if emit else (None, None)
    xn = _rms_rows(h_ref[...], g_ref[...]).astype(BF16)
    k = jnp.dot(xn, _bf16_weight(wk_ref, wk16_ref), preferred_element_type=F32)
    v = jnp.dot(xn, _bf16_weight(wv_ref, wv16_ref), preferred_element_type=F32)
    kg = kg_ref[...]
    tm = k.shape[0]
    v32_ref[...] = v[tm - keep:, :]
    for hd in range(k.shape[1] // HEAD_DIM):
        cols = slice(hd * HEAD_DIM, (hd + 1) * HEAD_DIM)
        kh = _rms_rows(k[:, cols], kg)
        k32_ref[:, cols] = kh[tm - keep:, :]
        k16_ref[hd] = kh.astype(BF16)
        if v_transposed:
            v16_ref[hd] = v[:, cols].T.astype(BF16)
        else:
            v16_ref[hd] = v[:, cols].astype(BF16)


def _kv_proj(src, rows, norm_g, wk, wv, k_gain, seq_len, tm, v_transposed):
    d = src.arr.shape[1]
    emit = _is_f32(wk)
    tn = CAST_COL_TILE if emit else KV_COL_TILE
    hpb = tn // HEAD_DIM
    if seq_len >= tm:
        keep = min(PAST_ROWS, seq_len)
        assert keep <= tm and seq_len % tm == 0
        tiles_per_seq = seq_len // tm
        kept_rows = (rows // seq_len) * keep
        kept_spec = pl.BlockSpec((keep, tn), lambda j, i: (i // tiles_per_seq, j))
    else:
        assert tm % seq_len == 0
        keep, kept_rows = tm, rows
        kept_spec = pl.BlockSpec((tm, tn), lambda j, i: (i, j))
    if v_transposed:
        v16_shape = jax.ShapeDtypeStruct((N_HEADS, HEAD_DIM, rows), BF16)
        v16_spec = pl.BlockSpec((hpb, HEAD_DIM, tm), lambda j, i: (j, 0, i))
    else:
        v16_shape = jax.ShapeDtypeStruct((N_HEADS, rows, HEAD_DIM), BF16)
        v16_spec = pl.BlockSpec((hpb, tm, HEAD_DIM), lambda j, i: (j, i, 0))
    out_shape = [jax.ShapeDtypeStruct((kept_rows, d), F32),
                 jax.ShapeDtypeStruct((kept_rows, d), F32),
                 jax.ShapeDtypeStruct((N_HEADS, rows, HEAD_DIM), BF16), v16_shape]
    out_specs = [kept_spec, kept_spec,
                 pl.BlockSpec((hpb, tm, HEAD_DIM), lambda j, i: (j, i, 0)), v16_spec]
    if emit:
        assert rows == tm
        out_shape += [jax.ShapeDtypeStruct((d, d), BF16)] * 2
        out_specs += [pl.BlockSpec((d, tn), lambda j, i: (0, j))] * 2
    res = pl.pallas_call(
        functools.partial(_kv_kernel, emit=emit, keep=keep, v_transposed=v_transposed),
        out_shape=out_shape,
        grid=(d // tn, rows // tm),
        in_specs=[
            pl.BlockSpec((tm, d), lambda j, i: (src.first + i, 0)),
            pl.BlockSpec((1, d), lambda j, i: (0, 0)),
            _wspec(wk, (d, tn), lambda j, i: (0, j)),
            _wspec(wv, (d, tn), lambda j, i: (0, j)),
            pl.BlockSpec((1, HEAD_DIM), lambda j, i: (0, 0)),
        ],
        out_specs=out_specs,
        compiler_params=_params(("arbitrary", "arbitrary")),
        name="kv_proj",
    )(src.arr, norm_g, wk.arr, wv.arr, k_gain)
    return res[:4], tuple(_weight(a) for a in res[4:])


def _q_kernel(h_ref, g_ref, wq_ref, qg_ref, q16_ref, *rest, emit):
    wq16_ref = rest[0] if emit else None
    xn_ref = rest[-1]

    @pl.when(pl.program_id(1) == 0)
    def _():
        xn_ref[...] = _rms_rows(h_ref[...], g_ref[...]).astype(BF16)

    q = jnp.dot(xn_ref[...], _bf16_weight(wq_ref, wq16_ref), preferred_element_type=F32)
    qg = qg_ref[...]
    for hd in range(q.shape[1] // HEAD_DIM):
        cols = slice(hd * HEAD_DIM, (hd + 1) * HEAD_DIM)
        q16_ref[hd] = _rms_rows(q[:, cols], qg).astype(BF16)


def _q_proj(src, rows, norm_g, g_lead, wq, q_gain, qg_lead, tm):
    d = src.arr.shape[1]
    emit = _is_f32(wq)
    tn = CAST_COL_TILE if emit else d
    hpb = tn // HEAD_DIM
    out_shape = [jax.ShapeDtypeStruct((N_HEADS, rows, HEAD_DIM), BF16)]
    out_specs = [pl.BlockSpec((hpb, tm, HEAD_DIM), lambda i, j: (j, i, 0))]
    if emit:
        assert rows == tm
        out_shape += [jax.ShapeDtypeStruct((d, d), BF16)]
        out_specs += [pl.BlockSpec((d, tn), lambda i, j: (0, j))]
    res = pl.pallas_call(
        functools.partial(_q_kernel, emit=emit),
        out_shape=out_shape,
        grid=(rows // tm, d // tn),
        in_specs=[
            pl.BlockSpec((tm, d), lambda i, j: (src.first + i, 0)),
            _lead_spec(g_lead, (1, d)),
            _wspec(wq, (d, tn), lambda i, j: (0, j)),
            _lead_spec(qg_lead, (1, HEAD_DIM)),
        ],
        out_specs=out_specs,
        scratch_shapes=[pltpu.VMEM((tm, d), BF16)],
        compiler_params=_params(("parallel", "arbitrary")),
        name="q_proj",
    )(src.arr, norm_g, wq.arr, q_gain)
    return res[0], tuple(_weight(a) for a in res[1:])


def _o_kernel(*refs, emit, att_transposed, aliased):
    h_ref, att_ref, wo_ref, out_ref, *rest = refs[aliased:]
    wo16_ref = rest[0] if emit else None
    wo = _bf16_weight(wo_ref, wo16_ref)
    if att_transposed:
        att_t = att_ref[...].reshape(D_MODEL, att_ref.shape[2])
        y = lax.dot_general(att_t, wo, (((0,), (0,)), ((), ())), preferred_element_type=F32)
    else:
        att2d_ref = rest[-1]

        @pl.when(pl.program_id(1) == 0)
        def _():
            for hd in range(N_HEADS):
                att2d_ref[:, hd * HEAD_DIM:(hd + 1) * HEAD_DIM] = att_ref[hd]

        y = jnp.dot(att2d_ref[...], wo, preferred_element_type=F32)
    out_ref[...] = h_ref[...] + y


def _o_proj(src, rows, dest, att16, wo, tm, att_transposed):
    d = src.arr.shape[1]
    emit = _is_f32(wo)
    tn = CAST_COL_TILE if emit else d
    out_shape = [jax.ShapeDtypeStruct((dest.rows, d), F32)]
    out_specs = [pl.BlockSpec((tm, tn), lambda i, j: (dest.first + i, j))]
    if emit:
        assert rows == tm
        out_shape += [jax.ShapeDtypeStruct((d, d), BF16)]
        out_specs += [pl.BlockSpec((d, tn), lambda i, j: (0, j))]
    if att_transposed:
        att_spec = pl.BlockSpec((N_HEADS, HEAD_DIM, tm), lambda i, j: (0, 0, i))
        scratch = []
    else:
        att_spec = pl.BlockSpec((N_HEADS, tm, HEAD_DIM), lambda i, j: (0, i, 0))
        scratch = [pltpu.VMEM((tm, d), BF16)]
    alias_specs, alias_args, aliases = _dest_args(src, dest)
    res = pl.pallas_call(
        functools.partial(_o_kernel, emit=emit, att_transposed=att_transposed,
                          aliased=len(alias_args)),
        out_shape=out_shape,
        grid=(rows // tm, d // tn),
        in_specs=alias_specs + [
            pl.BlockSpec((tm, tn), lambda i, j: (src.first + i, j)),
            att_spec,
            _wspec(wo, (d, tn), lambda i, j: (0, j)),
        ],
        out_specs=out_specs,
        scratch_shapes=scratch,
        input_output_aliases=aliases,
        compiler_params=_params(("parallel", "arbitrary")),
        name="o_proj",
    )(*alias_args, src.arr, att16, wo.arr)
    return res[0], tuple(_weight(a) for a in res[1:])


def _build_prompt_bias(near_ref, far_ref, bias_ref):
    assert ATTN_WIN == 3 * ATTN_SUB and PAST_ROWS - MAX_REL == ATTN_WIN // 2
    row = lax.broadcasted_iota(jnp.int32, (ATTN_SUB, ATTN_SUB), 0)
    col = lax.broadcasted_iota(jnp.int32, (ATTN_SUB, ATTN_SUB), 1)
    for h in range(N_HEADS):
        circ = jnp.broadcast_to(near_ref[h], (ATTN_SUB, ATTN_SUB))
        for bit in range(ATTN_SUB.bit_length() - 1):
            circ = jnp.where((row >> bit) & 1 == 1, pltpu.roll(circ, 1 << bit, 1), circ)
        far = jnp.broadcast_to(far_ref[h], (ATTN_SUB, ATTN_SUB))
        for t in range(ATTN_WIN // ATTN_SUB):
            key = row + t * ATTN_SUB
            kc, qc = key // CHUNK, col // CHUNK
            band = (kc >= qc) & (kc <= qc + N_PAST_CHUNKS)
            val = jnp.where(key - col <= ATTN_WIN // 2, far, circ)
            bias_ref[h, t * ATTN_SUB:(t + 1) * ATTN_SUB, :] = jnp.where(band, val, NEG_INF)


def _attn_prompt_kernel(q_ref, kp_ref, kc_ref, vp_ref, vc_ref, near_ref, far_ref, o_ref,
                        bias_ref, s_refs, p_refs):
    i = pl.program_id(1)

    @pl.when((pl.program_id(0) == 0) & (i == 0))
    def _():
        _build_prompt_bias(near_ref, far_ref, bias_ref)
        p_refs[...] = jnp.zeros(p_refs.shape, BF16)

    n_before_start = jnp.where(i == 0, ATTN_Q_TILE, 0)
    groups = CHUNK // SUBLANES
    lane_tiles = ATTN_SUB // LANES
    live_blocks = [(c, lt) for c in range(ATTN_WIN // CHUNK) for lt in range(lane_tiles)
                   if lt * LANES // CHUNK <= c <= ((lt + 1) * LANES - 1) // CHUNK + N_PAST_CHUNKS]

    subs = ATTN_Q_TILE // ATTN_SUB
    n_units = ATTN_HEADS_PER_STEP * subs

    def scores(h0, u):
        h, lo = h0 + u // subs, (u % subs) * ATTN_SUB
        q = q_ref[h, lo:lo + ATTN_SUB, :]
        n_past = ATTN_Q_TILE - lo
        dims = (((1,), (1,)), ((), ()))
        s_refs[u, 0:n_past, :] = lax.dot_general(kp_ref[h, lo:, :], q, dims,
                                                 preferred_element_type=F32)
        s_refs[u, n_past:, :] = lax.dot_general(kc_ref[h, 0:ATTN_WIN - n_past, :], q, dims,
                                                preferred_element_type=F32)

    def softmax_and_values(h0, u):
        h, lo = h0 + u // subs, (u % subs) * ATTN_SUB
        s_ref, p_ref = s_refs.at[u], p_refs.at[u]
        m8 = [jnp.full((SUBLANES, LANES), NEG_INF, F32) for _ in range(lane_tiles)]
        for c, lt in live_blocks:
            rows, cols = slice(c * CHUNK, (c + 1) * CHUNK), slice(lt * LANES, (lt + 1) * LANES)
            before_start = c * CHUNK < n_before_start - lo
            x = s_ref[rows, cols] * (ATTN_SCALE * LOG2_E) + bias_ref[h, rows, cols]
            x = jnp.where(before_start, NEG_INF, x)
            s_ref[rows, cols] = x
            m8[lt] = jnp.maximum(m8[lt], jnp.max(x.reshape(groups, SUBLANES, LANES), axis=0))
        m = [jnp.max(v, axis=0, keepdims=True) for v in m8]
        l8 = [jnp.zeros((SUBLANES, LANES), F32) for _ in range(lane_tiles)]
        for c, lt in live_blocks:
            rows, cols = slice(c * CHUNK, (c + 1) * CHUNK), slice(lt * LANES, (lt + 1) * LANES)
            e = jnp.exp2(s_ref[rows, cols] - m[lt])
            p_ref[rows, cols] = e.astype(BF16)
            l8[lt] = l8[lt] + jnp.sum(e.reshape(groups, SUBLANES, LANES), axis=0)
        inv = 1.0 / jnp.concatenate([jnp.sum(v, axis=0, keepdims=True) for v in l8], axis=1)
        n_past = ATTN_Q_TILE - lo
        o = jnp.dot(vp_ref[h, :, lo:], p_ref[0:n_past, :], preferred_element_type=F32)
        o = o + jnp.dot(vc_ref[h, :, 0:ATTN_WIN - n_past], p_ref[n_past:, :],
                        preferred_element_type=F32)
        o_ref[h, :, lo:lo + ATTN_SUB] = (o * inv).astype(BF16)

    def step(t, carry):
        h0 = t * ATTN_HEADS_PER_STEP
        scores(h0, 0)
        for u in range(n_units):
            if u + 1 < n_units:
                scores(h0, u + 1)
            softmax_and_values(h0, u)
        return carry

    lax.fori_loop(0, N_HEADS // ATTN_HEADS_PER_STEP, step, 0)


def _prompt_bias_rows(rel_bias):
    assert ATTN_SUB == 2 * MAX_REL
    k_mod = (-jnp.arange(ATTN_SUB)) % ATTN_SUB
    idx = 2 * MAX_REL - (k_mod - ATTN_WIN // 2) % ATTN_SUB
    near = rel_bias[:, None, idx].astype(F32) * LOG2_E
    far = jnp.broadcast_to(rel_bias[:, None, 2 * MAX_REL:].astype(F32) * LOG2_E, near.shape)
    return near, far


def _attn_prompt(q16, k16, v16t, rel_bias, n_seq, seq_len):
    tiles = seq_len // ATTN_Q_TILE
    cur = lambda b, i: b * tiles + i
    past = lambda b, i: b * tiles + jnp.maximum(i - 1, 0)
    rows = lambda at: pl.BlockSpec((N_HEADS, ATTN_Q_TILE, HEAD_DIM), lambda b, i: (0, at(b, i), 0))
    cols = lambda at: pl.BlockSpec((N_HEADS, HEAD_DIM, ATTN_Q_TILE), lambda b, i: (0, 0, at(b, i)))
    near, far = _prompt_bias_rows(rel_bias)
    bias_row = pl.BlockSpec(near.shape, lambda b, i: (0, 0, 0))
    n_units = ATTN_HEADS_PER_STEP * (ATTN_Q_TILE // ATTN_SUB)
    return pl.pallas_call(
        _attn_prompt_kernel,
        out_shape=jax.ShapeDtypeStruct(v16t.shape, BF16),
        grid=(n_seq, tiles),
        in_specs=[rows(cur), rows(past), rows(cur), cols(past), cols(cur), bias_row, bias_row],
        out_specs=cols(cur),
        scratch_shapes=[pltpu.VMEM((N_HEADS, ATTN_WIN, ATTN_SUB), F32),
                        pltpu.VMEM((n_units, ATTN_WIN, ATTN_SUB), F32),
                        pltpu.VMEM((n_units, ATTN_WIN, ATTN_SUB), BF16)],
        compiler_params=_params(("arbitrary", "arbitrary")),
        name="attn_prompt",
    )(q16, k16, k16, v16t, v16t, near, far)


def _attn_sample_kernel(q_ref, kn_ref, vn_ref, ck_ref, cv_ref, bc_ref, bn_ref, o_ref):
    cache_len = ck_ref.shape[1] // N_HEADS
    for h in range(N_HEADS):
        q = q_ref[h]
        kc = ck_ref[0, pl.ds(h, cache_len, stride=N_HEADS), :].astype(BF16)
        vc = cv_ref[0, pl.ds(h, cache_len, stride=N_HEADS), :].astype(BF16)
        dims = (((1,), (1,)), ((), ()))
        sc = lax.dot_general(q, kc, dims, preferred_element_type=F32) * ATTN_SCALE + bc_ref[h]
        sn = lax.dot_general(q, kn_ref[h], dims, preferred_element_type=F32) * ATTN_SCALE + bn_ref[h]
        m = jnp.maximum(jnp.max(sc, axis=-1, keepdims=True), jnp.max(sn, axis=-1, keepdims=True))
        ec = jnp.exp(sc - m)
        en = jnp.exp(sn - m)
        inv = 1.0 / (jnp.sum(ec, axis=-1, keepdims=True) + jnp.sum(en, axis=-1, keepdims=True))
        o = jnp.dot((ec * inv).astype(BF16), vc, preferred_element_type=F32)
        o = o + jnp.dot((en * inv).astype(BF16), vn_ref[h], preferred_element_type=F32)
        o_ref[h] = o.astype(BF16)


def _attn_sample(q16, k16, v16, cache_k, cache_v, rel_bias, n_seq, seq_len):
    cache_len = cache_k.shape[1]
    ck = cache_k.reshape(n_seq, cache_len * N_HEADS, HEAD_DIM)
    cv = cache_v.reshape(n_seq, cache_len * N_HEADS, HEAD_DIM)
    n_far = max(cache_len - MAX_REL, 0)
    a = jnp.arange(seq_len)[:, None]
    w = jnp.arange(n_far, cache_len + seq_len)[None, :]
    idx = jnp.clip(a - w + cache_len, -MAX_REL, MAX_REL) + MAX_REL
    near = rel_bias[:, idx].astype(F32)
    far = jnp.broadcast_to(rel_bias[:, None, 2 * MAX_REL:].astype(F32), (N_HEADS, seq_len, n_far))
    bias_c = jnp.concatenate([far, near[:, :, :cache_len - n_far]], axis=2)
    bias_n = near[:, :, cache_len - n_far:]
    new = pl.BlockSpec((N_HEADS, seq_len, HEAD_DIM), lambda b: (0, b, 0))
    cache = pl.BlockSpec((1, cache_len * N_HEADS, HEAD_DIM), lambda b: (b, 0, 0))
    return pl.pallas_call(
        _attn_sample_kernel,
        out_shape=jax.ShapeDtypeStruct(q16.shape, BF16),
        grid=(n_seq,),
        in_specs=[new, new, new, cache, cache,
                  pl.BlockSpec(bias_c.shape, lambda b: (0, 0, 0)),
                  pl.BlockSpec(bias_n.shape, lambda b: (0, 0, 0))],
        out_specs=new,
        compiler_params=_params(("parallel",)),
        name="attn_sample",
    )(q16, k16, v16, ck, cv, bias_c, bias_n)


def kernel(x_prompt, x_sample, state_conv, cache_k, cache_v, ffn_norm, ffn_w_gate, ffn_w_up,
           ffn_w_down, mix_norm, conv_w_in, conv_w, conv_w_out, kv_norm, w_kv, k_gain,
           w_q, q_gain, rel_bias, w_o):
    depth = ffn_norm.shape[0]
    assert depth == 2 and conv_w_in.shape[0] == 1 and w_q.shape[0] == 1
    d = D_MODEL
    n_p, t_p, _ = x_prompt.shape
    n_s, t_s, _ = x_sample.shape
    m_p, m_s = n_p * t_p, n_s * t_s
    m_all = m_p + m_s
    tm_p, tm_f = PROMPT_ROW_TILE, FFN_ROW_TILE
    assert m_all % tm_f == 0 and m_s < tm_f and m_p % m_s == 0 and m_p % tm_p == 0
    ffn_tiles = m_all // tm_f
    s_first = m_p // m_s

    ffn_norm4 = ffn_norm.reshape(depth, 2, 1, d)
    mix_norm3 = mix_norm.reshape(depth, 1, d)
    kv_norm2 = kv_norm.reshape(1, d)
    k_gain2 = k_gain.reshape(1, HEAD_DIM)
    q_gain3 = q_gain.reshape(-1, 1, HEAD_DIM)

    def ffn(tail, main, layer, slot, final=False):
        w32 = [_weight(w, (layer, slot)) for w in (ffn_w_gate, ffn_w_up, ffn_w_down)]
        rows = m_p if final else m_all
        in_place = tail.arr is main.arr and tail.arr.shape[0] == rows
        *ys, w16 = _ffn(tail, 1, Dest(rows, ffn_tiles - 1, IN_PLACE if in_place else None),
                        ffn_norm4, (layer, slot), *w32, tm_f, n_split=m_s if final else 0)
        main = Rows(ys[0], 0) if in_place else main
        h, _ = _ffn(main, ffn_tiles - 1, Dest(rows, 0, IN_PLACE if in_place else ys[0]),
                    ffn_norm4, (layer, slot), *w16, tm_f)
        return (h, ys[1]) if final else h

    def both(h):
        return Rows(h, ffn_tiles - 1), Rows(h, 0)

    xp = x_prompt.reshape(m_p, d)
    x_tail = jnp.concatenate([xp[m_p - (tm_f - m_s):], x_sample.reshape(m_s, d)], axis=0)
    h = ffn(Rows(x_tail, 0), Rows(xp, 0), 0, 0)

    mix_blocks = d // MIX_COL_TILE
    w_mix32 = (_weight(conv_w_in, (0,)), _weight(conv_w_in, (0,), mix_blocks),
               _weight(conv_w_in, (0,), 2 * mix_blocks))
    h, conv_s, w_mix16 = _mixer(Rows(h, s_first), m_s, Dest(m_all, s_first, IN_PLACE), mix_norm3,
                                (0,), *w_mix32, conv_w, (0,), _weight(conv_w_out, (0,)),
                                state_conv[0], t_s, m_s, MIX_COL_TILE)
    conv_zero = jnp.zeros((n_p, CONV_WIDTH - 1, d), F32)
    h, conv_p, _ = _mixer(Rows(h, 0), m_p, Dest(m_all, 0, IN_PLACE), mix_norm3, (0,),
                          *w_mix16[:3], conv_w, (0,), w_mix16[3], conv_zero, t_p,
                          PROMPT_MIX_ROW_TILE, PROMPT_MIX_COL_TILE, PROMPT_MIX_PARTS)

    h = ffn(*both(h), 0, 1)

    proj_blocks = d // CAST_COL_TILE
    (k_s, v_s, k16_s, v16_s), w_kv16 = _kv_proj(
        Rows(h, s_first), m_s, kv_norm2, _weight(w_kv), _weight(w_kv, (), proj_blocks), k_gain2,
        t_s, m_s, v_transposed=False)
    (k_p, v_p, k16_p, v16t_p), _ = _kv_proj(
        Rows(h, 0), m_p, kv_norm2, *w_kv16, k_gain2, t_p, tm_p, v_transposed=True)

    h = ffn(*both(h), 1, 0)

    q16_s, w_q16 = _q_proj(Rows(h, s_first), m_s, mix_norm3, (1,), _weight(w_q, (0,)),
                           q_gain3, (0,), m_s)
    q16_p, _ = _q_proj(Rows(h, 0), m_p, mix_norm3, (1,), *w_q16, q_gain3, (0,), tm_p)
    att_s = _attn_sample(q16_s, k16_s, v16_s, cache_k, cache_v, rel_bias[0], n_s, t_s)
    att_p = _attn_prompt(q16_p, k16_p, v16t_p, rel_bias[0], n_p, t_p)
    h, w_o16 = _o_proj(Rows(h, s_first), m_s, Dest(m_all, s_first, IN_PLACE), att_s,
                       _weight(w_o, (0,)), m_s, att_transposed=False)
    h, _ = _o_proj(Rows(h, 0), m_p, Dest(m_all, 0, IN_PLACE), att_p, *w_o16, tm_p,
                   att_transposed=True)

    y_p, y_s = ffn(*both(h), 1, 1, final=True)

    keep_p, keep_s = min(PAST_ROWS, t_p), min(PAST_ROWS, t_s)
    heads = (N_HEADS, HEAD_DIM)
    return (y_p.reshape(n_p, t_p, d), y_s.reshape(n_s, t_s, d), conv_p[None],
            k_p.reshape(n_p, keep_p, *heads), v_p.reshape(n_p, keep_p, *heads), conv_s[None],
            k_s.reshape(n_s, keep_s, *heads), v_s.reshape(n_s, keep_s, *heads))
```

```python
import collections
import functools

import jax
import jax.numpy as jnp
from jax import lax
from jax.experimental import pallas as pl
from jax.experimental.pallas import tpu as pltpu

F32 = jnp.float32
BF16 = jnp.bfloat16

D_MODEL = 2048
HEAD_DIM = 128
N_HEADS = D_MODEL // HEAD_DIM
CHUNK = 64
N_PAST_CHUNKS = 8
PAST_ROWS = N_PAST_CHUNKS * CHUNK
MAX_REL = 128
CONV_WIDTH = 3
EPS = 1e-6
FFN_RES = 0.5
NEG_INF = -1e30
ATTN_SCALE = HEAD_DIM ** -0.5
LOG2_E = 1.4426950408889634

SUBLANES = 8
LANES = 128
V7X_VMEM_BYTES = 64 * 1024 * 1024
VMEM_LIMIT_BYTES = V7X_VMEM_BYTES - 8 * 1024 * 1024

PROMPT_ROW_TILE = 512
PROMPT_PROJ_PARTS = 2
FFN_ROW_TILE = 1040
FFN_COL_TILE = 512
FFN_CAST_COL_TILE = 256
MIX_COL_TILE = 512
PROMPT_MIX_ROW_TILE = 1024
PROMPT_MIX_PARTS = 2
PROMPT_MIX_COL_TILE = 256
KV_COL_TILE = 1024
CAST_COL_TILE = 512
ATTN_Q_TILE = 512
ATTN_SUB = 256
ATTN_WIN = ATTN_SUB + PAST_ROWS
ATTN_HEADS_PER_STEP = 16

Weight = collections.namedtuple("Weight", ["arr", "lead", "col_off", "col_tiled"])
Rows = collections.namedtuple("Rows", ["arr", "first"])
Dest = collections.namedtuple("Dest", ["rows", "first", "alias"])
IN_PLACE = "in place"


def _weight(arr, lead=(), col_off=0, col_tiled=False):
    return Weight(arr, tuple(lead), col_off, col_tiled)


def _wspec(w, block, index):
    lead = w.lead

    def index_map(*ids):
        r, c = index(*ids)
        return lead + (r, c + w.col_off)

    return pl.BlockSpec((None,) * len(lead) + block, index_map)


def _col_spec(w, rows, tn, col):
    if w.col_tiled:
        assert w.arr.shape[1:] == (rows, tn)
        return pl.BlockSpec((None, rows, tn), lambda *ids: (col(*ids), 0, 0))
    return _wspec(w, (rows, tn), lambda *ids: (0, col(*ids)))


def _lead_spec(lead, block):
    lead = tuple(lead)
    return pl.BlockSpec((None,) * len(lead) + block, lambda *ids: lead + (0,) * len(block))


def _is_f32(w):
    return w.arr.dtype == F32


def _bf16_weight(w_ref, w16_ref):
    if w16_ref is None:
        return w_ref[...]
    w16 = w_ref[...].astype(BF16)
    w16_ref[...] = w16
    return w16


def _dest_args(src, dest):
    if dest.alias is None:
        return [], [], {}
    if dest.alias is IN_PLACE:
        assert src.arr.shape[0] == dest.rows and src.first == dest.first
        return [], [], {0: 0}
    return [pl.BlockSpec(memory_space=pl.ANY)], [dest.alias], {0: 0}


def _params(semantics):
    return pltpu.CompilerParams(dimension_semantics=semantics,
                                vmem_limit_bytes=VMEM_LIMIT_BYTES)


def _rms_rows(x, g):
    ms = jnp.mean(x * x, axis=-1, keepdims=True)
    return (x * lax.rsqrt(ms + EPS)) * g


def _ffn_kernel(*refs, emit, aliased, n_split):
    h_ref, g_ref, wg_ref, wu_ref, wd_ref, out_ref, *rest = refs[aliased:]
    split_ref = rest.pop(0) if n_split else None
    wg16_ref, wu16_ref, wd16_ref = rest[:3] if emit else (None, None, None)
    xn_ref = rest[-1]
    f = pl.program_id(1)

    @pl.when(f == 0)
    def _():
        h = h_ref[...]
        xn_ref[...] = _rms_rows(h, g_ref[...]).astype(BF16)
        out_ref[...] = h

    xn = xn_ref[...]
    gate = jnp.dot(xn, _bf16_weight(wg_ref, wg16_ref), preferred_element_type=F32)
    up = jnp.dot(xn, _bf16_weight(wu_ref, wu16_ref), preferred_element_type=F32)
    act = (gate * jax.nn.sigmoid(gate) * (FFN_RES * up)).astype(BF16)
    out_ref[...] += jnp.dot(act, _bf16_weight(wd_ref, wd16_ref), preferred_element_type=F32)

    if n_split:
        @pl.when(f == pl.num_programs(1) - 1)
        def _():
            split_ref[...] = out_ref[out_ref.shape[0] - n_split:, :]


def _ffn(src, n_tiles, dest, norm_g, g_lead, wg, wu, wd, tm, n_split=0):
    d = src.arr.shape[1]
    d_ff = wd.arr.shape[-2]
    emit = _is_f32(wg)
    tf = FFN_CAST_COL_TILE if emit else FFN_COL_TILE
    out_shape = [jax.ShapeDtypeStruct((dest.rows, d), F32)]
    out_specs = [pl.BlockSpec((tm, d), lambda i, f: (dest.first + i, 0))]
    if n_split:
        assert n_tiles == 1
        out_shape += [jax.ShapeDtypeStruct((n_split, d), F32)]
        out_specs += [pl.BlockSpec((n_split, d), lambda i, f: (0, 0))]
    h_mode = {}
    if emit:
        assert n_tiles == 1
        per = FFN_COL_TILE // tf
        tiled = jax.ShapeDtypeStruct((d_ff // FFN_COL_TILE, d, FFN_COL_TILE), BF16)
        out_shape += [tiled, tiled, jax.ShapeDtypeStruct((d_ff, d), BF16)]
        out_specs += [pl.BlockSpec((None, d, tf), lambda i, f: (f // per, 0, f % per)),
                      pl.BlockSpec((None, d, tf), lambda i, f: (f // per, 0, f % per)),
                      pl.BlockSpec((tf, d), lambda i, f: (f, 0))]
        h_mode = dict(pipeline_mode=pl.Buffered(1))
    alias_specs, alias_args, aliases = _dest_args(src, dest)
    res = pl.pallas_call(
        functools.partial(_ffn_kernel, emit=emit, aliased=len(alias_args), n_split=n_split),
        out_shape=out_shape,
        grid=(n_tiles, d_ff // tf),
        in_specs=alias_specs + [
            pl.BlockSpec((tm, d), lambda i, f: (src.first + i, 0), **h_mode),
            _lead_spec(g_lead, (1, d)),
            _col_spec(wg, d, tf, lambda i, f: f),
            _col_spec(wu, d, tf, lambda i, f: f),
            _wspec(wd, (tf, d), lambda i, f: (f, 0)),
        ],
        out_specs=out_specs,
        scratch_shapes=[pltpu.VMEM((tm, d), BF16)],
        input_output_aliases=aliases,
        compiler_params=_params(("parallel", "arbitrary")),
        name="ffn",
    )(*alias_args, src.arr, norm_g, wg.arr, wu.arr, wd.arr)
    n_act = 2 if n_split else 1
    w16 = tuple(_weight(a, col_tiled=a.ndim == 3) for a in res[n_act:])
    return tuple(res[:n_act]) + (w16,)


def _mixer_kernel(*refs, seq_len, tm, emit, aliased, n_parts):
    (h_ref, g_ref, wb_ref, wc_ref, wx_ref, cw_ref, wo_ref, p0_ref, p1_ref,
     out_ref, st_ref, *rest) = refs[aliased:]
    wb16_ref, wc16_ref, wx16_ref, wo16_ref = rest[:4] if emit else (None,) * 4
    xn_ref, carry_ref = rest[-2:]
    i = pl.program_id(0)
    j = pl.program_id(1)

    @pl.when(j == 0)
    def _():
        h = h_ref[...]
        xn_ref[...] = _rms_rows(h, g_ref[...]).astype(BF16)
        out_ref[...] = h

    if seq_len >= tm:
        @pl.when((i % (seq_len // tm)) == 0)
        def _():
            carry_ref[j] = p0_ref[0]

    wb, wc, wx, wo = (_bf16_weight(wb_ref, wb16_ref), _bf16_weight(wc_ref, wc16_ref),
                      _bf16_weight(wx_ref, wx16_ref), _bf16_weight(wo_ref, wo16_ref))
    cw = cw_ref[...]
    tn = cw.shape[1]
    hm = tm // n_parts
    bs, us = [], []
    for part in range(n_parts):
        xn = xn_ref[part * hm:(part + 1) * hm, :]
        bs.append(jnp.dot(xn, wb, preferred_element_type=F32))
        c = jnp.dot(xn, wc, preferred_element_type=F32)
        x = jnp.dot(xn, wx, preferred_element_type=F32)
        us.append(c * x)
    row = lax.broadcasted_iota(jnp.int32, (hm, tn), 0)
    for part in range(n_parts):
        u = us[part]
        if seq_len >= tm:
            prev = carry_ref[j] if part == 0 else us[part - 1][hm - 2:hm, :]
            p0 = prev[0:1, :]
            p1 = prev[1:2, :]
            pos = row
        else:
            p0 = p0_ref[...]
            p1 = p1_ref[...]
            pos = row % seq_len
        um1 = jnp.where(pos == 0, p1, pltpu.roll(u, 1, 0))
        um2 = jnp.where(pos == 0, p0, jnp.where(pos == 1, p1, pltpu.roll(u, 2, 0)))
        conv = cw[0:1, :] * um2 + cw[1:2, :] * um1 + cw[2:3, :] * u
        v = (bs[part] * conv).astype(BF16)
        out_ref[part * hm:(part + 1) * hm, :] += jnp.dot(v, wo, preferred_element_type=F32)
    if seq_len >= tm:
        carry_ref[j] = us[-1][hm - 2:hm, :]
        st_ref[0] = us[-1][hm - 2:hm, :]
    else:
        st_ref[...] = us[0].reshape(tm // seq_len, seq_len, tn)[:, seq_len - 2:seq_len, :]


def _mixer(src, rows, dest, norm_g, g_lead, wb, wc, wx, conv_w, cw_lead, wo, state, seq_len, tm,
           tn, n_parts=1):
    d = src.arr.shape[1]
    n_seq = rows // seq_len
    n_j = d // tn
    n_i = rows // tm
    emit = _is_f32(wb)
    if seq_len >= tm:
        tiles_per_seq = seq_len // tm
        p0, p1 = state, state
        p_spec = pl.BlockSpec((1, CONV_WIDTH - 1, tn), lambda i, j: (i // tiles_per_seq, 0, j))
        st_shape = jax.ShapeDtypeStruct((n_i, CONV_WIDTH - 1, d), F32)
        st_spec = pl.BlockSpec((1, CONV_WIDTH - 1, tn), lambda i, j: (i, 0, j))
    else:
        assert tm == rows and tm % seq_len == 0 and n_parts == 1
        p0 = jnp.repeat(state[:, 0], seq_len, axis=0)
        p1 = jnp.repeat(state[:, 1], seq_len, axis=0)
        p_spec = pl.BlockSpec((tm, tn), lambda i, j: (i, j))
        st_shape = jax.ShapeDtypeStruct((n_seq, CONV_WIDTH - 1, d), F32)
        st_spec = pl.BlockSpec((n_seq, CONV_WIDTH - 1, tn), lambda i, j: (0, 0, j))
    out_shape = [jax.ShapeDtypeStruct((dest.rows, d), F32), st_shape]
    out_specs = [pl.BlockSpec((tm, d), lambda i, j: (dest.first + i, 0)), st_spec]
    if emit:
        assert n_i == 1
        out_shape += [jax.ShapeDtypeStruct((d, d), BF16)] * 4
        out_specs += [pl.BlockSpec((d, tn), lambda i, j: (0, j))] * 3
        out_specs += [pl.BlockSpec((tn, d), lambda i, j: (j, 0))]
    alias_specs, alias_args, aliases = _dest_args(src, dest)
    res = pl.pallas_call(
        functools.partial(_mixer_kernel, seq_len=seq_len, tm=tm, emit=emit,
                          aliased=len(alias_args), n_parts=n_parts),
        out_shape=out_shape,
        grid=(n_i, n_j),
        in_specs=alias_specs + [
            pl.BlockSpec((tm, d), lambda i, j: (src.first + i, 0)),
            _lead_spec(g_lead, (1, d)),
            _wspec(wb, (d, tn), lambda i, j: (0, j)),
            _wspec(wc, (d, tn), lambda i, j: (0, j)),
            _wspec(wx, (d, tn), lambda i, j: (0, j)),
            pl.BlockSpec((None,) * len(cw_lead) + (CONV_WIDTH, tn),
                         lambda i, j: tuple(cw_lead) + (0, j)),
            _wspec(wo, (tn, d), lambda i, j: (j, 0)),
            p_spec,
            p_spec,
        ],
        out_specs=out_specs,
        scratch_shapes=[pltpu.VMEM((tm, d), BF16),
                        pltpu.VMEM((n_j, CONV_WIDTH - 1, tn), F32)],
        input_output_aliases=aliases,
        compiler_params=_params(("arbitrary", "arbitrary")),
        name="conv_mixer",
    )(*alias_args, src.arr, norm_g, wb.arr, wc.arr, wx.arr, conv_w, wo.arr, p0, p1)
    out, st = res[0], res[1]
    if seq_len >= tm:
        st = st[seq_len // tm - 1::seq_len // tm]
    return out, st, tuple(_weight(a) for a in res[2:])


def _kv_kernel(h_ref, g_ref, wk_ref, wv_ref, kg_ref, k32_ref, v32_ref, k16_ref, v16_ref,
               *rest, emit, n_parts, v_transposed):
    wk16_ref, wv16_ref = rest if emit else (None, None)
    wk, wv = _bf16_weight(wk_ref, wk16_ref), _bf16_weight(wv_ref, wv16_ref)
    g, kg = g_ref[...], kg_ref[...]
    hm = h_ref.shape[0] // n_parts
    ks, vs = [], []
    for part in range(n_parts):
        xn = _rms_rows(h_ref[part * hm:(part + 1) * hm, :], g).astype(BF16)
        ks.append(jnp.dot(xn, wk, preferred_element_type=F32))
        vs.append(jnp.dot(xn, wv, preferred_element_type=F32))
    for part, (k, v) in enumerate(zip(ks, vs)):
        rows = slice(part * hm, (part + 1) * hm)
        v32_ref[rows, :] = v
        for hd in range(k.shape[1] // HEAD_DIM):
            cols = slice(hd * HEAD_DIM, (hd + 1) * HEAD_DIM)
            kh = _rms_rows(k[:, cols], kg)
            k32_ref[rows, cols] = kh
            k16_ref[hd, rows, :] = kh.astype(BF16)
            if v_transposed:
                v16_ref[hd, :, rows] = v[:, cols].T.astype(BF16)
            else:
                v16_ref[hd, rows, :] = v[:, cols].astype(BF16)


def _kv_proj(src, rows, norm_g, wk, wv, k_gain, seq_len, tm, v_transposed, n_parts=1):
    d = src.arr.shape[1]
    emit = _is_f32(wk)
    tn = CAST_COL_TILE if emit else KV_COL_TILE
    hpb = tn // HEAD_DIM
    if seq_len >= tm:
        assert tm == min(PAST_ROWS, seq_len) and seq_len % tm == 0
        tiles_per_seq = seq_len // tm
        kept_rows = (rows // seq_len) * tm
        kept_spec = pl.BlockSpec((tm, tn), lambda j, i: (i // tiles_per_seq, j))
    else:
        assert tm % seq_len == 0 and seq_len <= PAST_ROWS
        kept_rows = rows
        kept_spec = pl.BlockSpec((tm, tn), lambda j, i: (i, j))
    if v_transposed:
        v16_shape = jax.ShapeDtypeStruct((N_HEADS, HEAD_DIM, rows), BF16)
        v16_spec = pl.BlockSpec((hpb, HEAD_DIM, tm), lambda j, i: (j, 0, i))
    else:
        v16_shape = jax.ShapeDtypeStruct((N_HEADS, rows, HEAD_DIM), BF16)
        v16_spec = pl.BlockSpec((hpb, tm, HEAD_DIM), lambda j, i: (j, i, 0))
    out_shape = [jax.ShapeDtypeStruct((kept_rows, d), F32),
                 jax.ShapeDtypeStruct((kept_rows, d), F32),
                 jax.ShapeDtypeStruct((N_HEADS, rows, HEAD_DIM), BF16), v16_shape]
    out_specs = [kept_spec, kept_spec,
                 pl.BlockSpec((hpb, tm, HEAD_DIM), lambda j, i: (j, i, 0)), v16_spec]
    if emit:
        assert rows == tm
        out_shape += [jax.ShapeDtypeStruct((d, d), BF16)] * 2
        out_specs += [pl.BlockSpec((d, tn), lambda j, i: (0, j))] * 2
    res = pl.pallas_call(
        functools.partial(_kv_kernel, emit=emit, n_parts=n_parts, v_transposed=v_transposed),
        out_shape=out_shape,
        grid=(d // tn, rows // tm),
        in_specs=[
            pl.BlockSpec((tm, d), lambda j, i: (src.first + i, 0)),
            pl.BlockSpec((1, d), lambda j, i: (0, 0)),
            _wspec(wk, (d, tn), lambda j, i: (0, j)),
            _wspec(wv, (d, tn), lambda j, i: (0, j)),
            pl.BlockSpec((1, HEAD_DIM), lambda j, i: (0, 0)),
        ],
        out_specs=out_specs,
        compiler_params=_params(("arbitrary", "arbitrary")),
        name="kv_proj",
    )(src.arr, norm_g, wk.arr, wv.arr, k_gain)
    return res[:4], tuple(_weight(a) for a in res[4:])


def _q_kernel(h_ref, g_ref, wq_ref, qg_ref, q16_ref, *rest, emit):
    wq16_ref = rest[0] if emit else None
    xn_ref = rest[-1]

    @pl.when(pl.program_id(1) == 0)
    def _():
        xn_ref[...] = _rms_rows(h_ref[...], g_ref[...]).astype(BF16)

    q = jnp.dot(xn_ref[...], _bf16_weight(wq_ref, wq16_ref), preferred_element_type=F32)
    qg = qg_ref[...]
    for hd in range(q.shape[1] // HEAD_DIM):
        cols = slice(hd * HEAD_DIM, (hd + 1) * HEAD_DIM)
        q16_ref[hd] = _rms_rows(q[:, cols], qg).astype(BF16)


def _q_proj(src, rows, norm_g, g_lead, wq, q_gain, qg_lead, tm):
    d = src.arr.shape[1]
    emit = _is_f32(wq)
    tn = CAST_COL_TILE if emit else d
    hpb = tn // HEAD_DIM
    out_shape = [jax.ShapeDtypeStruct((N_HEADS, rows, HEAD_DIM), BF16)]
    out_specs = [pl.BlockSpec((hpb, tm, HEAD_DIM), lambda i, j: (j, i, 0))]
    if emit:
        assert rows == tm
        out_shape += [jax.ShapeDtypeStruct((d, d), BF16)]
        out_specs += [pl.BlockSpec((d, tn), lambda i, j: (0, j))]
    res = pl.pallas_call(
        functools.partial(_q_kernel, emit=emit),
        out_shape=out_shape,
        grid=(rows // tm, d // tn),
        in_specs=[
            pl.BlockSpec((tm, d), lambda i, j: (src.first + i, 0)),
            _lead_spec(g_lead, (1, d)),
            _wspec(wq, (d, tn), lambda i, j: (0, j)),
            _lead_spec(qg_lead, (1, HEAD_DIM)),
        ],
        out_specs=out_specs,
        scratch_shapes=[pltpu.VMEM((tm, d), BF16)],
        compiler_params=_params(("parallel", "arbitrary")),
        name="q_proj",
    )(src.arr, norm_g, wq.arr, q_gain)
    return res[0], tuple(_weight(a) for a in res[1:])


def _o_kernel(*refs, emit, att_transposed, aliased):
    h_ref, att_ref, wo_ref, out_ref, *rest = refs[aliased:]
    wo16_ref = rest[0] if emit else None
    wo = _bf16_weight(wo_ref, wo16_ref)
    if att_transposed:
        att_t = att_ref[...].reshape(D_MODEL, att_ref.shape[2])
        y = lax.dot_general(att_t, wo, (((0,), (0,)), ((), ())), preferred_element_type=F32)
    else:
        att2d_ref = rest[-1]

        @pl.when(pl.program_id(1) == 0)
        def _():
            for hd in range(N_HEADS):
                att2d_ref[:, hd * HEAD_DIM:(hd + 1) * HEAD_DIM] = att_ref[hd]

        y = jnp.dot(att2d_ref[...], wo, preferred_element_type=F32)
    out_ref[...] = h_ref[...] + y


def _o_proj(src, rows, dest, att16, wo, tm, att_transposed):
    d = src.arr.shape[1]
    emit = _is_f32(wo)
    tn = CAST_COL_TILE if emit else d
    out_shape = [jax.ShapeDtypeStruct((dest.rows, d), F32)]
    out_specs = [pl.BlockSpec((tm, tn), lambda i, j: (dest.first + i, j))]
    if emit:
        assert rows == tm
        out_shape += [jax.ShapeDtypeStruct((d, d), BF16)]
        out_specs += [pl.BlockSpec((d, tn), lambda i, j: (0, j))]
    if att_transposed:
        att_spec = pl.BlockSpec((N_HEADS, HEAD_DIM, tm), lambda i, j: (0, 0, i))
        scratch = []
    else:
        att_spec = pl.BlockSpec((N_HEADS, tm, HEAD_DIM), lambda i, j: (0, i, 0))
        scratch = [pltpu.VMEM((tm, d), BF16)]
    alias_specs, alias_args, aliases = _dest_args(src, dest)
    res = pl.pallas_call(
        functools.partial(_o_kernel, emit=emit, att_transposed=att_transposed,
                          aliased=len(alias_args)),
        out_shape=out_shape,
        grid=(rows // tm, d // tn),
        in_specs=alias_specs + [
            pl.BlockSpec((tm, tn), lambda i, j: (src.first + i, j)),
            att_spec,
            _wspec(wo, (d, tn), lambda i, j: (0, j)),
        ],
        out_specs=out_specs,
        scratch_shapes=scratch,
        input_output_aliases=aliases,
        compiler_params=_params(("parallel", "arbitrary")),
        name="o_proj",
    )(*alias_args, src.arr, att16, wo.arr)
    return res[0], tuple(_weight(a) for a in res[1:])


def _build_prompt_bias(near_ref, far_ref, bias_ref):
    assert ATTN_WIN == 3 * ATTN_SUB and PAST_ROWS - MAX_REL == ATTN_WIN // 2
    row = lax.broadcasted_iota(jnp.int32, (ATTN_SUB, ATTN_SUB), 0)
    col = lax.broadcasted_iota(jnp.int32, (ATTN_SUB, ATTN_SUB), 1)
    for h in range(N_HEADS):
        circ = jnp.broadcast_to(near_ref[h], (ATTN_SUB, ATTN_SUB))
        for bit in range(ATTN_SUB.bit_length() - 1):
            circ = jnp.where((row >> bit) & 1 == 1, pltpu.roll(circ, 1 << bit, 1), circ)
        far = jnp.broadcast_to(far_ref[h], (ATTN_SUB, ATTN_SUB))
        for t in range(ATTN_WIN // ATTN_SUB):
            key = row + t * ATTN_SUB
            kc, qc = key // CHUNK, col // CHUNK
            band = (kc >= qc) & (kc <= qc + N_PAST_CHUNKS)
            val = jnp.where(key - col <= ATTN_WIN // 2, far, circ)
            bias_ref[h, t * ATTN_SUB:(t + 1) * ATTN_SUB, :] = jnp.where(band, val, NEG_INF)


def _attn_prompt_kernel(q_ref, kp_ref, kc_ref, vp_ref, vc_ref, near_ref, far_ref, o_ref,
                        bias_ref, s_refs, p_refs):
    i = pl.program_id(1)

    @pl.when((pl.program_id(0) == 0) & (i == 0))
    def _():
        _build_prompt_bias(near_ref, far_ref, bias_ref)
        p_refs[...] = jnp.zeros(p_refs.shape, BF16)

    n_before_start = jnp.where(i == 0, ATTN_Q_TILE, 0)
    groups = CHUNK // SUBLANES
    lane_tiles = ATTN_SUB // LANES
    live_blocks = [(c, lt) for c in range(ATTN_WIN // CHUNK) for lt in range(lane_tiles)
                   if lt * LANES // CHUNK <= c <= ((lt + 1) * LANES - 1) // CHUNK + N_PAST_CHUNKS]

    subs = ATTN_Q_TILE // ATTN_SUB
    n_units = ATTN_HEADS_PER_STEP * subs

    def scores(h0, u):
        h, lo = h0 + u // subs, (u % subs) * ATTN_SUB
        q = q_ref[h, lo:lo + ATTN_SUB, :]
        n_past = ATTN_Q_TILE - lo
        dims = (((1,), (1,)), ((), ()))
        s_refs[u, 0:n_past, :] = lax.dot_general(kp_ref[h, lo:, :], q, dims,
                                                 preferred_element_type=F32)
        s_refs[u, n_past:, :] = lax.dot_general(kc_ref[h, 0:ATTN_WIN - n_past, :], q, dims,
                                                preferred_element_type=F32)

    def block(c, lt):
        return slice(c * CHUNK, (c + 1) * CHUNK), slice(lt * LANES, (lt + 1) * LANES)

    def biased_max(h0, u):
        h, lo = h0 + u // subs, (u % subs) * ATTN_SUB
        s_ref = s_refs.at[u]
        m8 = [jnp.full((SUBLANES, LANES), NEG_INF, F32) for _ in range(lane_tiles)]
        for c, lt in live_blocks:
            rows, cols = block(c, lt)
            before_start = c * CHUNK < n_before_start - lo
            x = s_ref[rows, cols] * (ATTN_SCALE * LOG2_E) + bias_ref[h, rows, cols]
            x = jnp.where(before_start, NEG_INF, x)
            s_ref[rows, cols] = x
            m8[lt] = jnp.maximum(m8[lt], jnp.max(x.reshape(groups, SUBLANES, LANES), axis=0))
        return [jnp.max(v, axis=0, keepdims=True) for v in m8]

    def exponentials(u, m):
        s_ref, p_ref = s_refs.at[u], p_refs.at[u]
        l8 = [jnp.zeros((SUBLANES, LANES), F32) for _ in range(lane_tiles)]
        for c, lt in live_blocks:
            rows, cols = block(c, lt)
            e = jnp.exp2(s_ref[rows, cols] - m[lt])
            p_ref[rows, cols] = e.astype(BF16)
            l8[lt] = l8[lt] + jnp.sum(e.reshape(groups, SUBLANES, LANES), axis=0)
        return 1.0 / jnp.concatenate([jnp.sum(v, axis=0, keepdims=True) for v in l8], axis=1)

    def values(h0, u, inv):
        h, lo = h0 + u // subs, (u % subs) * ATTN_SUB
        p_ref = p_refs.at[u]
        n_past = ATTN_Q_TILE - lo
        o = jnp.dot(vp_ref[h, :, lo:], p_ref[0:n_past, :], preferred_element_type=F32)
        o = o + jnp.dot(vc_ref[h, :, 0:ATTN_WIN - n_past], p_ref[n_past:, :],
                        preferred_element_type=F32)
        o_ref[h, :, lo:lo + ATTN_SUB] = (o * inv).astype(BF16)

    def step(t, carry):
        h0 = t * ATTN_HEADS_PER_STEP
        m, inv = {}, {}
        for r in range(n_units + 3):
            if 0 <= r - 3:
                values(h0, r - 3, inv.pop(r - 3))
            if 0 <= r - 2 < n_units:
                inv[r - 2] = exponentials(r - 2, m.pop(r - 2))
            if 0 <= r - 1 < n_units:
                m[r - 1] = biased_max(h0, r - 1)
            if r < n_units:
                scores(h0, r)
        return carry

    lax.fori_loop(0, N_HEADS // ATTN_HEADS_PER_STEP, step, 0)


def _prompt_bias_rows(rel_bias):
    assert ATTN_SUB == 2 * MAX_REL
    k_mod = (-jnp.arange(ATTN_SUB)) % ATTN_SUB
    idx = 2 * MAX_REL - (k_mod - ATTN_WIN // 2) % ATTN_SUB
    near = rel_bias[:, None, idx].astype(F32) * LOG2_E
    far = jnp.broadcast_to(rel_bias[:, None, 2 * MAX_REL:].astype(F32) * LOG2_E, near.shape)
    return near, far


def _attn_prompt(q16, k16, v16t, rel_bias, n_seq, seq_len):
    tiles = seq_len // ATTN_Q_TILE
    cur = lambda b, i: b * tiles + i
    past = lambda b, i: b * tiles + jnp.maximum(i - 1, 0)
    rows = lambda at: pl.BlockSpec((N_HEADS, ATTN_Q_TILE, HEAD_DIM), lambda b, i: (0, at(b, i), 0))
    cols = lambda at: pl.BlockSpec((N_HEADS, HEAD_DIM, ATTN_Q_TILE), lambda b, i: (0, 0, at(b, i)))
    near, far = _prompt_bias_rows(rel_bias)
    bias_row = pl.BlockSpec(near.shape, lambda b, i: (0, 0, 0))
    n_units = ATTN_HEADS_PER_STEP * (ATTN_Q_TILE // ATTN_SUB)
    return pl.pallas_call(
        _attn_prompt_kernel,
        out_shape=jax.ShapeDtypeStruct(v16t.shape, BF16),
        grid=(n_seq, tiles),
        in_specs=[rows(cur), rows(past), rows(cur), cols(past), cols(cur), bias_row, bias_row],
        out_specs=cols(cur),
        scratch_shapes=[pltpu.VMEM((N_HEADS, ATTN_WIN, ATTN_SUB), F32),
                        pltpu.VMEM((n_units, ATTN_WIN, ATTN_SUB), F32),
                        pltpu.VMEM((n_units, ATTN_WIN, ATTN_SUB), BF16)],
        compiler_params=_params(("arbitrary", "arbitrary")),
        name="attn_prompt",
    )(q16, k16, k16, v16t, v16t, near, far)


def _attn_sample_kernel(q_ref, kn_ref, vn_ref, ck_ref, cv_ref, bc_ref, bn_ref, o_ref):
    cache_len = ck_ref.shape[1] // N_HEADS
    for h in range(N_HEADS):
        q = q_ref[h]
        kc = ck_ref[0, pl.ds(h, cache_len, stride=N_HEADS), :].astype(BF16)
        vc = cv_ref[0, pl.ds(h, cache_len, stride=N_HEADS), :].astype(BF16)
        dims = (((1,), (1,)), ((), ()))
        sc = lax.dot_general(q, kc, dims, preferred_element_type=F32) * ATTN_SCALE + bc_ref[h]
        sn = lax.dot_general(q, kn_ref[h], dims, preferred_element_type=F32) * ATTN_SCALE + bn_ref[h]
        m = jnp.maximum(jnp.max(sc, axis=-1, keepdims=True), jnp.max(sn, axis=-1, keepdims=True))
        ec = jnp.exp(sc - m)
        en = jnp.exp(sn - m)
        inv = 1.0 / (jnp.sum(ec, axis=-1, keepdims=True) + jnp.sum(en, axis=-1, keepdims=True))
        o = jnp.dot((ec * inv).astype(BF16), vc, preferred_element_type=F32)
        o = o + jnp.dot((en * inv).astype(BF16), vn_ref[h], preferred_element_type=F32)
        o_ref[h] = o.astype(BF16)


def _attn_sample(q16, k16, v16, cache_k, cache_v, rel_bias, n_seq, seq_len):
    cache_len = cache_k.shape[1]
    ck = cache_k.reshape(n_seq, cache_len * N_HEADS, HEAD_DIM)
    cv = cache_v.reshape(n_seq, cache_len * N_HEADS, HEAD_DIM)
    n_far = max(cache_len - MAX_REL, 0)
    a = jnp.arange(seq_len)[:, None]
    w = jnp.arange(n_far, cache_len + seq_len)[None, :]
    idx = jnp.clip(a - w + cache_len, -MAX_REL, MAX_REL) + MAX_REL
    near = rel_bias[:, idx].astype(F32)
    far = jnp.broadcast_to(rel_bias[:, None, 2 * MAX_REL:].astype(F32), (N_HEADS, seq_len, n_far))
    bias_c = jnp.concatenate([far, near[:, :, :cache_len - n_far]], axis=2)
    bias_n = near[:, :, cache_len - n_far:]
    new = pl.BlockSpec((N_HEADS, seq_len, HEAD_DIM), lambda b: (0, b, 0))
    cache = pl.BlockSpec((1, cache_len * N_HEADS, HEAD_DIM), lambda b: (b, 0, 0))
    return pl.pallas_call(
        _attn_sample_kernel,
        out_shape=jax.ShapeDtypeStruct(q16.shape, BF16),
        grid=(n_seq,),
        in_specs=[new, new, new, cache, cache,
                  pl.BlockSpec(bias_c.shape, lambda b: (0, 0, 0)),
                  pl.BlockSpec(bias_n.shape, lambda b: (0, 0, 0))],
        out_specs=new,
        compiler_params=_params(("parallel",)),
        name="attn_sample",
    )(q16, k16, v16, ck, cv, bias_c, bias_n)


def kernel(x_prompt, x_sample, state_conv, cache_k, cache_v, ffn_norm, ffn_w_gate, ffn_w_up,
           ffn_w_down, mix_norm, conv_w_in, conv_w, conv_w_out, kv_norm, w_kv, k_gain,
           w_q, q_gain, rel_bias, w_o):
    depth = ffn_norm.shape[0]
    assert depth == 2 and conv_w_in.shape[0] == 1 and w_q.shape[0] == 1
    d = D_MODEL
    n_p, t_p, _ = x_prompt.shape
    n_s, t_s, _ = x_sample.shape
    m_p, m_s = n_p * t_p, n_s * t_s
    m_all = m_p + m_s
    tm_p, tm_f = PROMPT_ROW_TILE, FFN_ROW_TILE
    assert m_all % tm_f == 0 and m_s < tm_f and m_p % m_s == 0 and m_p % tm_p == 0
    ffn_tiles = m_all // tm_f
    s_first = m_p // m_s

    ffn_norm4 = ffn_norm.reshape(depth, 2, 1, d)
    mix_norm3 = mix_norm.reshape(depth, 1, d)
    kv_norm2 = kv_norm.reshape(1, d)
    k_gain2 = k_gain.reshape(1, HEAD_DIM)
    q_gain3 = q_gain.reshape(-1, 1, HEAD_DIM)

    def ffn(tail, main, layer, slot, final=False):
        w32 = [_weight(w, (layer, slot)) for w in (ffn_w_gate, ffn_w_up, ffn_w_down)]
        rows = m_p if final else m_all
        in_place = tail.arr is main.arr and tail.arr.shape[0] == rows
        *ys, w16 = _ffn(tail, 1, Dest(rows, ffn_tiles - 1, IN_PLACE if in_place else None),
                        ffn_norm4, (layer, slot), *w32, tm_f, n_split=m_s if final else 0)
        main = Rows(ys[0], 0) if in_place else main
        h, _ = _ffn(main, ffn_tiles - 1, Dest(rows, 0, IN_PLACE if in_place else ys[0]),
                    ffn_norm4, (layer, slot), *w16, tm_f)
        return (h, ys[1]) if final else h

    def both(h):
        return Rows(h, ffn_tiles - 1), Rows(h, 0)

    xp = x_prompt.reshape(m_p, d)
    x_tail = jnp.concatenate([xp[m_p - (tm_f - m_s):], x_sample.reshape(m_s, d)], axis=0)
    h = ffn(Rows(x_tail, 0), Rows(xp, 0), 0, 0)

    mix_blocks = d // MIX_COL_TILE
    w_mix32 = (_weight(conv_w_in, (0,)), _weight(conv_w_in, (0,), mix_blocks),
               _weight(conv_w_in, (0,), 2 * mix_blocks))
    h, conv_s, w_mix16 = _mixer(Rows(h, s_first), m_s, Dest(m_all, s_first, IN_PLACE), mix_norm3,
                                (0,), *w_mix32, conv_w, (0,), _weight(conv_w_out, (0,)),
                                state_conv[0], t_s, m_s, MIX_COL_TILE)
    conv_zero = jnp.zeros((n_p, CONV_WIDTH - 1, d), F32)
    h, conv_p, _ = _mixer(Rows(h, 0), m_p, Dest(m_all, 0, IN_PLACE), mix_norm3, (0,),
                          *w_mix16[:3], conv_w, (0,), w_mix16[3], conv_zero, t_p,
                          PROMPT_MIX_ROW_TILE, PROMPT_MIX_COL_TILE, PROMPT_MIX_PARTS)

    h = ffn(*both(h), 0, 1)

    proj_blocks = d // CAST_COL_TILE
    (k_s, v_s, k16_s, v16_s), w_kv16 = _kv_proj(
        Rows(h, s_first), m_s, kv_norm2, _weight(w_kv), _weight(w_kv, (), proj_blocks), k_gain2,
        t_s, m_s, v_transposed=False)
    (k_p, v_p, k16_p, v16t_p), _ = _kv_proj(
        Rows(h, 0), m_p, kv_norm2, *w_kv16, k_gain2, t_p, tm_p, v_transposed=True,
        n_parts=PROMPT_PROJ_PARTS)

    h = ffn(*both(h), 1, 0)

    q16_s, w_q16 = _q_proj(Rows(h, s_first), m_s, mix_norm3, (1,), _weight(w_q, (0,)),
                           q_gain3, (0,), m_s)
    q16_p, _ = _q_proj(Rows(h, 0), m_p, mix_norm3, (1,), *w_q16, q_gain3, (0,), tm_p)
    att_s = _attn_sample(q16_s, k16_s, v16_s, cache_k, cache_v, rel_bias[0], n_s, t_s)
    att_p = _attn_prompt(q16_p, k16_p, v16t_p, rel_bias[0], n_p, t_p)
    h, w_o16 = _o_proj(Rows(h, s_first), m_s, Dest(m_all, s_first, IN_PLACE), att_s,
                       _weight(w_o, (0,)), m_s, att_transposed=False)
    h, _ = _o_proj(Rows(h, 0), m_p, Dest(m_all, 0, IN_PLACE), att_p, *w_o16, tm_p,
                   att_transposed=True)

    y_p, y_s = ffn(*both(h), 1, 1, final=True)

    keep_p, keep_s = min(PAST_ROWS, t_p), min(PAST_ROWS, t_s)
    heads = (N_HEADS, HEAD_DIM)
    return (y_p.reshape(n_p, t_p, d), y_s.reshape(n_s, t_s, d), conv_p[None],
            k_p.reshape(n_p, keep_p, *heads), v_p.reshape(n_p, keep_p, *heads), conv_s[None],
            k_s.reshape(n_s, keep_s, *heads), v_s.reshape(n_s, keep_s, *heads))
```

```python
import collections
import functools

import jax
import jax.numpy as jnp
from jax import lax
from jax.experimental import pallas as pl
from jax.experimental.pallas import tpu as pltpu

F32 = jnp.float32
BF16 = jnp.bfloat16

D_MODEL = 2048
HEAD_DIM = 128
N_HEADS = D_MODEL // HEAD_DIM
CHUNK = 64
N_PAST_CHUNKS = 8
PAST_ROWS = N_PAST_CHUNKS * CHUNK
MAX_REL = 128
CONV_WIDTH = 3
EPS = 1e-6
FFN_RES = 0.5
NEG_INF = -1e30
ATTN_SCALE = HEAD_DIM ** -0.5
LOG2_E = 1.4426950408889634

SUBLANES = 8
LANES = 128
V7X_VMEM_BYTES = 64 * 1024 * 1024
VMEM_LIMIT_BYTES = V7X_VMEM_BYTES - 8 * 1024 * 1024

PROMPT_ROW_TILE = 512
PROMPT_QO_ROW_TILE = 1024
PROMPT_PROJ_PARTS = 2
FFN_ROW_TILE = 1040
FFN_COL_TILE = 512
FFN_CAST_COL_TILE = 256
MIX_COL_TILE = 512
PROMPT_MIX_ROW_TILE = 1024
PROMPT_MIX_PARTS = 2
PROMPT_MIX_COL_TILE = 256
KV_COL_TILE = 1024
CAST_COL_TILE = 512
ATTN_Q_TILE = 512
ATTN_SUB = 256
ATTN_WIN = ATTN_SUB + PAST_ROWS
ATTN_HEADS_PER_STEP = 16

Weight = collections.namedtuple("Weight", ["arr", "lead", "col_off", "col_tiled"])
Rows = collections.namedtuple("Rows", ["arr", "first"])
Dest = collections.namedtuple("Dest", ["rows", "first", "alias"])
IN_PLACE = "in place"


def _weight(arr, lead=(), col_off=0, col_tiled=False):
    return Weight(arr, tuple(lead), col_off, col_tiled)


def _wspec(w, block, index, **mode):
    lead = w.lead

    def index_map(*ids):
        r, c = index(*ids)
        return lead + (r, c + w.col_off)

    return pl.BlockSpec((None,) * len(lead) + block, index_map, **mode)


def _resident(n_blocks):
    return dict(pipeline_mode=pl.Buffered(1)) if n_blocks == 1 else {}


def _col_spec(w, rows, tn, col):
    if w.col_tiled:
        assert w.arr.shape[1:] == (rows, tn)
        return pl.BlockSpec((None, rows, tn), lambda *ids: (col(*ids), 0, 0))
    return _wspec(w, (rows, tn), lambda *ids: (0, col(*ids)))


def _lead_spec(lead, block):
    lead = tuple(lead)
    return pl.BlockSpec((None,) * len(lead) + block, lambda *ids: lead + (0,) * len(block))


def _is_f32(w):
    return w.arr.dtype == F32


def _bf16_weight(w_ref, w16_ref):
    if w16_ref is None:
        return w_ref[...]
    w16 = w_ref[...].astype(BF16)
    w16_ref[...] = w16
    return w16


def _dest_args(src, dest):
    if dest.alias is None:
        return [], [], {}
    if dest.alias is IN_PLACE:
        assert src.arr.shape[0] == dest.rows and src.first == dest.first
        return [], [], {0: 0}
    return [pl.BlockSpec(memory_space=pl.ANY)], [dest.alias], {0: 0}


def _params(semantics):
    return pltpu.CompilerParams(dimension_semantics=semantics,
                                vmem_limit_bytes=VMEM_LIMIT_BYTES)


def _rms_rows(x, g):
    ms = jnp.mean(x * x, axis=-1, keepdims=True)
    return (x * lax.rsqrt(ms + EPS)) * g


def _ffn_kernel(*refs, emit, aliased, n_split):
    h_ref, g_ref, wg_ref, wu_ref, wd_ref, out_ref, *rest = refs[aliased:]
    split_ref = rest.pop(0) if n_split else None
    wg16_ref, wu16_ref, wd16_ref = rest[:3] if emit else (None, None, None)
    xn_ref = rest[-1]
    f = pl.program_id(1)

    @pl.when(f == 0)
    def _():
        h = h_ref[...]
        xn_ref[...] = _rms_rows(h, g_ref[...]).astype(BF16)
        out_ref[...] = h

    xn = xn_ref[...]
    gate = jnp.dot(xn, _bf16_weight(wg_ref, wg16_ref), preferred_element_type=F32)
    up = jnp.dot(xn, _bf16_weight(wu_ref, wu16_ref), preferred_element_type=F32)
    act = (gate * jax.nn.sigmoid(gate) * (FFN_RES * up)).astype(BF16)
    out_ref[...] += jnp.dot(act, _bf16_weight(wd_ref, wd16_ref), preferred_element_type=F32)

    if n_split:
        @pl.when(f == pl.num_programs(1) - 1)
        def _():
            split_ref[...] = out_ref[out_ref.shape[0] - n_split:, :]


def _ffn(src, n_tiles, dest, norm_g, g_lead, wg, wu, wd, tm, n_split=0):
    d = src.arr.shape[1]
    d_ff = wd.arr.shape[-2]
    emit = _is_f32(wg)
    tf = FFN_CAST_COL_TILE if emit else FFN_COL_TILE
    out_shape = [jax.ShapeDtypeStruct((dest.rows, d), F32)]
    out_specs = [pl.BlockSpec((tm, d), lambda i, f: (dest.first + i, 0))]
    if n_split:
        assert n_tiles == 1
        out_shape += [jax.ShapeDtypeStruct((n_split, d), F32)]
        out_specs += [pl.BlockSpec((n_split, d), lambda i, f: (0, 0))]
    h_mode = {}
    if emit:
        assert n_tiles == 1
        per = FFN_COL_TILE // tf
        tiled = jax.ShapeDtypeStruct((d_ff // FFN_COL_TILE, d, FFN_COL_TILE), BF16)
        out_shape += [tiled, tiled, jax.ShapeDtypeStruct((d_ff, d), BF16)]
        out_specs += [pl.BlockSpec((None, d, tf), lambda i, f: (f // per, 0, f % per)),
                      pl.BlockSpec((None, d, tf), lambda i, f: (f // per, 0, f % per)),
                      pl.BlockSpec((tf, d), lambda i, f: (f, 0))]
        h_mode = dict(pipeline_mode=pl.Buffered(1))
    alias_specs, alias_args, aliases = _dest_args(src, dest)
    res = pl.pallas_call(
        functools.partial(_ffn_kernel, emit=emit, aliased=len(alias_args), n_split=n_split),
        out_shape=out_shape,
        grid=(n_tiles, d_ff // tf),
        in_specs=alias_specs + [
            pl.BlockSpec((tm, d), lambda i, f: (src.first + i, 0), **h_mode),
            _lead_spec(g_lead, (1, d)),
            _col_spec(wg, d, tf, lambda i, f: f),
            _col_spec(wu, d, tf, lambda i, f: f),
            _wspec(wd, (tf, d), lambda i, f: (f, 0)),
        ],
        out_specs=out_specs,
        scratch_shapes=[pltpu.VMEM((tm, d), BF16)],
        input_output_aliases=aliases,
        compiler_params=_params(("parallel", "arbitrary")),
        name="ffn",
    )(*alias_args, src.arr, norm_g, wg.arr, wu.arr, wd.arr)
    n_act = 2 if n_split else 1
    w16 = tuple(_weight(a, col_tiled=a.ndim == 3) for a in res[n_act:])
    return tuple(res[:n_act]) + (w16,)


def _mixer_kernel(*refs, seq_len, tm, emit, aliased, n_parts):
    (h_ref, g_ref, wb_ref, wc_ref, wx_ref, cw_ref, wo_ref, p0_ref, p1_ref,
     out_ref, st_ref, *rest) = refs[aliased:]
    wb16_ref, wc16_ref, wx16_ref, wo16_ref = rest[:4] if emit else (None,) * 4
    xn_ref, carry_ref = rest[-2:]
    i = pl.program_id(0)
    j = pl.program_id(1)

    @pl.when(j == 0)
    def _():
        h = h_ref[...]
        xn_ref[...] = _rms_rows(h, g_ref[...]).astype(BF16)
        out_ref[...] = h

    if seq_len >= tm:
        @pl.when((i % (seq_len // tm)) == 0)
        def _():
            carry_ref[j] = p0_ref[0]

    wb, wc, wx, wo = (_bf16_weight(wb_ref, wb16_ref), _bf16_weight(wc_ref, wc16_ref),
                      _bf16_weight(wx_ref, wx16_ref), _bf16_weight(wo_ref, wo16_ref))
    cw = cw_ref[...]
    tn = cw.shape[1]
    hm = tm // n_parts
    bs, us = [], []
    for part in range(n_parts):
        xn = xn_ref[part * hm:(part + 1) * hm, :]
        bs.append(jnp.dot(xn, wb, preferred_element_type=F32))
        c = jnp.dot(xn, wc, preferred_element_type=F32)
        x = jnp.dot(xn, wx, preferred_element_type=F32)
        us.append(c * x)
    row = lax.broadcasted_iota(jnp.int32, (hm, tn), 0)
    for part in range(n_parts):
        u = us[part]
        if seq_len >= tm:
            prev = carry_ref[j] if part == 0 else us[part - 1][hm - 2:hm, :]
            p0 = prev[0:1, :]
            p1 = prev[1:2, :]
            pos = row
        else:
            p0 = p0_ref[...]
            p1 = p1_ref[...]
            pos = row % seq_len
        um1 = jnp.where(pos == 0, p1, pltpu.roll(u, 1, 0))
        um2 = jnp.where(pos == 0, p0, jnp.where(pos == 1, p1, pltpu.roll(u, 2, 0)))
        conv = cw[0:1, :] * um2 + cw[1:2, :] * um1 + cw[2:3, :] * u
        v = (bs[part] * conv).astype(BF16)
        out_ref[part * hm:(part + 1) * hm, :] += jnp.dot(v, wo, preferred_element_type=F32)
    if seq_len >= tm:
        carry_ref[j] = us[-1][hm - 2:hm, :]
        st_ref[0] = us[-1][hm - 2:hm, :]
    else:
        st_ref[...] = us[0].reshape(tm // seq_len, seq_len, tn)[:, seq_len - 2:seq_len, :]


def _mixer(src, rows, dest, norm_g, g_lead, wb, wc, wx, conv_w, cw_lead, wo, state, seq_len, tm,
           tn, n_parts=1):
    d = src.arr.shape[1]
    n_seq = rows // seq_len
    n_j = d // tn
    n_i = rows // tm
    emit = _is_f32(wb)
    if seq_len >= tm:
        tiles_per_seq = seq_len // tm
        p0, p1 = state, state
        p_spec = pl.BlockSpec((1, CONV_WIDTH - 1, tn), lambda i, j: (i // tiles_per_seq, 0, j))
        st_shape = jax.ShapeDtypeStruct((n_i, CONV_WIDTH - 1, d), F32)
        st_spec = pl.BlockSpec((1, CONV_WIDTH - 1, tn), lambda i, j: (i, 0, j))
    else:
        assert tm == rows and tm % seq_len == 0 and n_parts == 1
        p0 = jnp.repeat(state[:, 0], seq_len, axis=0)
        p1 = jnp.repeat(state[:, 1], seq_len, axis=0)
        p_spec = pl.BlockSpec((tm, tn), lambda i, j: (i, j))
        st_shape = jax.ShapeDtypeStruct((n_seq, CONV_WIDTH - 1, d), F32)
        st_spec = pl.BlockSpec((n_seq, CONV_WIDTH - 1, tn), lambda i, j: (0, 0, j))
    out_shape = [jax.ShapeDtypeStruct((dest.rows, d), F32), st_shape]
    out_specs = [pl.BlockSpec((tm, d), lambda i, j: (dest.first + i, 0)), st_spec]
    if emit:
        assert n_i == 1
        out_shape += [jax.ShapeDtypeStruct((d, d), BF16)] * 4
        out_specs += [pl.BlockSpec((d, tn), lambda i, j: (0, j))] * 3
        out_specs += [pl.BlockSpec((tn, d), lambda i, j: (j, 0))]
    alias_specs, alias_args, aliases = _dest_args(src, dest)
    res = pl.pallas_call(
        functools.partial(_mixer_kernel, seq_len=seq_len, tm=tm, emit=emit,
                          aliased=len(alias_args), n_parts=n_parts),
        out_shape=out_shape,
        grid=(n_i, n_j),
        in_specs=alias_specs + [
            pl.BlockSpec((tm, d), lambda i, j: (src.first + i, 0)),
            _lead_spec(g_lead, (1, d)),
            _wspec(wb, (d, tn), lambda i, j: (0, j)),
            _wspec(wc, (d, tn), lambda i, j: (0, j)),
            _wspec(wx, (d, tn), lambda i, j: (0, j)),
            pl.BlockSpec((None,) * len(cw_lead) + (CONV_WIDTH, tn),
                         lambda i, j: tuple(cw_lead) + (0, j)),
            _wspec(wo, (tn, d), lambda i, j: (j, 0)),
            p_spec,
            p_spec,
        ],
        out_specs=out_specs,
        scratch_shapes=[pltpu.VMEM((tm, d), BF16),
                        pltpu.VMEM((n_j, CONV_WIDTH - 1, tn), F32)],
        input_output_aliases=aliases,
        compiler_params=_params(("arbitrary", "arbitrary")),
        name="conv_mixer",
    )(*alias_args, src.arr, norm_g, wb.arr, wc.arr, wx.arr, conv_w, wo.arr, p0, p1)
    out, st = res[0], res[1]
    if seq_len >= tm:
        st = st[seq_len // tm - 1::seq_len // tm]
    return out, st, tuple(_weight(a) for a in res[2:])


def _kv_kernel(h_ref, g_ref, wk_ref, wv_ref, kg_ref, k32_ref, v32_ref, k16_ref, v16_ref,
               *rest, emit, n_parts, v_transposed):
    wk16_ref, wv16_ref = rest if emit else (None, None)
    wk, wv = _bf16_weight(wk_ref, wk16_ref), _bf16_weight(wv_ref, wv16_ref)
    g, kg = g_ref[...], kg_ref[...]
    hm = h_ref.shape[0] // n_parts
    ks, vs = [], []
    for part in range(n_parts):
        xn = _rms_rows(h_ref[part * hm:(part + 1) * hm, :], g).astype(BF16)
        ks.append(jnp.dot(xn, wk, preferred_element_type=F32))
        vs.append(jnp.dot(xn, wv, preferred_element_type=F32))
    for part, (k, v) in enumerate(zip(ks, vs)):
        rows = slice(part * hm, (part + 1) * hm)
        v32_ref[rows, :] = v
        for hd in range(k.shape[1] // HEAD_DIM):
            cols = slice(hd * HEAD_DIM, (hd + 1) * HEAD_DIM)
            kh = _rms_rows(k[:, cols], kg)
            k32_ref[rows, cols] = kh
            k16_ref[hd, rows, :] = kh.astype(BF16)
            if v_transposed:
                v16_ref[hd, :, rows] = v[:, cols].T.astype(BF16)
            else:
                v16_ref[hd, rows, :] = v[:, cols].astype(BF16)


def _kv_proj(src, rows, norm_g, wk, wv, k_gain, seq_len, tm, v_transposed, n_parts=1):
    d = src.arr.shape[1]
    emit = _is_f32(wk)
    tn = CAST_COL_TILE if emit else KV_COL_TILE
    hpb = tn // HEAD_DIM
    if seq_len >= tm:
        assert tm == min(PAST_ROWS, seq_len) and seq_len % tm == 0
        tiles_per_seq = seq_len // tm
        kept_rows = (rows // seq_len) * tm
        kept_spec = pl.BlockSpec((tm, tn), lambda j, i: (i // tiles_per_seq, j))
    else:
        assert tm % seq_len == 0 and seq_len <= PAST_ROWS
        kept_rows = rows
        kept_spec = pl.BlockSpec((tm, tn), lambda j, i: (i, j))
    if v_transposed:
        v16_shape = jax.ShapeDtypeStruct((N_HEADS, HEAD_DIM, rows), BF16)
        v16_spec = pl.BlockSpec((hpb, HEAD_DIM, tm), lambda j, i: (j, 0, i))
    else:
        v16_shape = jax.ShapeDtypeStruct((N_HEADS, rows, HEAD_DIM), BF16)
        v16_spec = pl.BlockSpec((hpb, tm, HEAD_DIM), lambda j, i: (j, i, 0))
    out_shape = [jax.ShapeDtypeStruct((kept_rows, d), F32),
                 jax.ShapeDtypeStruct((kept_rows, d), F32),
                 jax.ShapeDtypeStruct((N_HEADS, rows, HEAD_DIM), BF16), v16_shape]
    out_specs = [kept_spec, kept_spec,
                 pl.BlockSpec((hpb, tm, HEAD_DIM), lambda j, i: (j, i, 0)), v16_spec]
    if emit:
        assert rows == tm
        out_shape += [jax.ShapeDtypeStruct((d, d), BF16)] * 2
        out_specs += [pl.BlockSpec((d, tn), lambda j, i: (0, j))] * 2
    res = pl.pallas_call(
        functools.partial(_kv_kernel, emit=emit, n_parts=n_parts, v_transposed=v_transposed),
        out_shape=out_shape,
        grid=(d // tn, rows // tm),
        in_specs=[
            pl.BlockSpec((tm, d), lambda j, i: (src.first + i, 0)),
            pl.BlockSpec((1, d), lambda j, i: (0, 0)),
            _wspec(wk, (d, tn), lambda j, i: (0, j)),
            _wspec(wv, (d, tn), lambda j, i: (0, j)),
            pl.BlockSpec((1, HEAD_DIM), lambda j, i: (0, 0)),
        ],
        out_specs=out_specs,
        compiler_params=_params(("arbitrary", "arbitrary")),
        name="kv_proj",
    )(src.arr, norm_g, wk.arr, wv.arr, k_gain)
    return res[:4], tuple(_weight(a) for a in res[4:])


def _q_kernel(h_ref, g_ref, wq_ref, qg_ref, q16_ref, *rest, emit):
    wq16_ref = rest[0] if emit else None
    xn_ref = rest[-1]

    @pl.when(pl.program_id(1) == 0)
    def _():
        xn_ref[...] = _rms_rows(h_ref[...], g_ref[...]).astype(BF16)

    q = jnp.dot(xn_ref[...], _bf16_weight(wq_ref, wq16_ref), preferred_element_type=F32)
    qg = qg_ref[...]
    for hd in range(q.shape[1] // HEAD_DIM):
        cols = slice(hd * HEAD_DIM, (hd + 1) * HEAD_DIM)
        q16_ref[hd] = _rms_rows(q[:, cols], qg).astype(BF16)


def _q_proj(src, rows, norm_g, g_lead, wq, q_gain, qg_lead, tm):
    d = src.arr.shape[1]
    emit = _is_f32(wq)
    tn = CAST_COL_TILE if emit else d
    hpb = tn // HEAD_DIM
    out_shape = [jax.ShapeDtypeStruct((N_HEADS, rows, HEAD_DIM), BF16)]
    out_specs = [pl.BlockSpec((hpb, tm, HEAD_DIM), lambda i, j: (j, i, 0))]
    if emit:
        assert rows == tm
        out_shape += [jax.ShapeDtypeStruct((d, d), BF16)]
        out_specs += [pl.BlockSpec((d, tn), lambda i, j: (0, j))]
    res = pl.pallas_call(
        functools.partial(_q_kernel, emit=emit),
        out_shape=out_shape,
        grid=(rows // tm, d // tn),
        in_specs=[
            pl.BlockSpec((tm, d), lambda i, j: (src.first + i, 0)),
            _lead_spec(g_lead, (1, d)),
            _wspec(wq, (d, tn), lambda i, j: (0, j), **_resident(d // tn)),
            _lead_spec(qg_lead, (1, HEAD_DIM)),
        ],
        out_specs=out_specs,
        scratch_shapes=[pltpu.VMEM((tm, d), BF16)],
        compiler_params=_params(("parallel", "arbitrary")),
        name="q_proj",
    )(src.arr, norm_g, wq.arr, q_gain)
    return res[0], tuple(_weight(a) for a in res[1:])


def _o_kernel(*refs, emit, att_transposed, aliased):
    h_ref, att_ref, wo_ref, out_ref, *rest = refs[aliased:]
    wo16_ref = rest[0] if emit else None
    wo = _bf16_weight(wo_ref, wo16_ref)
    if att_transposed:
        att_t = att_ref[...].reshape(D_MODEL, att_ref.shape[2])
        y = lax.dot_general(att_t, wo, (((0,), (0,)), ((), ())), preferred_element_type=F32)
    else:
        att2d_ref = rest[-1]

        @pl.when(pl.program_id(1) == 0)
        def _():
            for hd in range(N_HEADS):
                att2d_ref[:, hd * HEAD_DIM:(hd + 1) * HEAD_DIM] = att_ref[hd]

        y = jnp.dot(att2d_ref[...], wo, preferred_element_type=F32)
    out_ref[...] = h_ref[...] + y


def _o_proj(src, rows, dest, att16, wo, tm, att_transposed):
    d = src.arr.shape[1]
    emit = _is_f32(wo)
    tn = CAST_COL_TILE if emit else d
    out_shape = [jax.ShapeDtypeStruct((dest.rows, d), F32)]
    out_specs = [pl.BlockSpec((tm, tn), lambda i, j: (dest.first + i, j))]
    if emit:
        assert rows == tm
        out_shape += [jax.ShapeDtypeStruct((d, d), BF16)]
        out_specs += [pl.BlockSpec((d, tn), lambda i, j: (0, j))]
    if att_transposed:
        att_spec = pl.BlockSpec((N_HEADS, HEAD_DIM, tm), lambda i, j: (0, 0, i))
        scratch = []
    else:
        att_spec = pl.BlockSpec((N_HEADS, tm, HEAD_DIM), lambda i, j: (0, i, 0))
        scratch = [pltpu.VMEM((tm, d), BF16)]
    alias_specs, alias_args, aliases = _dest_args(src, dest)
    res = pl.pallas_call(
        functools.partial(_o_kernel, emit=emit, att_transposed=att_transposed,
                          aliased=len(alias_args)),
        out_shape=out_shape,
        grid=(rows // tm, d // tn),
        in_specs=alias_specs + [
            pl.BlockSpec((tm, tn), lambda i, j: (src.first + i, j)),
            att_spec,
            _wspec(wo, (d, tn), lambda i, j: (0, j), **_resident(d // tn)),
        ],
        out_specs=out_specs,
        scratch_shapes=scratch,
        input_output_aliases=aliases,
        compiler_params=_params(("parallel", "arbitrary")),
        name="o_proj",
    )(*alias_args, src.arr, att16, wo.arr)
    return res[0], tuple(_weight(a) for a in res[1:])


def _build_prompt_bias(near_ref, far_ref, bias_ref):
    assert ATTN_WIN == 3 * ATTN_SUB and PAST_ROWS - MAX_REL == ATTN_WIN // 2
    row = lax.broadcasted_iota(jnp.int32, (ATTN_SUB, ATTN_SUB), 0)
    col = lax.broadcasted_iota(jnp.int32, (ATTN_SUB, ATTN_SUB), 1)
    for h in range(N_HEADS):
        circ = jnp.broadcast_to(near_ref[h], (ATTN_SUB, ATTN_SUB))
        for bit in range(ATTN_SUB.bit_length() - 1):
            circ = jnp.where((row >> bit) & 1 == 1, pltpu.roll(circ, 1 << bit, 1), circ)
        far = jnp.broadcast_to(far_ref[h], (ATTN_SUB, ATTN_SUB))
        for t in range(ATTN_WIN // ATTN_SUB):
            key = row + t * ATTN_SUB
            kc, qc = key // CHUNK, col // CHUNK
            band = (kc >= qc) & (kc <= qc + N_PAST_CHUNKS)
            val = jnp.where(key - col <= ATTN_WIN // 2, far, circ)
            bias_ref[h, t * ATTN_SUB:(t + 1) * ATTN_SUB, :] = jnp.where(band, val, NEG_INF)


def _attn_prompt_kernel(q_ref, kp_ref, kc_ref, vp_ref, vc_ref, near_ref, far_ref, o_ref,
                        bias_ref, s_refs, p_refs):
    i = pl.program_id(1)

    @pl.when((pl.program_id(0) == 0) & (i == 0))
    def _():
        _build_prompt_bias(near_ref, far_ref, bias_ref)
        p_refs[...] = jnp.zeros(p_refs.shape, BF16)

    n_before_start = jnp.where(i == 0, ATTN_Q_TILE, 0)
    groups = CHUNK // SUBLANES
    lane_tiles = ATTN_SUB // LANES
    live_blocks = [(c, lt) for c in range(ATTN_WIN // CHUNK) for lt in range(lane_tiles)
                   if lt * LANES // CHUNK <= c <= ((lt + 1) * LANES - 1) // CHUNK + N_PAST_CHUNKS]

    subs = ATTN_Q_TILE // ATTN_SUB
    n_units = ATTN_HEADS_PER_STEP * subs

    def scores(h0, u):
        h, lo = h0 + u // subs, (u % subs) * ATTN_SUB
        q = q_ref[h, lo:lo + ATTN_SUB, :]
        n_past = ATTN_Q_TILE - lo
        dims = (((1,), (1,)), ((), ()))
        s_refs[u, 0:n_past, :] = lax.dot_general(kp_ref[h, lo:, :], q, dims,
                                                 preferred_element_type=F32)
        s_refs[u, n_past:, :] = lax.dot_general(kc_ref[h, 0:ATTN_WIN - n_past, :], q, dims,
                                                preferred_element_type=F32)

    def block(c, lt):
        return slice(c * CHUNK, (c + 1) * CHUNK), slice(lt * LANES, (lt + 1) * LANES)

    def biased_max(h0, u):
        h, lo = h0 + u // subs, (u % subs) * ATTN_SUB
        s_ref = s_refs.at[u]
        m8 = [jnp.full((SUBLANES, LANES), NEG_INF, F32) for _ in range(lane_tiles)]
        for c, lt in live_blocks:
            rows, cols = block(c, lt)
            before_start = c * CHUNK < n_before_start - lo
            x = s_ref[rows, cols] * (ATTN_SCALE * LOG2_E) + bias_ref[h, rows, cols]
            x = jnp.where(before_start, NEG_INF, x)
            s_ref[rows, cols] = x
            m8[lt] = jnp.maximum(m8[lt], jnp.max(x.reshape(groups, SUBLANES, LANES), axis=0))
        return [jnp.max(v, axis=0, keepdims=True) for v in m8]

    def exponentials(u, m):
        s_ref, p_ref = s_refs.at[u], p_refs.at[u]
        l8 = [jnp.zeros((SUBLANES, LANES), F32) for _ in range(lane_tiles)]
        for c, lt in live_blocks:
            rows, cols = block(c, lt)
            e = jnp.exp2(s_ref[rows, cols] - m[lt])
            p_ref[rows, cols] = e.astype(BF16)
            l8[lt] = l8[lt] + jnp.sum(e.reshape(groups, SUBLANES, LANES), axis=0)
        return 1.0 / jnp.concatenate([jnp.sum(v, axis=0, keepdims=True) for v in l8], axis=1)

    def values(h0, u, inv):
        h, lo = h0 + u // subs, (u % subs) * ATTN_SUB
        p_ref = p_refs.at[u]
        n_past = ATTN_Q_TILE - lo
        o = jnp.dot(vp_ref[h, :, lo:], p_ref[0:n_past, :], preferred_element_type=F32)
        o = o + jnp.dot(vc_ref[h, :, 0:ATTN_WIN - n_past], p_ref[n_past:, :],
                        preferred_element_type=F32)
        o_ref[h, :, lo:lo + ATTN_SUB] = (o * inv).astype(BF16)

    def step(t, carry):
        h0 = t * ATTN_HEADS_PER_STEP
        m, inv = {}, {}
        for r in range(n_units + 3):
            if 0 <= r - 3:
                values(h0, r - 3, inv.pop(r - 3))
            if 0 <= r - 2 < n_units:
                inv[r - 2] = exponentials(r - 2, m.pop(r - 2))
            if 0 <= r - 1 < n_units:
                m[r - 1] = biased_max(h0, r - 1)
            if r < n_units:
                scores(h0, r)
        return carry

    lax.fori_loop(0, N_HEADS // ATTN_HEADS_PER_STEP, step, 0)


def _prompt_bias_rows(rel_bias):
    assert ATTN_SUB == 2 * MAX_REL
    k_mod = (-jnp.arange(ATTN_SUB)) % ATTN_SUB
    idx = 2 * MAX_REL - (k_mod - ATTN_WIN // 2) % ATTN_SUB
    near = rel_bias[:, None, idx].astype(F32) * LOG2_E
    far = jnp.broadcast_to(rel_bias[:, None, 2 * MAX_REL:].astype(F32) * LOG2_E, near.shape)
    return near, far


def _attn_prompt(q16, k16, v16t, rel_bias, n_seq, seq_len):
    tiles = seq_len // ATTN_Q_TILE
    cur = lambda b, i: b * tiles + i
    past = lambda b, i: b * tiles + jnp.maximum(i - 1, 0)
    rows = lambda at: pl.BlockSpec((N_HEADS, ATTN_Q_TILE, HEAD_DIM), lambda b, i: (0, at(b, i), 0))
    cols = lambda at: pl.BlockSpec((N_HEADS, HEAD_DIM, ATTN_Q_TILE), lambda b, i: (0, 0, at(b, i)))
    near, far = _prompt_bias_rows(rel_bias)
    bias_row = pl.BlockSpec(near.shape, lambda b, i: (0, 0, 0))
    n_units = ATTN_HEADS_PER_STEP * (ATTN_Q_TILE // ATTN_SUB)
    return pl.pallas_call(
        _attn_prompt_kernel,
        out_shape=jax.ShapeDtypeStruct(v16t.shape, BF16),
        grid=(n_seq, tiles),
        in_specs=[rows(cur), rows(past), rows(cur), cols(past), cols(cur), bias_row, bias_row],
        out_specs=cols(cur),
        scratch_shapes=[pltpu.VMEM((N_HEADS, ATTN_WIN, ATTN_SUB), F32),
                        pltpu.VMEM((n_units, ATTN_WIN, ATTN_SUB), F32),
                        pltpu.VMEM((n_units, ATTN_WIN, ATTN_SUB), BF16)],
        compiler_params=_params(("arbitrary", "arbitrary")),
        name="attn_prompt",
    )(q16, k16, k16, v16t, v16t, near, far)


def _attn_sample_kernel(q_ref, kn_ref, vn_ref, ck_ref, cv_ref, bc_ref, bn_ref, o_ref):
    cache_len = ck_ref.shape[1] // N_HEADS
    for h in range(N_HEADS):
        q = q_ref[h]
        kc = ck_ref[0, pl.ds(h, cache_len, stride=N_HEADS), :].astype(BF16)
        vc = cv_ref[0, pl.ds(h, cache_len, stride=N_HEADS), :].astype(BF16)
        dims = (((1,), (1,)), ((), ()))
        sc = lax.dot_general(q, kc, dims, preferred_element_type=F32) * ATTN_SCALE + bc_ref[h]
        sn = lax.dot_general(q, kn_ref[h], dims, preferred_element_type=F32) * ATTN_SCALE + bn_ref[h]
        m = jnp.maximum(jnp.max(sc, axis=-1, keepdims=True), jnp.max(sn, axis=-1, keepdims=True))
        ec = jnp.exp(sc - m)
        en = jnp.exp(sn - m)
        inv = 1.0 / (jnp.sum(ec, axis=-1, keepdims=True) + jnp.sum(en, axis=-1, keepdims=True))
        o = jnp.dot((ec * inv).astype(BF16), vc, preferred_element_type=F32)
        o = o + jnp.dot((en * inv).astype(BF16), vn_ref[h], preferred_element_type=F32)
        o_ref[h] = o.astype(BF16)


def _attn_sample(q16, k16, v16, cache_k, cache_v, rel_bias, n_seq, seq_len):
    cache_len = cache_k.shape[1]
    ck = cache_k.reshape(n_seq, cache_len * N_HEADS, HEAD_DIM)
    cv = cache_v.reshape(n_seq, cache_len * N_HEADS, HEAD_DIM)
    n_far = max(cache_len - MAX_REL, 0)
    a = jnp.arange(seq_len)[:, None]
    w = jnp.arange(n_far, cache_len + seq_len)[None, :]
    idx = jnp.clip(a - w + cache_len, -MAX_REL, MAX_REL) + MAX_REL
    near = rel_bias[:, idx].astype(F32)
    far = jnp.broadcast_to(rel_bias[:, None, 2 * MAX_REL:].astype(F32), (N_HEADS, seq_len, n_far))
    bias_c = jnp.concatenate([far, near[:, :, :cache_len - n_far]], axis=2)
    bias_n = near[:, :, cache_len - n_far:]
    new = pl.BlockSpec((N_HEADS, seq_len, HEAD_DIM), lambda b: (0, b, 0))
    cache = pl.BlockSpec((1, cache_len * N_HEADS, HEAD_DIM), lambda b: (b, 0, 0))
    return pl.pallas_call(
        _attn_sample_kernel,
        out_shape=jax.ShapeDtypeStruct(q16.shape, BF16),
        grid=(n_seq,),
        in_specs=[new, new, new, cache, cache,
                  pl.BlockSpec(bias_c.shape, lambda b: (0, 0, 0)),
                  pl.BlockSpec(bias_n.shape, lambda b: (0, 0, 0))],
        out_specs=new,
        compiler_params=_params(("parallel",)),
        name="attn_sample",
    )(q16, k16, v16, ck, cv, bias_c, bias_n)


def kernel(x_prompt, x_sample, state_conv, cache_k, cache_v, ffn_norm, ffn_w_gate, ffn_w_up,
           ffn_w_down, mix_norm, conv_w_in, conv_w, conv_w_out, kv_norm, w_kv, k_gain,
           w_q, q_gain, rel_bias, w_o):
    depth = ffn_norm.shape[0]
    assert depth == 2 and conv_w_in.shape[0] == 1 and w_q.shape[0] == 1
    d = D_MODEL
    n_p, t_p, _ = x_prompt.shape
    n_s, t_s, _ = x_sample.shape
    m_p, m_s = n_p * t_p, n_s * t_s
    m_all = m_p + m_s
    tm_p, tm_f = PROMPT_ROW_TILE, FFN_ROW_TILE
    assert m_all % tm_f == 0 and m_s < tm_f and m_p % m_s == 0 and m_p % tm_p == 0
    ffn_tiles = m_all // tm_f
    s_first = m_p // m_s

    ffn_norm4 = ffn_norm.reshape(depth, 2, 1, d)
    mix_norm3 = mix_norm.reshape(depth, 1, d)
    kv_norm2 = kv_norm.reshape(1, d)
    k_gain2 = k_gain.reshape(1, HEAD_DIM)
    q_gain3 = q_gain.reshape(-1, 1, HEAD_DIM)

    def ffn(tail, main, layer, slot, final=False):
        w32 = [_weight(w, (layer, slot)) for w in (ffn_w_gate, ffn_w_up, ffn_w_down)]
        rows = m_p if final else m_all
        in_place = tail.arr is main.arr and tail.arr.shape[0] == rows
        *ys, w16 = _ffn(tail, 1, Dest(rows, ffn_tiles - 1, IN_PLACE if in_place else None),
                        ffn_norm4, (layer, slot), *w32, tm_f, n_split=m_s if final else 0)
        main = Rows(ys[0], 0) if in_place else main
        h, _ = _ffn(main, ffn_tiles - 1, Dest(rows, 0, IN_PLACE if in_place else ys[0]),
                    ffn_norm4, (layer, slot), *w16, tm_f)
        return (h, ys[1]) if final else h

    def both(h):
        return Rows(h, ffn_tiles - 1), Rows(h, 0)

    xp = x_prompt.reshape(m_p, d)
    x_tail = jnp.concatenate([xp[m_p - (tm_f - m_s):], x_sample.reshape(m_s, d)], axis=0)
    h = ffn(Rows(x_tail, 0), Rows(xp, 0), 0, 0)

    mix_blocks = d // MIX_COL_TILE
    w_mix32 = (_weight(conv_w_in, (0,)), _weight(conv_w_in, (0,), mix_blocks),
               _weight(conv_w_in, (0,), 2 * mix_blocks))
    h, conv_s, w_mix16 = _mixer(Rows(h, s_first), m_s, Dest(m_all, s_first, IN_PLACE), mix_norm3,
                                (0,), *w_mix32, conv_w, (0,), _weight(conv_w_out, (0,)),
                                state_conv[0], t_s, m_s, MIX_COL_TILE)
    conv_zero = jnp.zeros((n_p, CONV_WIDTH - 1, d), F32)
    h, conv_p, _ = _mixer(Rows(h, 0), m_p, Dest(m_all, 0, IN_PLACE), mix_norm3, (0,),
                          *w_mix16[:3], conv_w, (0,), w_mix16[3], conv_zero, t_p,
                          PROMPT_MIX_ROW_TILE, PROMPT_MIX_COL_TILE, PROMPT_MIX_PARTS)

    h = ffn(*both(h), 0, 1)

    proj_blocks = d // CAST_COL_TILE
    (k_s, v_s, k16_s, v16_s), w_kv16 = _kv_proj(
        Rows(h, s_first), m_s, kv_norm2, _weight(w_kv), _weight(w_kv, (), proj_blocks), k_gain2,
        t_s, m_s, v_transposed=False)
    (k_p, v_p, k16_p, v16t_p), _ = _kv_proj(
        Rows(h, 0), m_p, kv_norm2, *w_kv16, k_gain2, t_p, tm_p, v_transposed=True,
        n_parts=PROMPT_PROJ_PARTS)

    h = ffn(*both(h), 1, 0)

    q16_s, w_q16 = _q_proj(Rows(h, s_first), m_s, mix_norm3, (1,), _weight(w_q, (0,)),
                           q_gain3, (0,), m_s)
    q16_p, _ = _q_proj(Rows(h, 0), m_p, mix_norm3, (1,), *w_q16, q_gain3, (0,), PROMPT_QO_ROW_TILE)
    att_s = _attn_sample(q16_s, k16_s, v16_s, cache_k, cache_v, rel_bias[0], n_s, t_s)
    att_p = _attn_prompt(q16_p, k16_p, v16t_p, rel_bias[0], n_p, t_p)
    h, w_o16 = _o_proj(Rows(h, s_first), m_s, Dest(m_all, s_first, IN_PLACE), att_s,
                       _weight(w_o, (0,)), m_s, att_transposed=False)
    h, _ = _o_proj(Rows(h, 0), m_p, Dest(m_all, 0, IN_PLACE), att_p, *w_o16, PROMPT_QO_ROW_TILE,
                   att_transposed=True)

    y_p, y_s = ffn(*both(h), 1, 1, final=True)

    keep_p, keep_s = min(PAST_ROWS, t_p), min(PAST_ROWS, t_s)
    heads = (N_HEADS, HEAD_DIM)
    return (y_p.reshape(n_p, t_p, d), y_s.reshape(n_s, t_s, d), conv_p[None],
            k_p.reshape(n_p, keep_p, *heads), v_p.reshape(n_p, keep_p, *heads), conv_s[None],
            k_s.reshape(n_s, keep_s, *heads), v_s.reshape(n_s, keep_s, *heads))
```

```python
import collections
import functools

import jax
import jax.numpy as jnp
from jax import lax
from jax.experimental import pallas as pl
from jax.experimental.pallas import tpu as pltpu

F32 = jnp.float32
BF16 = jnp.bfloat16

D_MODEL = 2048
HEAD_DIM = 128
N_HEADS = D_MODEL // HEAD_DIM
CHUNK = 64
N_PAST_CHUNKS = 8
PAST_ROWS = N_PAST_CHUNKS * CHUNK
MAX_REL = 128
CONV_WIDTH = 3
EPS = 1e-6
FFN_RES = 0.5
NEG_INF = -1e30
ATTN_SCALE = HEAD_DIM ** -0.5
LOG2_E = 1.4426950408889634

SUBLANES = 8
LANES = 128
V7X_VMEM_BYTES = 64 * 1024 * 1024
VMEM_RESERVE_BYTES = 8 * 1024 * 1024
VMEM_LIMIT_BYTES = V7X_VMEM_BYTES - VMEM_RESERVE_BYTES

PROMPT_ROW_TILE = 512
PROMPT_QO_ROW_TILE = 1024
PROMPT_PROJ_PARTS = 2
FFN_ROW_TILE = 1040
FFN_COL_TILE = 512
FFN_CAST_COL_TILE = 256
MIX_COL_TILE = 512
PROMPT_MIX_ROW_TILE = 1024
PROMPT_MIX_PARTS = 2
PROMPT_MIX_COL_TILE = 256
KV_COL_TILE = 1024
CAST_COL_TILE = 512
ATTN_Q_TILE = 512
ATTN_SUB = 256
ATTN_WIN = ATTN_SUB + PAST_ROWS
ATTN_STAGES = 4

Weight = collections.namedtuple("Weight", ["arr", "lead", "col_off", "col_tiled"])
Rows = collections.namedtuple("Rows", ["arr", "first"])
Dest = collections.namedtuple("Dest", ["rows", "first", "alias"])
IN_PLACE = "in place"


def _weight(arr, lead=(), col_off=0, col_tiled=False):
    return Weight(arr, tuple(lead), col_off, col_tiled)


def _wspec(w, block, index, **mode):
    lead = w.lead

    def index_map(*ids):
        r, c = index(*ids)
        return lead + (r, c + w.col_off)

    return pl.BlockSpec((None,) * len(lead) + block, index_map, **mode)


def _resident(n_blocks):
    return dict(pipeline_mode=pl.Buffered(1)) if n_blocks == 1 else {}


def _col_spec(w, rows, tn, col):
    if w.col_tiled:
        assert w.arr.shape[1:] == (rows, tn)
        return pl.BlockSpec((None, rows, tn), lambda *ids: (col(*ids), 0, 0))
    return _wspec(w, (rows, tn), lambda *ids: (0, col(*ids)))


def _lead_spec(lead, block):
    lead = tuple(lead)
    return pl.BlockSpec((None,) * len(lead) + block, lambda *ids: lead + (0,) * len(block))


def _is_f32(w):
    return w.arr.dtype == F32


def _bf16_weight(w_ref, w16_ref):
    if w16_ref is None:
        return w_ref[...]
    w16 = w_ref[...].astype(BF16)
    w16_ref[...] = w16
    return w16


def _dest_args(src, dest):
    if dest.alias is None:
        return [], [], {}
    if dest.alias is IN_PLACE:
        assert src.arr.shape[0] == dest.rows and src.first == dest.first
        return [], [], {0: 0}
    return [pl.BlockSpec(memory_space=pl.ANY)], [dest.alias], {0: 0}


def _params(semantics):
    return pltpu.CompilerParams(dimension_semantics=semantics,
                                vmem_limit_bytes=VMEM_LIMIT_BYTES)


def _rms_rows(x, g):
    ms = jnp.mean(x * x, axis=-1, keepdims=True)
    return (x * lax.rsqrt(ms + EPS)) * g


def _ffn_kernel(*refs, emit, aliased, n_split):
    h_ref, g_ref, wg_ref, wu_ref, wd_ref, out_ref, *rest = refs[aliased:]
    split_ref = rest.pop(0) if n_split else None
    wg16_ref, wu16_ref, wd16_ref = rest[:3] if emit else (None, None, None)
    xn_ref = rest[-1]
    f = pl.program_id(1)

    @pl.when(f == 0)
    def _():
        h = h_ref[...]
        xn_ref[...] = _rms_rows(h, g_ref[...]).astype(BF16)
        out_ref[...] = h

    xn = xn_ref[...]
    gate = jnp.dot(xn, _bf16_weight(wg_ref, wg16_ref), preferred_element_type=F32)
    up = jnp.dot(xn, _bf16_weight(wu_ref, wu16_ref), preferred_element_type=F32)
    act = (gate * jax.nn.sigmoid(gate) * (FFN_RES * up)).astype(BF16)
    out_ref[...] += jnp.dot(act, _bf16_weight(wd_ref, wd16_ref), preferred_element_type=F32)

    if n_split:
        @pl.when(f == pl.num_programs(1) - 1)
        def _():
            split_ref[...] = out_ref[out_ref.shape[0] - n_split:, :]


def _ffn(src, n_tiles, dest, norm_g, g_lead, wg, wu, wd, tm, n_split=0):
    d = src.arr.shape[1]
    d_ff = wd.arr.shape[-2]
    emit = _is_f32(wg)
    tf = FFN_CAST_COL_TILE if emit else FFN_COL_TILE
    out_shape = [jax.ShapeDtypeStruct((dest.rows, d), F32)]
    out_specs = [pl.BlockSpec((tm, d), lambda i, f: (dest.first + i, 0))]
    if n_split:
        assert n_tiles == 1
        out_shape += [jax.ShapeDtypeStruct((n_split, d), F32)]
        out_specs += [pl.BlockSpec((n_split, d), lambda i, f: (0, 0))]
    h_mode = {}
    if emit:
        assert n_tiles == 1
        per = FFN_COL_TILE // tf
        tiled = jax.ShapeDtypeStruct((d_ff // FFN_COL_TILE, d, FFN_COL_TILE), BF16)
        out_shape += [tiled, tiled, jax.ShapeDtypeStruct((d_ff, d), BF16)]
        out_specs += [pl.BlockSpec((None, d, tf), lambda i, f: (f // per, 0, f % per)),
                      pl.BlockSpec((None, d, tf), lambda i, f: (f // per, 0, f % per)),
                      pl.BlockSpec((tf, d), lambda i, f: (f, 0))]
        h_mode = dict(pipeline_mode=pl.Buffered(1))
    alias_specs, alias_args, aliases = _dest_args(src, dest)
    res = pl.pallas_call(
        functools.partial(_ffn_kernel, emit=emit, aliased=len(alias_args), n_split=n_split),
        out_shape=out_shape,
        grid=(n_tiles, d_ff // tf),
        in_specs=alias_specs + [
            pl.BlockSpec((tm, d), lambda i, f: (src.first + i, 0), **h_mode),
            _lead_spec(g_lead, (1, d)),
            _col_spec(wg, d, tf, lambda i, f: f),
            _col_spec(wu, d, tf, lambda i, f: f),
            _wspec(wd, (tf, d), lambda i, f: (f, 0)),
        ],
        out_specs=out_specs,
        scratch_shapes=[pltpu.VMEM((tm, d), BF16)],
        input_output_aliases=aliases,
        compiler_params=_params(("parallel", "arbitrary")),
        name="ffn",
    )(*alias_args, src.arr, norm_g, wg.arr, wu.arr, wd.arr)
    n_act = 2 if n_split else 1
    w16 = tuple(_weight(a, col_tiled=a.ndim == 3) for a in res[n_act:])
    return tuple(res[:n_act]) + (w16,)


def _mixer_kernel(*refs, seq_len, tm, emit, aliased, n_parts):
    (h_ref, g_ref, wb_ref, wc_ref, wx_ref, cw_ref, wo_ref, p0_ref, p1_ref,
     out_ref, st_ref, *rest) = refs[aliased:]
    wb16_ref, wc16_ref, wx16_ref, wo16_ref = rest[:4] if emit else (None,) * 4
    xn_ref, carry_ref = rest[-2:]
    i = pl.program_id(0)
    j = pl.program_id(1)

    @pl.when(j == 0)
    def _():
        h = h_ref[...]
        xn_ref[...] = _rms_rows(h, g_ref[...]).astype(BF16)
        out_ref[...] = h

    if seq_len >= tm:
        @pl.when((i % (seq_len // tm)) == 0)
        def _():
            carry_ref[j] = p0_ref[0]

    wb, wc, wx, wo = (_bf16_weight(wb_ref, wb16_ref), _bf16_weight(wc_ref, wc16_ref),
                      _bf16_weight(wx_ref, wx16_ref), _bf16_weight(wo_ref, wo16_ref))
    cw = cw_ref[...]
    tn = cw.shape[1]
    hm = tm // n_parts
    bs, us = [], []
    for part in range(n_parts):
        xn = xn_ref[part * hm:(part + 1) * hm, :]
        bs.append(jnp.dot(xn, wb, preferred_element_type=F32))
        c = jnp.dot(xn, wc, preferred_element_type=F32)
        x = jnp.dot(xn, wx, preferred_element_type=F32)
        us.append(c * x)
    row = lax.broadcasted_iota(jnp.int32, (hm, tn), 0)
    for part in range(n_parts):
        u = us[part]
        if seq_len >= tm:
            prev = carry_ref[j] if part == 0 else us[part - 1][hm - 2:hm, :]
            p0 = prev[0:1, :]
            p1 = prev[1:2, :]
            pos = row
        else:
            p0 = p0_ref[...]
            p1 = p1_ref[...]
            pos = row % seq_len
        um1 = jnp.where(pos == 0, p1, pltpu.roll(u, 1, 0))
        um2 = jnp.where(pos == 0, p0, jnp.where(pos == 1, p1, pltpu.roll(u, 2, 0)))
        conv = cw[0:1, :] * um2 + cw[1:2, :] * um1 + cw[2:3, :] * u
        v = (bs[part] * conv).astype(BF16)
        out_ref[part * hm:(part + 1) * hm, :] += jnp.dot(v, wo, preferred_element_type=F32)
    if seq_len >= tm:
        carry_ref[j] = us[-1][hm - 2:hm, :]
        st_ref[0] = us[-1][hm - 2:hm, :]
    else:
        st_ref[...] = us[0].reshape(tm // seq_len, seq_len, tn)[:, seq_len - 2:seq_len, :]


def _mixer(src, rows, dest, norm_g, g_lead, wb, wc, wx, conv_w, cw_lead, wo, state, seq_len, tm,
           tn, n_parts=1):
    d = src.arr.shape[1]
    n_seq = rows // seq_len
    n_j = d // tn
    n_i = rows // tm
    emit = _is_f32(wb)
    if seq_len >= tm:
        tiles_per_seq = seq_len // tm
        p0, p1 = state, state
        p_spec = pl.BlockSpec((1, CONV_WIDTH - 1, tn), lambda i, j: (i // tiles_per_seq, 0, j))
        st_shape = jax.ShapeDtypeStruct((n_i, CONV_WIDTH - 1, d), F32)
        st_spec = pl.BlockSpec((1, CONV_WIDTH - 1, tn), lambda i, j: (i, 0, j))
    else:
        assert tm == rows and tm % seq_len == 0 and n_parts == 1
        p0 = jnp.repeat(state[:, 0], seq_len, axis=0)
        p1 = jnp.repeat(state[:, 1], seq_len, axis=0)
        p_spec = pl.BlockSpec((tm, tn), lambda i, j: (i, j))
        st_shape = jax.ShapeDtypeStruct((n_seq, CONV_WIDTH - 1, d), F32)
        st_spec = pl.BlockSpec((n_seq, CONV_WIDTH - 1, tn), lambda i, j: (0, 0, j))
    out_shape = [jax.ShapeDtypeStruct((dest.rows, d), F32), st_shape]
    out_specs = [pl.BlockSpec((tm, d), lambda i, j: (dest.first + i, 0)), st_spec]
    if emit:
        assert n_i == 1
        out_shape += [jax.ShapeDtypeStruct((d, d), BF16)] * 4
        out_specs += [pl.BlockSpec((d, tn), lambda i, j: (0, j))] * 3
        out_specs += [pl.BlockSpec((tn, d), lambda i, j: (j, 0))]
    alias_specs, alias_args, aliases = _dest_args(src, dest)
    res = pl.pallas_call(
        functools.partial(_mixer_kernel, seq_len=seq_len, tm=tm, emit=emit,
                          aliased=len(alias_args), n_parts=n_parts),
        out_shape=out_shape,
        grid=(n_i, n_j),
        in_specs=alias_specs + [
            pl.BlockSpec((tm, d), lambda i, j: (src.first + i, 0)),
            _lead_spec(g_lead, (1, d)),
            _wspec(wb, (d, tn), lambda i, j: (0, j)),
            _wspec(wc, (d, tn), lambda i, j: (0, j)),
            _wspec(wx, (d, tn), lambda i, j: (0, j)),
            pl.BlockSpec((None,) * len(cw_lead) + (CONV_WIDTH, tn),
                         lambda i, j: tuple(cw_lead) + (0, j)),
            _wspec(wo, (tn, d), lambda i, j: (j, 0)),
            p_spec,
            p_spec,
        ],
        out_specs=out_specs,
        scratch_shapes=[pltpu.VMEM((tm, d), BF16),
                        pltpu.VMEM((n_j, CONV_WIDTH - 1, tn), F32)],
        input_output_aliases=aliases,
        compiler_params=_params(("arbitrary", "arbitrary")),
        name="conv_mixer",
    )(*alias_args, src.arr, norm_g, wb.arr, wc.arr, wx.arr, conv_w, wo.arr, p0, p1)
    out, st = res[0], res[1]
    if seq_len >= tm:
        st = st[seq_len // tm - 1::seq_len // tm]
    return out, st, tuple(_weight(a) for a in res[2:])


def _kv_kernel(h_ref, g_ref, wk_ref, wv_ref, kg_ref, k32_ref, v32_ref, k16_ref, v16_ref,
               *rest, emit, n_parts, v_transposed):
    wk16_ref, wv16_ref = rest if emit else (None, None)
    wk, wv = _bf16_weight(wk_ref, wk16_ref), _bf16_weight(wv_ref, wv16_ref)
    g, kg = g_ref[...], kg_ref[...]
    hm = h_ref.shape[0] // n_parts
    ks, vs = [], []
    for part in range(n_parts):
        xn = _rms_rows(h_ref[part * hm:(part + 1) * hm, :], g).astype(BF16)
        ks.append(jnp.dot(xn, wk, preferred_element_type=F32))
        vs.append(jnp.dot(xn, wv, preferred_element_type=F32))
    for part, (k, v) in enumerate(zip(ks, vs)):
        rows = slice(part * hm, (part + 1) * hm)
        v32_ref[rows, :] = v
        for hd in range(k.shape[1] // HEAD_DIM):
            cols = slice(hd * HEAD_DIM, (hd + 1) * HEAD_DIM)
            kh = _rms_rows(k[:, cols], kg)
            k32_ref[rows, cols] = kh
            k16_ref[hd, rows, :] = kh.astype(BF16)
            if v_transposed:
                v16_ref[hd, :, rows] = v[:, cols].T.astype(BF16)
            else:
                v16_ref[hd, rows, :] = v[:, cols].astype(BF16)


def _kv_proj(src, rows, norm_g, wk, wv, k_gain, seq_len, tm, v_transposed, n_parts=1):
    d = src.arr.shape[1]
    emit = _is_f32(wk)
    tn = CAST_COL_TILE if emit else KV_COL_TILE
    hpb = tn // HEAD_DIM
    if seq_len >= tm:
        assert tm == min(PAST_ROWS, seq_len) and seq_len % tm == 0
        tiles_per_seq = seq_len // tm
        kept_rows = (rows // seq_len) * tm
        kept_spec = pl.BlockSpec((tm, tn), lambda j, i: (i // tiles_per_seq, j))
    else:
        assert tm % seq_len == 0 and seq_len <= PAST_ROWS
        kept_rows = rows
        kept_spec = pl.BlockSpec((tm, tn), lambda j, i: (i, j))
    if v_transposed:
        v16_shape = jax.ShapeDtypeStruct((N_HEADS, HEAD_DIM, rows), BF16)
        v16_spec = pl.BlockSpec((hpb, HEAD_DIM, tm), lambda j, i: (j, 0, i))
    else:
        v16_shape = jax.ShapeDtypeStruct((N_HEADS, rows, HEAD_DIM), BF16)
        v16_spec = pl.BlockSpec((hpb, tm, HEAD_DIM), lambda j, i: (j, i, 0))
    out_shape = [jax.ShapeDtypeStruct((kept_rows, d), F32),
                 jax.ShapeDtypeStruct((kept_rows, d), F32),
                 jax.ShapeDtypeStruct((N_HEADS, rows, HEAD_DIM), BF16), v16_shape]
    out_specs = [kept_spec, kept_spec,
                 pl.BlockSpec((hpb, tm, HEAD_DIM), lambda j, i: (j, i, 0)), v16_spec]
    if emit:
        assert rows == tm
        out_shape += [jax.ShapeDtypeStruct((d, d), BF16)] * 2
        out_specs += [pl.BlockSpec((d, tn), lambda j, i: (0, j))] * 2
    res = pl.pallas_call(
        functools.partial(_kv_kernel, emit=emit, n_parts=n_parts, v_transposed=v_transposed),
        out_shape=out_shape,
        grid=(d // tn, rows // tm),
        in_specs=[
            pl.BlockSpec((tm, d), lambda j, i: (src.first + i, 0)),
            pl.BlockSpec((1, d), lambda j, i: (0, 0)),
            _wspec(wk, (d, tn), lambda j, i: (0, j)),
            _wspec(wv, (d, tn), lambda j, i: (0, j)),
            pl.BlockSpec((1, HEAD_DIM), lambda j, i: (0, 0)),
        ],
        out_specs=out_specs,
        compiler_params=_params(("arbitrary", "arbitrary")),
        name="kv_proj",
    )(src.arr, norm_g, wk.arr, wv.arr, k_gain)
    return res[:4], tuple(_weight(a) for a in res[4:])


def _q_kernel(h_ref, g_ref, wq_ref, qg_ref, q16_ref, *rest, emit):
    wq16_ref = rest[0] if emit else None
    xn_ref = rest[-1]

    @pl.when(pl.program_id(1) == 0)
    def _():
        xn_ref[...] = _rms_rows(h_ref[...], g_ref[...]).astype(BF16)

    q = jnp.dot(xn_ref[...], _bf16_weight(wq_ref, wq16_ref), preferred_element_type=F32)
    qg = qg_ref[...]
    for hd in range(q.shape[1] // HEAD_DIM):
        cols = slice(hd * HEAD_DIM, (hd + 1) * HEAD_DIM)
        q16_ref[hd] = _rms_rows(q[:, cols], qg).astype(BF16)


def _q_proj(src, rows, norm_g, g_lead, wq, q_gain, qg_lead, tm):
    d = src.arr.shape[1]
    emit = _is_f32(wq)
    tn = CAST_COL_TILE if emit else d
    hpb = tn // HEAD_DIM
    out_shape = [jax.ShapeDtypeStruct((N_HEADS, rows, HEAD_DIM), BF16)]
    out_specs = [pl.BlockSpec((hpb, tm, HEAD_DIM), lambda i, j: (j, i, 0))]
    if emit:
        assert rows == tm
        out_shape += [jax.ShapeDtypeStruct((d, d), BF16)]
        out_specs += [pl.BlockSpec((d, tn), lambda i, j: (0, j))]
    res = pl.pallas_call(
        functools.partial(_q_kernel, emit=emit),
        out_shape=out_shape,
        grid=(rows // tm, d // tn),
        in_specs=[
            pl.BlockSpec((tm, d), lambda i, j: (src.first + i, 0)),
            _lead_spec(g_lead, (1, d)),
            _wspec(wq, (d, tn), lambda i, j: (0, j), **_resident(d // tn)),
            _lead_spec(qg_lead, (1, HEAD_DIM)),
        ],
        out_specs=out_specs,
        scratch_shapes=[pltpu.VMEM((tm, d), BF16)],
        compiler_params=_params(("parallel", "arbitrary")),
        name="q_proj",
    )(src.arr, norm_g, wq.arr, q_gain)
    return res[0], tuple(_weight(a) for a in res[1:])


def _o_kernel(*refs, emit, att_transposed, aliased):
    h_ref, att_ref, wo_ref, out_ref, *rest = refs[aliased:]
    wo16_ref = rest[0] if emit else None
    wo = _bf16_weight(wo_ref, wo16_ref)
    if att_transposed:
        att_t = att_ref[...].reshape(D_MODEL, att_ref.shape[2])
        y = lax.dot_general(att_t, wo, (((0,), (0,)), ((), ())), preferred_element_type=F32)
    else:
        att2d_ref = rest[-1]

        @pl.when(pl.program_id(1) == 0)
        def _():
            for hd in range(N_HEADS):
                att2d_ref[:, hd * HEAD_DIM:(hd + 1) * HEAD_DIM] = att_ref[hd]

        y = jnp.dot(att2d_ref[...], wo, preferred_element_type=F32)
    out_ref[...] = h_ref[...] + y


def _o_proj(src, rows, dest, att16, wo, tm, att_transposed):
    d = src.arr.shape[1]
    emit = _is_f32(wo)
    tn = CAST_COL_TILE if emit else d
    out_shape = [jax.ShapeDtypeStruct((dest.rows, d), F32)]
    out_specs = [pl.BlockSpec((tm, tn), lambda i, j: (dest.first + i, j))]
    if emit:
        assert rows == tm
        out_shape += [jax.ShapeDtypeStruct((d, d), BF16)]
        out_specs += [pl.BlockSpec((d, tn), lambda i, j: (0, j))]
    if att_transposed:
        att_spec = pl.BlockSpec((N_HEADS, HEAD_DIM, tm), lambda i, j: (0, 0, i))
        scratch = []
    else:
        att_spec = pl.BlockSpec((N_HEADS, tm, HEAD_DIM), lambda i, j: (0, i, 0))
        scratch = [pltpu.VMEM((tm, d), BF16)]
    alias_specs, alias_args, aliases = _dest_args(src, dest)
    res = pl.pallas_call(
        functools.partial(_o_kernel, emit=emit, att_transposed=att_transposed,
                          aliased=len(alias_args)),
        out_shape=out_shape,
        grid=(rows // tm, d // tn),
        in_specs=alias_specs + [
            pl.BlockSpec((tm, tn), lambda i, j: (src.first + i, j)),
            att_spec,
            _wspec(wo, (d, tn), lambda i, j: (0, j), **_resident(d // tn)),
        ],
        out_specs=out_specs,
        scratch_shapes=scratch,
        input_output_aliases=aliases,
        compiler_params=_params(("parallel", "arbitrary")),
        name="o_proj",
    )(*alias_args, src.arr, att16, wo.arr)
    return res[0], tuple(_weight(a) for a in res[1:])


def _build_prompt_bias(near_ref, far_ref, bias_ref):
    assert ATTN_WIN == 3 * ATTN_SUB and PAST_ROWS - MAX_REL == ATTN_WIN // 2
    row = lax.broadcasted_iota(jnp.int32, (ATTN_SUB, ATTN_SUB), 0)
    col = lax.broadcasted_iota(jnp.int32, (ATTN_SUB, ATTN_SUB), 1)
    for h in range(N_HEADS):
        circ = jnp.broadcast_to(near_ref[h], (ATTN_SUB, ATTN_SUB))
        for bit in range(ATTN_SUB.bit_length() - 1):
            circ = jnp.where((row >> bit) & 1 == 1, pltpu.roll(circ, 1 << bit, 1), circ)
        far = jnp.broadcast_to(far_ref[h], (ATTN_SUB, ATTN_SUB))
        for t in range(ATTN_WIN // ATTN_SUB):
            key = row + t * ATTN_SUB
            kc, qc = key // CHUNK, col // CHUNK
            band = (kc >= qc) & (kc <= qc + N_PAST_CHUNKS)
            val = jnp.where(key - col <= ATTN_WIN // 2, far, circ)
            bias_ref[h, t * ATTN_SUB:(t + 1) * ATTN_SUB, :] = jnp.where(band, val, NEG_INF)


def _attn_prompt_kernel(q_ref, kp_ref, kc_ref, vp_ref, vc_ref, near_ref, far_ref, o_ref,
                        bias_ref, s_refs, p_refs):
    i = pl.program_id(1)

    @pl.when((pl.program_id(0) == 0) & (i == 0))
    def _():
        _build_prompt_bias(near_ref, far_ref, bias_ref)
        p_refs[...] = jnp.zeros(p_refs.shape, BF16)

    n_before_start = jnp.where(i == 0, ATTN_Q_TILE, 0)
    groups = CHUNK // SUBLANES
    lane_tiles = ATTN_SUB // LANES
    live_blocks = [(c, lt) for c in range(ATTN_WIN // CHUNK) for lt in range(lane_tiles)
                   if lt * LANES // CHUNK <= c <= ((lt + 1) * LANES - 1) // CHUNK + N_PAST_CHUNKS]

    subs = ATTN_Q_TILE // ATTN_SUB
    n_units = N_HEADS * subs

    def unit(u):
        return u // subs, (u % subs) * ATTN_SUB, u % ATTN_STAGES

    def scores(u):
        h, lo, slot = unit(u)
        q = q_ref[h, lo:lo + ATTN_SUB, :]
        n_past = ATTN_Q_TILE - lo
        dims = (((1,), (1,)), ((), ()))
        s_refs[slot, 0:n_past, :] = lax.dot_general(kp_ref[h, lo:, :], q, dims,
                                                    preferred_element_type=F32)
        s_refs[slot, n_past:, :] = lax.dot_general(kc_ref[h, 0:ATTN_WIN - n_past, :], q, dims,
                                                   preferred_element_type=F32)

    def block(c, lt):
        return slice(c * CHUNK, (c + 1) * CHUNK), slice(lt * LANES, (lt + 1) * LANES)

    def biased_max(u):
        h, lo, slot = unit(u)
        s_ref = s_refs.at[slot]
        m8 = [jnp.full((SUBLANES, LANES), NEG_INF, F32) for _ in range(lane_tiles)]
        for c, lt in live_blocks:
            rows, cols = block(c, lt)
            before_start = c * CHUNK < n_before_start - lo
            x = s_ref[rows, cols] * (ATTN_SCALE * LOG2_E) + bias_ref[h, rows, cols]
            x = jnp.where(before_start, NEG_INF, x)
            s_ref[rows, cols] = x
            m8[lt] = jnp.maximum(m8[lt], jnp.max(x.reshape(groups, SUBLANES, LANES), axis=0))
        return [jnp.max(v, axis=0, keepdims=True) for v in m8]

    def exponentials(u, m):
        slot = unit(u)[2]
        s_ref, p_ref = s_refs.at[slot], p_refs.at[slot]
        l8 = [jnp.zeros((SUBLANES, LANES), F32) for _ in range(lane_tiles)]
        for c, lt in live_blocks:
            rows, cols = block(c, lt)
            e = jnp.exp2(s_ref[rows, cols] - m[lt])
            p_ref[rows, cols] = e.astype(BF16)
            l8[lt] = l8[lt] + jnp.sum(e.reshape(groups, SUBLANES, LANES), axis=0)
        return 1.0 / jnp.concatenate([jnp.sum(v, axis=0, keepdims=True) for v in l8], axis=1)

    def values(u, inv):
        h, lo, slot = unit(u)
        p_ref = p_refs.at[slot]
        n_past = ATTN_Q_TILE - lo
        o = jnp.dot(vp_ref[h, :, lo:], p_ref[0:n_past, :], preferred_element_type=F32)
        o = o + jnp.dot(vc_ref[h, :, 0:ATTN_WIN - n_past], p_ref[n_past:, :],
                        preferred_element_type=F32)
        o_ref[h, :, lo:lo + ATTN_SUB] = (o * inv).astype(BF16)

    m, inv = {}, {}
    for r in range(n_units + ATTN_STAGES - 1):
        if 0 <= r - 3:
            values(r - 3, inv.pop(r - 3))
        if 0 <= r - 2 < n_units:
            inv[r - 2] = exponentials(r - 2, m.pop(r - 2))
        if 0 <= r - 1 < n_units:
            m[r - 1] = biased_max(r - 1)
        if r < n_units:
            scores(r)


def _prompt_bias_rows(rel_bias):
    assert ATTN_SUB == 2 * MAX_REL
    k_mod = (-jnp.arange(ATTN_SUB)) % ATTN_SUB
    idx = 2 * MAX_REL - (k_mod - ATTN_WIN // 2) % ATTN_SUB
    near = rel_bias[:, None, idx].astype(F32) * LOG2_E
    far = jnp.broadcast_to(rel_bias[:, None, 2 * MAX_REL:].astype(F32) * LOG2_E, near.shape)
    return near, far


def _attn_prompt(q16, k16, v16t, rel_bias, n_seq, seq_len):
    tiles = seq_len // ATTN_Q_TILE
    cur = lambda b, i: b * tiles + i
    past = lambda b, i: b * tiles + jnp.maximum(i - 1, 0)
    rows = lambda at: pl.BlockSpec((N_HEADS, ATTN_Q_TILE, HEAD_DIM), lambda b, i: (0, at(b, i), 0))
    cols = lambda at: pl.BlockSpec((N_HEADS, HEAD_DIM, ATTN_Q_TILE), lambda b, i: (0, 0, at(b, i)))
    near, far = _prompt_bias_rows(rel_bias)
    bias_row = pl.BlockSpec(near.shape, lambda b, i: (0, 0, 0))
    return pl.pallas_call(
        _attn_prompt_kernel,
        out_shape=jax.ShapeDtypeStruct(v16t.shape, BF16),
        grid=(n_seq, tiles),
        in_specs=[rows(cur), rows(past), rows(cur), cols(past), cols(cur), bias_row, bias_row],
        out_specs=cols(cur),
        scratch_shapes=[pltpu.VMEM((N_HEADS, ATTN_WIN, ATTN_SUB), F32),
                        pltpu.VMEM((ATTN_STAGES, ATTN_WIN, ATTN_SUB), F32),
                        pltpu.VMEM((ATTN_STAGES, ATTN_WIN, ATTN_SUB), BF16)],
        compiler_params=_params(("arbitrary", "arbitrary")),
        name="attn_prompt",
    )(q16, k16, k16, v16t, v16t, near, far)


def _attn_sample_kernel(q_ref, kn_ref, vn_ref, ck_ref, cv_ref, bc_ref, bn_ref, o_ref):
    cache_len = ck_ref.shape[1] // N_HEADS
    for h in range(N_HEADS):
        q = q_ref[h]
        kc = ck_ref[0, pl.ds(h, cache_len, stride=N_HEADS), :].astype(BF16)
        vc = cv_ref[0, pl.ds(h, cache_len, stride=N_HEADS), :].astype(BF16)
        dims = (((1,), (1,)), ((), ()))
        sc = lax.dot_general(q, kc, dims, preferred_element_type=F32) * ATTN_SCALE + bc_ref[h]
        sn = lax.dot_general(q, kn_ref[h], dims, preferred_element_type=F32) * ATTN_SCALE + bn_ref[h]
        m = jnp.maximum(jnp.max(sc, axis=-1, keepdims=True), jnp.max(sn, axis=-1, keepdims=True))
        ec = jnp.exp(sc - m)
        en = jnp.exp(sn - m)
        inv = 1.0 / (jnp.sum(ec, axis=-1, keepdims=True) + jnp.sum(en, axis=-1, keepdims=True))
        o = jnp.dot((ec * inv).astype(BF16), vc, preferred_element_type=F32)
        o = o + jnp.dot((en * inv).astype(BF16), vn_ref[h], preferred_element_type=F32)
        o_ref[h] = o.astype(BF16)


def _attn_sample(q16, k16, v16, cache_k, cache_v, rel_bias, n_seq, seq_len):
    cache_len = cache_k.shape[1]
    ck = cache_k.reshape(n_seq, cache_len * N_HEADS, HEAD_DIM)
    cv = cache_v.reshape(n_seq, cache_len * N_HEADS, HEAD_DIM)
    n_far = max(cache_len - MAX_REL, 0)
    a = jnp.arange(seq_len)[:, None]
    w = jnp.arange(n_far, cache_len + seq_len)[None, :]
    idx = jnp.clip(a - w + cache_len, -MAX_REL, MAX_REL) + MAX_REL
    near = rel_bias[:, idx].astype(F32)
    far = jnp.broadcast_to(rel_bias[:, None, 2 * MAX_REL:].astype(F32), (N_HEADS, seq_len, n_far))
    bias_c = jnp.concatenate([far, near[:, :, :cache_len - n_far]], axis=2)
    bias_n = near[:, :, cache_len - n_far:]
    new = pl.BlockSpec((N_HEADS, seq_len, HEAD_DIM), lambda b: (0, b, 0))
    cache = pl.BlockSpec((1, cache_len * N_HEADS, HEAD_DIM), lambda b: (b, 0, 0))
    return pl.pallas_call(
        _attn_sample_kernel,
        out_shape=jax.ShapeDtypeStruct(q16.shape, BF16),
        grid=(n_seq,),
        in_specs=[new, new, new, cache, cache,
                  pl.BlockSpec(bias_c.shape, lambda b: (0, 0, 0)),
                  pl.BlockSpec(bias_n.shape, lambda b: (0, 0, 0))],
        out_specs=new,
        compiler_params=_params(("parallel",)),
        name="attn_sample",
    )(q16, k16, v16, ck, cv, bias_c, bias_n)


def kernel(x_prompt, x_sample, state_conv, cache_k, cache_v, ffn_norm, ffn_w_gate, ffn_w_up,
           ffn_w_down, mix_norm, conv_w_in, conv_w, conv_w_out, kv_norm, w_kv, k_gain,
           w_q, q_gain, rel_bias, w_o):
    depth = ffn_norm.shape[0]
    assert depth == 2 and conv_w_in.shape[0] == 1 and w_q.shape[0] == 1
    d = D_MODEL
    n_p, t_p, _ = x_prompt.shape
    n_s, t_s, _ = x_sample.shape
    m_p, m_s = n_p * t_p, n_s * t_s
    m_all = m_p + m_s
    tm_p, tm_f = PROMPT_ROW_TILE, FFN_ROW_TILE
    assert m_all % tm_f == 0 and m_s < tm_f and m_p % m_s == 0 and m_p % tm_p == 0
    ffn_tiles = m_all // tm_f
    s_first = m_p // m_s

    ffn_norm4 = ffn_norm.reshape(depth, 2, 1, d)
    mix_norm3 = mix_norm.reshape(depth, 1, d)
    kv_norm2 = kv_norm.reshape(1, d)
    k_gain2 = k_gain.reshape(1, HEAD_DIM)
    q_gain3 = q_gain.reshape(-1, 1, HEAD_DIM)

    def ffn(tail, main, layer, slot, final=False):
        w32 = [_weight(w, (layer, slot)) for w in (ffn_w_gate, ffn_w_up, ffn_w_down)]
        rows = m_p if final else m_all
        in_place = tail.arr is main.arr and tail.arr.shape[0] == rows
        *ys, w16 = _ffn(tail, 1, Dest(rows, ffn_tiles - 1, IN_PLACE if in_place else None),
                        ffn_norm4, (layer, slot), *w32, tm_f, n_split=m_s if final else 0)
        main = Rows(ys[0], 0) if in_place else main
        h, _ = _ffn(main, ffn_tiles - 1, Dest(rows, 0, IN_PLACE if in_place else ys[0]),
                    ffn_norm4, (layer, slot), *w16, tm_f)
        return (h, ys[1]) if final else h

    def both(h):
        return Rows(h, ffn_tiles - 1), Rows(h, 0)

    xp = x_prompt.reshape(m_p, d)
    x_tail = jnp.concatenate([xp[m_p - (tm_f - m_s):], x_sample.reshape(m_s, d)], axis=0)
    h = ffn(Rows(x_tail, 0), Rows(xp, 0), 0, 0)

    mix_blocks = d // MIX_COL_TILE
    w_mix32 = (_weight(conv_w_in, (0,)), _weight(conv_w_in, (0,), mix_blocks),
               _weight(conv_w_in, (0,), 2 * mix_blocks))
    h, conv_s, w_mix16 = _mixer(Rows(h, s_first), m_s, Dest(m_all, s_first, IN_PLACE), mix_norm3,
                                (0,), *w_mix32, conv_w, (0,), _weight(conv_w_out, (0,)),
                                state_conv[0], t_s, m_s, MIX_COL_TILE)
    conv_zero = jnp.zeros((n_p, CONV_WIDTH - 1, d), F32)
    h, conv_p, _ = _mixer(Rows(h, 0), m_p, Dest(m_all, 0, IN_PLACE), mix_norm3, (0,),
                          *w_mix16[:3], conv_w, (0,), w_mix16[3], conv_zero, t_p,
                          PROMPT_MIX_ROW_TILE, PROMPT_MIX_COL_TILE, PROMPT_MIX_PARTS)

    h = ffn(*both(h), 0, 1)

    proj_blocks = d // CAST_COL_TILE
    (k_s, v_s, k16_s, v16_s), w_kv16 = _kv_proj(
        Rows(h, s_first), m_s, kv_norm2, _weight(w_kv), _weight(w_kv, (), proj_blocks), k_gain2,
        t_s, m_s, v_transposed=False)
    (k_p, v_p, k16_p, v16t_p), _ = _kv_proj(
        Rows(h, 0), m_p, kv_norm2, *w_kv16, k_gain2, t_p, tm_p, v_transposed=True,
        n_parts=PROMPT_PROJ_PARTS)

    h = ffn(*both(h), 1, 0)

    q16_s, w_q16 = _q_proj(Rows(h, s_first), m_s, mix_norm3, (1,), _weight(w_q, (0,)),
                           q_gain3, (0,), m_s)
    q16_p, _ = _q_proj(Rows(h, 0), m_p, mix_norm3, (1,), *w_q16, q_gain3, (0,), PROMPT_QO_ROW_TILE)
    att_s = _attn_sample(q16_s, k16_s, v16_s, cache_k, cache_v, rel_bias[0], n_s, t_s)
    att_p = _attn_prompt(q16_p, k16_p, v16t_p, rel_bias[0], n_p, t_p)
    h, w_o16 = _o_proj(Rows(h, s_first), m_s, Dest(m_all, s_first, IN_PLACE), att_s,
                       _weight(w_o, (0,)), m_s, att_transposed=False)
    h, _ = _o_proj(Rows(h, 0), m_p, Dest(m_all, 0, IN_PLACE), att_p, *w_o16, PROMPT_QO_ROW_TILE,
                   att_transposed=True)

    y_p, y_s = ffn(*both(h), 1, 1, final=True)

    keep_p, keep_s = min(PAST_ROWS, t_p), min(PAST_ROWS, t_s)
    heads = (N_HEADS, HEAD_DIM)
    return (y_p.reshape(n_p, t_p, d), y_s.reshape(n_s, t_s, d), conv_p[None],
            k_p.reshape(n_p, keep_p, *heads), v_p.reshape(n_p, keep_p, *heads), conv_s[None],
            k_s.reshape(n_s, keep_s, *heads), v_s.reshape(n_s, keep_s, *heads))
```

```python
import collections
import functools

import jax
import jax.numpy as jnp
from jax import lax
from jax.experimental import pallas as pl
from jax.experimental.pallas import tpu as pltpu

F32 = jnp.float32
BF16 = jnp.bfloat16

D_MODEL = 2048
HEAD_DIM = 128
N_HEADS = D_MODEL // HEAD_DIM
CHUNK = 64
N_PAST_CHUNKS = 8
PAST_ROWS = N_PAST_CHUNKS * CHUNK
MAX_REL = 128
CONV_WIDTH = 3
EPS = 1e-6
FFN_RES = 0.5
NEG_INF = -1e30
ATTN_SCALE = HEAD_DIM ** -0.5
LOG2_E = 1.4426950408889634

SUBLANES = 8
LANES = 128
V7X_VMEM_BYTES = 64 * 1024 * 1024
VMEM_RESERVE_BYTES = 8 * 1024 * 1024
VMEM_LIMIT_BYTES = V7X_VMEM_BYTES - VMEM_RESERVE_BYTES

PROMPT_ROW_TILE = 512
PROMPT_QO_ROW_TILE = 1024
PROMPT_PROJ_PARTS = 2
FFN_ROW_TILE = 1040
FFN_COL_TILE = 512
FFN_CAST_COL_TILE = 256
MIX_COL_TILE = 512
PROMPT_MIX_ROW_TILE = 1024
PROMPT_MIX_PARTS = 2
PROMPT_MIX_COL_TILE = 256
KV_COL_TILE = 1024
CAST_COL_TILE = 512
ATTN_Q_TILE = 512
ATTN_SUB = 256
ATTN_WIN = ATTN_SUB + PAST_ROWS
ATTN_STAGES = 4

Weight = collections.namedtuple("Weight", ["arr", "lead", "col_off", "col_tiled"])
Rows = collections.namedtuple("Rows", ["arr", "first"])
Dest = collections.namedtuple("Dest", ["rows", "first", "alias"])
IN_PLACE = "in place"


def _weight(arr, lead=(), col_off=0, col_tiled=False):
    return Weight(arr, tuple(lead), col_off, col_tiled)


def _wspec(w, block, index, **mode):
    lead = w.lead

    def index_map(*ids):
        r, c = index(*ids)
        return lead + (r, c + w.col_off)

    return pl.BlockSpec((None,) * len(lead) + block, index_map, **mode)


def _resident(n_blocks):
    return dict(pipeline_mode=pl.Buffered(1)) if n_blocks == 1 else {}


def _col_spec(w, rows, tn, col):
    if w.col_tiled:
        assert w.arr.shape[1:] == (rows, tn)
        return pl.BlockSpec((None, rows, tn), lambda *ids: (col(*ids), 0, 0))
    return _wspec(w, (rows, tn), lambda *ids: (0, col(*ids)))


def _lead_spec(lead, block):
    lead = tuple(lead)
    return pl.BlockSpec((None,) * len(lead) + block, lambda *ids: lead + (0,) * len(block))


def _is_f32(w):
    return w.arr.dtype == F32


def _bf16_weight(w_ref, w16_ref):
    if w16_ref is None:
        return w_ref[...]
    w16 = w_ref[...].astype(BF16)
    w16_ref[...] = w16
    return w16


def _dest_args(src, dest):
    if dest.alias is None:
        return [], [], {}
    if dest.alias is IN_PLACE:
        assert src.arr.shape[0] == dest.rows and src.first == dest.first
        return [], [], {0: 0}
    return [pl.BlockSpec(memory_space=pl.ANY)], [dest.alias], {0: 0}


def _params(semantics):
    return pltpu.CompilerParams(dimension_semantics=semantics,
                                vmem_limit_bytes=VMEM_LIMIT_BYTES)


def _rms_rows(x, g):
    ms = jnp.mean(x * x, axis=-1, keepdims=True)
    return (x * lax.rsqrt(ms + EPS)) * g


def _ffn_kernel(*refs, emit, aliased, n_split):
    h_ref, g_ref, wg_ref, wu_ref, wd_ref, out_ref, *rest = refs[aliased:]
    split_ref = rest.pop(0) if n_split else None
    wg16_ref, wu16_ref, wd16_ref = rest[:3] if emit else (None, None, None)
    xn_ref = rest[-1]
    f = pl.program_id(1)

    @pl.when(f == 0)
    def _():
        h = h_ref[...]
        xn_ref[...] = _rms_rows(h, g_ref[...]).astype(BF16)
        out_ref[...] = h

    xn = xn_ref[...]
    gate = jnp.dot(xn, _bf16_weight(wg_ref, wg16_ref), preferred_element_type=F32)
    up = jnp.dot(xn, _bf16_weight(wu_ref, wu16_ref), preferred_element_type=F32)
    act = (gate * jax.nn.sigmoid(gate) * (FFN_RES * up)).astype(BF16)
    out_ref[...] += jnp.dot(act, _bf16_weight(wd_ref, wd16_ref), preferred_element_type=F32)

    if n_split:
        @pl.when(f == pl.num_programs(1) - 1)
        def _():
            split_ref[...] = out_ref[out_ref.shape[0] - n_split:, :]


def _ffn(src, n_tiles, dest, norm_g, g_lead, wg, wu, wd, tm, n_split=0):
    d = src.arr.shape[1]
    d_ff = wd.arr.shape[-2]
    emit = _is_f32(wg)
    tf = FFN_CAST_COL_TILE if emit else FFN_COL_TILE
    out_shape = [jax.ShapeDtypeStruct((dest.rows, d), F32)]
    out_specs = [pl.BlockSpec((tm, d), lambda i, f: (dest.first + i, 0))]
    if n_split:
        assert n_tiles == 1
        out_shape += [jax.ShapeDtypeStruct((n_split, d), F32)]
        out_specs += [pl.BlockSpec((n_split, d), lambda i, f: (0, 0))]
    h_mode = {}
    if emit:
        assert n_tiles == 1
        per = FFN_COL_TILE // tf
        tiled = jax.ShapeDtypeStruct((d_ff // FFN_COL_TILE, d, FFN_COL_TILE), BF16)
        out_shape += [tiled, tiled, jax.ShapeDtypeStruct((d_ff, d), BF16)]
        out_specs += [pl.BlockSpec((None, d, tf), lambda i, f: (f // per, 0, f % per)),
                      pl.BlockSpec((None, d, tf), lambda i, f: (f // per, 0, f % per)),
                      pl.BlockSpec((tf, d), lambda i, f: (f, 0))]
        h_mode = dict(pipeline_mode=pl.Buffered(1))
    alias_specs, alias_args, aliases = _dest_args(src, dest)
    res = pl.pallas_call(
        functools.partial(_ffn_kernel, emit=emit, aliased=len(alias_args), n_split=n_split),
        out_shape=out_shape,
        grid=(n_tiles, d_ff // tf),
        in_specs=alias_specs + [
            pl.BlockSpec((tm, d), lambda i, f: (src.first + i, 0), **h_mode),
            _lead_spec(g_lead, (1, d)),
            _col_spec(wg, d, tf, lambda i, f: f),
            _col_spec(wu, d, tf, lambda i, f: f),
            _wspec(wd, (tf, d), lambda i, f: (f, 0)),
        ],
        out_specs=out_specs,
        scratch_shapes=[pltpu.VMEM((tm, d), BF16)],
        input_output_aliases=aliases,
        compiler_params=_params(("parallel", "arbitrary")),
        name="ffn",
    )(*alias_args, src.arr, norm_g, wg.arr, wu.arr, wd.arr)
    n_act = 2 if n_split else 1
    w16 = tuple(_weight(a, col_tiled=a.ndim == 3) for a in res[n_act:])
    return tuple(res[:n_act]) + (w16,)


def _mixer_kernel(*refs, seq_len, tm, emit, aliased, n_parts):
    (h_ref, g_ref, wb_ref, wc_ref, wx_ref, cw_ref, wo_ref, p0_ref, p1_ref,
     out_ref, st_ref, *rest) = refs[aliased:]
    wb16_ref, wc16_ref, wx16_ref, wo16_ref = rest[:4] if emit else (None,) * 4
    xn_ref, carry_ref = rest[-2:]
    i = pl.program_id(0)
    j = pl.program_id(1)

    @pl.when(j == 0)
    def _():
        h = h_ref[...]
        xn_ref[...] = _rms_rows(h, g_ref[...]).astype(BF16)
        out_ref[...] = h

    if seq_len >= tm:
        @pl.when((i % (seq_len // tm)) == 0)
        def _():
            carry_ref[j] = p0_ref[0]

    wb, wc, wx, wo = (_bf16_weight(wb_ref, wb16_ref), _bf16_weight(wc_ref, wc16_ref),
                      _bf16_weight(wx_ref, wx16_ref), _bf16_weight(wo_ref, wo16_ref))
    cw = cw_ref[...]
    tn = cw.shape[1]
    hm = tm // n_parts
    bs, us = [], []
    for part in range(n_parts):
        xn = xn_ref[part * hm:(part + 1) * hm, :]
        bs.append(jnp.dot(xn, wb, preferred_element_type=F32))
        c = jnp.dot(xn, wc, preferred_element_type=F32)
        x = jnp.dot(xn, wx, preferred_element_type=F32)
        us.append(c * x)
    row = lax.broadcasted_iota(jnp.int32, (hm, tn), 0)
    for part in range(n_parts):
        u = us[part]
        if seq_len >= tm:
            prev = carry_ref[j] if part == 0 else us[part - 1][hm - 2:hm, :]
            p0 = prev[0:1, :]
            p1 = prev[1:2, :]
            pos = row
        else:
            p0 = p0_ref[...]
            p1 = p1_ref[...]
            pos = row % seq_len
        um1 = jnp.where(pos == 0, p1, pltpu.roll(u, 1, 0))
        um2 = jnp.where(pos == 0, p0, jnp.where(pos == 1, p1, pltpu.roll(u, 2, 0)))
        conv = cw[0:1, :] * um2 + cw[1:2, :] * um1 + cw[2:3, :] * u
        v = (bs[part] * conv).astype(BF16)
        out_ref[part * hm:(part + 1) * hm, :] += jnp.dot(v, wo, preferred_element_type=F32)
    if seq_len >= tm:
        carry_ref[j] = us[-1][hm - 2:hm, :]
        st_ref[0] = us[-1][hm - 2:hm, :]
    else:
        st_ref[...] = us[0].reshape(tm // seq_len, seq_len, tn)[:, seq_len - 2:seq_len, :]


def _mixer(src, rows, dest, norm_g, g_lead, wb, wc, wx, conv_w, cw_lead, wo, state, seq_len, tm,
           tn, n_parts=1):
    d = src.arr.shape[1]
    n_seq = rows // seq_len
    n_j = d // tn
    n_i = rows // tm
    emit = _is_f32(wb)
    if seq_len >= tm:
        tiles_per_seq = seq_len // tm
        p0, p1 = state, state
        p_spec = pl.BlockSpec((1, CONV_WIDTH - 1, tn), lambda i, j: (i // tiles_per_seq, 0, j))
        st_shape = jax.ShapeDtypeStruct((n_i, CONV_WIDTH - 1, d), F32)
        st_spec = pl.BlockSpec((1, CONV_WIDTH - 1, tn), lambda i, j: (i, 0, j))
    else:
        assert tm == rows and tm % seq_len == 0 and n_parts == 1
        p0 = jnp.repeat(state[:, 0], seq_len, axis=0)
        p1 = jnp.repeat(state[:, 1], seq_len, axis=0)
        p_spec = pl.BlockSpec((tm, tn), lambda i, j: (i, j))
        st_shape = jax.ShapeDtypeStruct((n_seq, CONV_WIDTH - 1, d), F32)
        st_spec = pl.BlockSpec((n_seq, CONV_WIDTH - 1, tn), lambda i, j: (0, 0, j))
    out_shape = [jax.ShapeDtypeStruct((dest.rows, d), F32), st_shape]
    out_specs = [pl.BlockSpec((tm, d), lambda i, j: (dest.first + i, 0)), st_spec]
    if emit:
        assert n_i == 1
        out_shape += [jax.ShapeDtypeStruct((d, d), BF16)] * 4
        out_specs += [pl.BlockSpec((d, tn), lambda i, j: (0, j))] * 3
        out_specs += [pl.BlockSpec((tn, d), lambda i, j: (j, 0))]
    alias_specs, alias_args, aliases = _dest_args(src, dest)
    res = pl.pallas_call(
        functools.partial(_mixer_kernel, seq_len=seq_len, tm=tm, emit=emit,
                          aliased=len(alias_args), n_parts=n_parts),
        out_shape=out_shape,
        grid=(n_i, n_j),
        in_specs=alias_specs + [
            pl.BlockSpec((tm, d), lambda i, j: (src.first + i, 0)),
            _lead_spec(g_lead, (1, d)),
            _wspec(wb, (d, tn), lambda i, j: (0, j)),
            _wspec(wc, (d, tn), lambda i, j: (0, j)),
            _wspec(wx, (d, tn), lambda i, j: (0, j)),
            pl.BlockSpec((None,) * len(cw_lead) + (CONV_WIDTH, tn),
                         lambda i, j: tuple(cw_lead) + (0, j)),
            _wspec(wo, (tn, d), lambda i, j: (j, 0)),
            p_spec,
            p_spec,
        ],
        out_specs=out_specs,
        scratch_shapes=[pltpu.VMEM((tm, d), BF16),
                        pltpu.VMEM((n_j, CONV_WIDTH - 1, tn), F32)],
        input_output_aliases=aliases,
        compiler_params=_params(("arbitrary", "arbitrary")),
        name="conv_mixer",
    )(*alias_args, src.arr, norm_g, wb.arr, wc.arr, wx.arr, conv_w, wo.arr, p0, p1)
    out, st = res[0], res[1]
    if seq_len >= tm:
        st = st[seq_len // tm - 1::seq_len // tm]
    return out, st, tuple(_weight(a) for a in res[2:])


def _kv_kernel(h_ref, g_ref, wk_ref, wv_ref, kg_ref, k32_ref, v32_ref, k16_ref, v16_ref,
               *rest, emit, n_parts, v_transposed):
    wk16_ref, wv16_ref = rest if emit else (None, None)
    wk, wv = _bf16_weight(wk_ref, wk16_ref), _bf16_weight(wv_ref, wv16_ref)
    g, kg = g_ref[...], kg_ref[...]
    hm = h_ref.shape[0] // n_parts
    ks, vs = [], []
    for part in range(n_parts):
        xn = _rms_rows(h_ref[part * hm:(part + 1) * hm, :], g).astype(BF16)
        ks.append(jnp.dot(xn, wk, preferred_element_type=F32))
        vs.append(jnp.dot(xn, wv, preferred_element_type=F32))
    for part, (k, v) in enumerate(zip(ks, vs)):
        rows = slice(part * hm, (part + 1) * hm)
        v32_ref[rows, :] = v
        for hd in range(k.shape[1] // HEAD_DIM):
            cols = slice(hd * HEAD_DIM, (hd + 1) * HEAD_DIM)
            kh = _rms_rows(k[:, cols], kg)
            k32_ref[rows, cols] = kh
            k16_ref[hd, rows, :] = kh.astype(BF16)
            if v_transposed:
                v16_ref[hd, :, rows] = v[:, cols].T.astype(BF16)
            else:
                v16_ref[hd, rows, :] = v[:, cols].astype(BF16)


def _kv_proj(src, rows, norm_g, wk, wv, k_gain, seq_len, tm, v_transposed, n_parts=1):
    d = src.arr.shape[1]
    emit = _is_f32(wk)
    tn = CAST_COL_TILE if emit else KV_COL_TILE
    hpb = tn // HEAD_DIM
    if seq_len >= tm:
        assert tm == min(PAST_ROWS, seq_len) and seq_len % tm == 0
        tiles_per_seq = seq_len // tm
        kept_rows = (rows // seq_len) * tm
        kept_spec = pl.BlockSpec((tm, tn), lambda j, i: (i // tiles_per_seq, j))
    else:
        assert tm % seq_len == 0 and seq_len <= PAST_ROWS
        kept_rows = rows
        kept_spec = pl.BlockSpec((tm, tn), lambda j, i: (i, j))
    if v_transposed:
        v16_shape = jax.ShapeDtypeStruct((N_HEADS, HEAD_DIM, rows), BF16)
        v16_spec = pl.BlockSpec((hpb, HEAD_DIM, tm), lambda j, i: (j, 0, i))
    else:
        v16_shape = jax.ShapeDtypeStruct((N_HEADS, rows, HEAD_DIM), BF16)
        v16_spec = pl.BlockSpec((hpb, tm, HEAD_DIM), lambda j, i: (j, i, 0))
    out_shape = [jax.ShapeDtypeStruct((kept_rows, d), F32),
                 jax.ShapeDtypeStruct((kept_rows, d), F32),
                 jax.ShapeDtypeStruct((N_HEADS, rows, HEAD_DIM), BF16), v16_shape]
    out_specs = [kept_spec, kept_spec,
                 pl.BlockSpec((hpb, tm, HEAD_DIM), lambda j, i: (j, i, 0)), v16_spec]
    if emit:
        assert rows == tm
        out_shape += [jax.ShapeDtypeStruct((d, d), BF16)] * 2
        out_specs += [pl.BlockSpec((d, tn), lambda j, i: (0, j))] * 2
    res = pl.pallas_call(
        functools.partial(_kv_kernel, emit=emit, n_parts=n_parts, v_transposed=v_transposed),
        out_shape=out_shape,
        grid=(d // tn, rows // tm),
        in_specs=[
            pl.BlockSpec((tm, d), lambda j, i: (src.first + i, 0)),
            pl.BlockSpec((1, d), lambda j, i: (0, 0)),
            _wspec(wk, (d, tn), lambda j, i: (0, j)),
            _wspec(wv, (d, tn), lambda j, i: (0, j)),
            pl.BlockSpec((1, HEAD_DIM), lambda j, i: (0, 0)),
        ],
        out_specs=out_specs,
        compiler_params=_params(("arbitrary", "arbitrary")),
        name="kv_proj",
    )(src.arr, norm_g, wk.arr, wv.arr, k_gain)
    return res[:4], tuple(_weight(a) for a in res[4:])


def _q_kernel(h_ref, g_ref, wq_ref, qg_ref, q16_ref, *rest, emit):
    wq16_ref = rest[0] if emit else None
    xn_ref = rest[-1]

    @pl.when(pl.program_id(1) == 0)
    def _():
        xn_ref[...] = _rms_rows(h_ref[...], g_ref[...]).astype(BF16)

    q = jnp.dot(xn_ref[...], _bf16_weight(wq_ref, wq16_ref), preferred_element_type=F32)
    qg = qg_ref[...]
    for hd in range(q.shape[1] // HEAD_DIM):
        cols = slice(hd * HEAD_DIM, (hd + 1) * HEAD_DIM)
        q16_ref[hd] = _rms_rows(q[:, cols], qg).astype(BF16)


def _q_proj(src, rows, norm_g, g_lead, wq, q_gain, qg_lead, tm):
    d = src.arr.shape[1]
    emit = _is_f32(wq)
    tn = CAST_COL_TILE if emit else d
    hpb = tn // HEAD_DIM
    out_shape = [jax.ShapeDtypeStruct((N_HEADS, rows, HEAD_DIM), BF16)]
    out_specs = [pl.BlockSpec((hpb, tm, HEAD_DIM), lambda i, j: (j, i, 0))]
    if emit:
        assert rows == tm
        out_shape += [jax.ShapeDtypeStruct((d, d), BF16)]
        out_specs += [pl.BlockSpec((d, tn), lambda i, j: (0, j))]
    res = pl.pallas_call(
        functools.partial(_q_kernel, emit=emit),
        out_shape=out_shape,
        grid=(rows // tm, d // tn),
        in_specs=[
            pl.BlockSpec((tm, d), lambda i, j: (src.first + i, 0)),
            _lead_spec(g_lead, (1, d)),
            _wspec(wq, (d, tn), lambda i, j: (0, j), **_resident(d // tn)),
            _lead_spec(qg_lead, (1, HEAD_DIM)),
        ],
        out_specs=out_specs,
        scratch_shapes=[pltpu.VMEM((tm, d), BF16)],
        compiler_params=_params(("parallel", "arbitrary")),
        name="q_proj",
    )(src.arr, norm_g, wq.arr, q_gain)
    return res[0], tuple(_weight(a) for a in res[1:])


def _o_kernel(*refs, emit, att_transposed, aliased):
    h_ref, att_ref, wo_ref, out_ref, *rest = refs[aliased:]
    wo16_ref = rest[0] if emit else None
    wo = _bf16_weight(wo_ref, wo16_ref)
    if att_transposed:
        att_t = att_ref[...].reshape(D_MODEL, att_ref.shape[2])
        y = lax.dot_general(att_t, wo, (((0,), (0,)), ((), ())), preferred_element_type=F32)
    else:
        att2d_ref = rest[-1]

        @pl.when(pl.program_id(1) == 0)
        def _():
            for hd in range(N_HEADS):
                att2d_ref[:, hd * HEAD_DIM:(hd + 1) * HEAD_DIM] = att_ref[hd]

        y = jnp.dot(att2d_ref[...], wo, preferred_element_type=F32)
    out_ref[...] = h_ref[...] + y


def _o_proj(src, rows, dest, att16, wo, tm, att_transposed):
    d = src.arr.shape[1]
    emit = _is_f32(wo)
    tn = CAST_COL_TILE if emit else d
    out_shape = [jax.ShapeDtypeStruct((dest.rows, d), F32)]
    out_specs = [pl.BlockSpec((tm, tn), lambda i, j: (dest.first + i, j))]
    if emit:
        assert rows == tm
        out_shape += [jax.ShapeDtypeStruct((d, d), BF16)]
        out_specs += [pl.BlockSpec((d, tn), lambda i, j: (0, j))]
    if att_transposed:
        att_spec = pl.BlockSpec((N_HEADS, HEAD_DIM, tm), lambda i, j: (0, 0, i))
        scratch = []
    else:
        att_spec = pl.BlockSpec((N_HEADS, tm, HEAD_DIM), lambda i, j: (0, i, 0))
        scratch = [pltpu.VMEM((tm, d), BF16)]
    alias_specs, alias_args, aliases = _dest_args(src, dest)
    res = pl.pallas_call(
        functools.partial(_o_kernel, emit=emit, att_transposed=att_transposed,
                          aliased=len(alias_args)),
        out_shape=out_shape,
        grid=(rows // tm, d // tn),
        in_specs=alias_specs + [
            pl.BlockSpec((tm, tn), lambda i, j: (src.first + i, j)),
            att_spec,
            _wspec(wo, (d, tn), lambda i, j: (0, j), **_resident(d // tn)),
        ],
        out_specs=out_specs,
        scratch_shapes=scratch,
        input_output_aliases=aliases,
        compiler_params=_params(("parallel", "arbitrary")),
        name="o_proj",
    )(*alias_args, src.arr, att16, wo.arr)
    return res[0], tuple(_weight(a) for a in res[1:])


def _build_prompt_bias(near_ref, far_ref, bias_ref):
    assert ATTN_WIN == 3 * ATTN_SUB and PAST_ROWS - MAX_REL == ATTN_WIN // 2
    row = lax.broadcasted_iota(jnp.int32, (ATTN_SUB, ATTN_SUB), 0)
    col = lax.broadcasted_iota(jnp.int32, (ATTN_SUB, ATTN_SUB), 1)
    for h in range(N_HEADS):
        circ = jnp.broadcast_to(near_ref[h], (ATTN_SUB, ATTN_SUB))
        for bit in range(ATTN_SUB.bit_length() - 1):
            circ = jnp.where((row >> bit) & 1 == 1, pltpu.roll(circ, 1 << bit, 1), circ)
        far = jnp.broadcast_to(far_ref[h], (ATTN_SUB, ATTN_SUB))
        for t in range(ATTN_WIN // ATTN_SUB):
            key = row + t * ATTN_SUB
            kc, qc = key // CHUNK, col // CHUNK
            band = (kc >= qc) & (kc <= qc + N_PAST_CHUNKS)
            val = jnp.where(key - col <= ATTN_WIN // 2, far, circ)
            bias_ref[h, t * ATTN_SUB:(t + 1) * ATTN_SUB, :] = jnp.where(band, val, NEG_INF)


def _attn_prompt_kernel(q_ref, kp_ref, kc_ref, vp_ref, vc_ref, near_ref, far_ref, o_ref,
                        bias_ref, s_refs, p_refs):
    i = pl.program_id(1)

    @pl.when((pl.program_id(0) == 0) & (i == 0))
    def _():
        _build_prompt_bias(near_ref, far_ref, bias_ref)
        p_refs[...] = jnp.zeros(p_refs.shape, BF16)

    n_before_start = jnp.where(i == 0, ATTN_Q_TILE, 0)
    groups = CHUNK // SUBLANES
    lane_tiles = ATTN_SUB // LANES
    live_blocks = [(c, lt) for c in range(ATTN_WIN // CHUNK) for lt in range(lane_tiles)
                   if lt * LANES // CHUNK <= c <= ((lt + 1) * LANES - 1) // CHUNK + N_PAST_CHUNKS]

    subs = ATTN_Q_TILE // ATTN_SUB
    n_units = N_HEADS * subs

    def unit(u):
        return u // subs, (u % subs) * ATTN_SUB, u % ATTN_STAGES

    def scores(u):
        h, lo, slot = unit(u)
        q = q_ref[h, lo:lo + ATTN_SUB, :]
        n_past = ATTN_Q_TILE - lo
        dims = (((1,), (1,)), ((), ()))
        s_refs[slot, 0:n_past, :] = lax.dot_general(kp_ref[h, lo:, :], q, dims,
                                                    preferred_element_type=F32)
        s_refs[slot, n_past:, :] = lax.dot_general(kc_ref[h, 0:ATTN_WIN - n_past, :], q, dims,
                                                   preferred_element_type=F32)

    def block(c, lt):
        return slice(c * CHUNK, (c + 1) * CHUNK), slice(lt * LANES, (lt + 1) * LANES)

    def biased_max(u):
        h, lo, slot = unit(u)
        s_ref = s_refs.at[slot]
        m8 = [jnp.full((SUBLANES, LANES), NEG_INF, F32) for _ in range(lane_tiles)]
        for c, lt in live_blocks:
            rows, cols = block(c, lt)
            before_start = c * CHUNK < n_before_start - lo
            x = s_ref[rows, cols] * (ATTN_SCALE * LOG2_E) + bias_ref[h, rows, cols]
            x = jnp.where(before_start, NEG_INF, x)
            s_ref[rows, cols] = x
            m8[lt] = jnp.maximum(m8[lt], jnp.max(x.reshape(groups, SUBLANES, LANES), axis=0))
        return [jnp.max(v, axis=0, keepdims=True) for v in m8]

    def exponentials(u, m):
        slot = unit(u)[2]
        s_ref, p_ref = s_refs.at[slot], p_refs.at[slot]
        l8 = [jnp.zeros((SUBLANES, LANES), F32) for _ in range(lane_tiles)]
        for c, lt in live_blocks:
            rows, cols = block(c, lt)
            e = jnp.exp2(s_ref[rows, cols] - m[lt])
            p_ref[rows, cols] = e.astype(BF16)
            l8[lt] = l8[lt] + jnp.sum(e.reshape(groups, SUBLANES, LANES), axis=0)
        return 1.0 / jnp.concatenate([jnp.sum(v, axis=0, keepdims=True) for v in l8], axis=1)

    def values(u, inv):
        h, lo, slot = unit(u)
        p_ref = p_refs.at[slot]
        n_past = ATTN_Q_TILE - lo
        o = jnp.dot(vp_ref[h, :, lo:], p_ref[0:n_past, :], preferred_element_type=F32)
        o = o + jnp.dot(vc_ref[h, :, 0:ATTN_WIN - n_past], p_ref[n_past:, :],
                        preferred_element_type=F32)
        o_ref[h, :, lo:lo + ATTN_SUB] = (o * inv).astype(BF16)

    m, inv = {}, {}
    for r in range(n_units + ATTN_STAGES - 1):
        if 0 <= r - 3:
            values(r - 3, inv.pop(r - 3))
        if 0 <= r - 2 < n_units:
            inv[r - 2] = exponentials(r - 2, m.pop(r - 2))
        if 0 <= r - 1 < n_units:
            m[r - 1] = biased_max(r - 1)
        if r < n_units:
            scores(r)


def _prompt_bias_rows(rel_bias):
    assert ATTN_SUB == 2 * MAX_REL
    k_mod = (-jnp.arange(ATTN_SUB)) % ATTN_SUB
    idx = 2 * MAX_REL - (k_mod - ATTN_WIN // 2) % ATTN_SUB
    near = rel_bias[:, None, idx].astype(F32) * LOG2_E
    far = jnp.broadcast_to(rel_bias[:, None, 2 * MAX_REL:].astype(F32) * LOG2_E, near.shape)
    return near, far


def _attn_prompt(q16, k16, v16t, rel_bias, n_seq, seq_len):
    tiles = seq_len // ATTN_Q_TILE
    cur = lambda b, i: b * tiles + i
    past = lambda b, i: b * tiles + jnp.maximum(i - 1, 0)
    rows = lambda at: pl.BlockSpec((N_HEADS, ATTN_Q_TILE, HEAD_DIM), lambda b, i: (0, at(b, i), 0))
    cols = lambda at: pl.BlockSpec((N_HEADS, HEAD_DIM, ATTN_Q_TILE), lambda b, i: (0, 0, at(b, i)))
    near, far = _prompt_bias_rows(rel_bias)
    bias_row = pl.BlockSpec(near.shape, lambda b, i: (0, 0, 0))
    return pl.pallas_call(
        _attn_prompt_kernel,
        out_shape=jax.ShapeDtypeStruct(v16t.shape, BF16),
        grid=(n_seq, tiles),
        in_specs=[rows(cur), rows(past), rows(cur), cols(past), cols(cur), bias_row, bias_row],
        out_specs=cols(cur),
        scratch_shapes=[pltpu.VMEM((N_HEADS, ATTN_WIN, ATTN_SUB), F32),
                        pltpu.VMEM((ATTN_STAGES, ATTN_WIN, ATTN_SUB), F32),
                        pltpu.VMEM((ATTN_STAGES, ATTN_WIN, ATTN_SUB), BF16)],
        compiler_params=_params(("arbitrary", "arbitrary")),
        name="attn_prompt",
    )(q16, k16, k16, v16t, v16t, near, far)


def _attn_sample_kernel(q_ref, kn_ref, vn_ref, ck_ref, cv_ref, bc_ref, bn_ref, o_ref):
    cache_len = ck_ref.shape[1] // N_HEADS
    dims = (((1,), (1,)), ((), ()))

    def gather(h):
        kc = ck_ref[0, pl.ds(h, cache_len, stride=N_HEADS), :].astype(BF16)
        vc = cv_ref[0, pl.ds(h, cache_len, stride=N_HEADS), :].astype(BF16)
        return kc, vc

    def scores(h, kc):
        q = q_ref[h]
        sc = lax.dot_general(q, kc, dims, preferred_element_type=F32) * ATTN_SCALE + bc_ref[h]
        sn = lax.dot_general(q, kn_ref[h], dims, preferred_element_type=F32) * ATTN_SCALE + bn_ref[h]
        return sc, sn

    def probabilities(sc, sn):
        m = jnp.maximum(jnp.max(sc, axis=-1, keepdims=True), jnp.max(sn, axis=-1, keepdims=True))
        ec = jnp.exp(sc - m)
        en = jnp.exp(sn - m)
        inv = 1.0 / (jnp.sum(ec, axis=-1, keepdims=True) + jnp.sum(en, axis=-1, keepdims=True))
        return (ec * inv).astype(BF16), (en * inv).astype(BF16)

    def values(h, pc, pn, vc):
        o = jnp.dot(pc, vc, preferred_element_type=F32)
        o = o + jnp.dot(pn, vn_ref[h], preferred_element_type=F32)
        o_ref[h] = o.astype(BF16)

    kv, s, p = {}, {}, {}
    for r in range(N_HEADS + ATTN_STAGES - 1):
        if 0 <= r - 3:
            values(r - 3, *p.pop(r - 3), kv.pop(r - 3)[1])
        if 0 <= r - 2 < N_HEADS:
            p[r - 2] = probabilities(*s.pop(r - 2))
        if 0 <= r - 1 < N_HEADS:
            s[r - 1] = scores(r - 1, kv[r - 1][0])
        if r < N_HEADS:
            kv[r] = gather(r)


def _attn_sample(q16, k16, v16, cache_k, cache_v, rel_bias, n_seq, seq_len):
    cache_len = cache_k.shape[1]
    ck = cache_k.reshape(n_seq, cache_len * N_HEADS, HEAD_DIM)
    cv = cache_v.reshape(n_seq, cache_len * N_HEADS, HEAD_DIM)
    n_far = max(cache_len - MAX_REL, 0)
    a = jnp.arange(seq_len)[:, None]
    w = jnp.arange(n_far, cache_len + seq_len)[None, :]
    idx = jnp.clip(a - w + cache_len, -MAX_REL, MAX_REL) + MAX_REL
    near = rel_bias[:, idx].astype(F32)
    far = jnp.broadcast_to(rel_bias[:, None, 2 * MAX_REL:].astype(F32), (N_HEADS, seq_len, n_far))
    bias_c = jnp.concatenate([far, near[:, :, :cache_len - n_far]], axis=2)
    bias_n = near[:, :, cache_len - n_far:]
    new = pl.BlockSpec((N_HEADS, seq_len, HEAD_DIM), lambda b: (0, b, 0))
    cache = pl.BlockSpec((1, cache_len * N_HEADS, HEAD_DIM), lambda b: (b, 0, 0))
    return pl.pallas_call(
        _attn_sample_kernel,
        out_shape=jax.ShapeDtypeStruct(q16.shape, BF16),
        grid=(n_seq,),
        in_specs=[new, new, new, cache, cache,
                  pl.BlockSpec(bias_c.shape, lambda b: (0, 0, 0)),
                  pl.BlockSpec(bias_n.shape, lambda b: (0, 0, 0))],
        out_specs=new,
        compiler_params=_params(("parallel",)),
        name="attn_sample",
    )(q16, k16, v16, ck, cv, bias_c, bias_n)


def kernel(x_prompt, x_sample, state_conv, cache_k, cache_v, ffn_norm, ffn_w_gate, ffn_w_up,
           ffn_w_down, mix_norm, conv_w_in, conv_w, conv_w_out, kv_norm, w_kv, k_gain,
           w_q, q_gain, rel_bias, w_o):
    depth = ffn_norm.shape[0]
    assert depth == 2 and conv_w_in.shape[0] == 1 and w_q.shape[0] == 1
    d = D_MODEL
    n_p, t_p, _ = x_prompt.shape
    n_s, t_s, _ = x_sample.shape
    m_p, m_s = n_p * t_p, n_s * t_s
    m_all = m_p + m_s
    tm_p, tm_f = PROMPT_ROW_TILE, FFN_ROW_TILE
    assert m_all % tm_f == 0 and m_s < tm_f and m_p % m_s == 0 and m_p % tm_p == 0
    ffn_tiles = m_all // tm_f
    s_first = m_p // m_s

    ffn_norm4 = ffn_norm.reshape(depth, 2, 1, d)
    mix_norm3 = mix_norm.reshape(depth, 1, d)
    kv_norm2 = kv_norm.reshape(1, d)
    k_gain2 = k_gain.reshape(1, HEAD_DIM)
    q_gain3 = q_gain.reshape(-1, 1, HEAD_DIM)

    def ffn(tail, main, layer, slot, final=False):
        w32 = [_weight(w, (layer, slot)) for w in (ffn_w_gate, ffn_w_up, ffn_w_down)]
        rows = m_p if final else m_all
        in_place = tail.arr is main.arr and tail.arr.shape[0] == rows
        *ys, w16 = _ffn(tail, 1, Dest(rows, ffn_tiles - 1, IN_PLACE if in_place else None),
                        ffn_norm4, (layer, slot), *w32, tm_f, n_split=m_s if final else 0)
        main = Rows(ys[0], 0) if in_place else main
        h, _ = _ffn(main, ffn_tiles - 1, Dest(rows, 0, IN_PLACE if in_place else ys[0]),
                    ffn_norm4, (layer, slot), *w16, tm_f)
        return (h, ys[1]) if final else h

    def both(h):
        return Rows(h, ffn_tiles - 1), Rows(h, 0)

    xp = x_prompt.reshape(m_p, d)
    x_tail = jnp.concatenate([xp[m_p - (tm_f - m_s):], x_sample.reshape(m_s, d)], axis=0)
    h = ffn(Rows(x_tail, 0), Rows(xp, 0), 0, 0)

    mix_blocks = d // MIX_COL_TILE
    w_mix32 = (_weight(conv_w_in, (0,)), _weight(conv_w_in, (0,), mix_blocks),
               _weight(conv_w_in, (0,), 2 * mix_blocks))
    h, conv_s, w_mix16 = _mixer(Rows(h, s_first), m_s, Dest(m_all, s_first, IN_PLACE), mix_norm3,
                                (0,), *w_mix32, conv_w, (0,), _weight(conv_w_out, (0,)),
                                state_conv[0], t_s, m_s, MIX_COL_TILE)
    conv_zero = jnp.zeros((n_p, CONV_WIDTH - 1, d), F32)
    h, conv_p, _ = _mixer(Rows(h, 0), m_p, Dest(m_all, 0, IN_PLACE), mix_norm3, (0,),
                          *w_mix16[:3], conv_w, (0,), w_mix16[3], conv_zero, t_p,
                          PROMPT_MIX_ROW_TILE, PROMPT_MIX_COL_TILE, PROMPT_MIX_PARTS)

    h = ffn(*both(h), 0, 1)

    proj_blocks = d // CAST_COL_TILE
    (k_s, v_s, k16_s, v16_s), w_kv16 = _kv_proj(
        Rows(h, s_first), m_s, kv_norm2, _weight(w_kv), _weight(w_kv, (), proj_blocks), k_gain2,
        t_s, m_s, v_transposed=False)
    (k_p, v_p, k16_p, v16t_p), _ = _kv_proj(
        Rows(h, 0), m_p, kv_norm2, *w_kv16, k_gain2, t_p, tm_p, v_transposed=True,
        n_parts=PROMPT_PROJ_PARTS)

    h = ffn(*both(h), 1, 0)

    q16_s, w_q16 = _q_proj(Rows(h, s_first), m_s, mix_norm3, (1,), _weight(w_q, (0,)),
                           q_gain3, (0,), m_s)
    q16_p, _ = _q_proj(Rows(h, 0), m_p, mix_norm3, (1,), *w_q16, q_gain3, (0,), PROMPT_QO_ROW_TILE)
    att_s = _attn_sample(q16_s, k16_s, v16_s, cache_k, cache_v, rel_bias[0], n_s, t_s)
    att_p = _attn_prompt(q16_p, k16_p, v16t_p, rel_bias[0], n_p, t_p)
    h, w_o16 = _o_proj(Rows(h, s_first), m_s, Dest(m_all, s_first, IN_PLACE), att_s,
                       _weight(w_o, (0,)), m_s, att_transposed=False)
    h, _ = _o_proj(Rows(h, 0), m_p, Dest(m_all, 0, IN_PLACE), att_p, *w_o16, PROMPT_QO_ROW_TILE,
                   att_transposed=True)

    y_p, y_s = ffn(*both(h), 1, 1, final=True)

    keep_p, keep_s = min(PAST_ROWS, t_p), min(PAST_ROWS, t_s)
    heads = (N_HEADS, HEAD_DIM)
    return (y_p.reshape(n_p, t_p, d), y_s.reshape(n_s, t_s, d), conv_p[None],
            k_p.reshape(n_p, keep_p, *heads), v_p.reshape(n_p, keep_p, *heads), conv_s[None],
            k_s.reshape(n_s, keep_s, *heads), v_s.reshape(n_s, keep_s, *heads))
```

```python
import collections
import functools

import jax
import jax.numpy as jnp
from jax import lax
from jax.experimental import pallas as pl
from jax.experimental.pallas import tpu as pltpu

F32 = jnp.float32
BF16 = jnp.bfloat16

D_MODEL = 2048
HEAD_DIM = 128
N_HEADS = D_MODEL // HEAD_DIM
CHUNK = 64
N_PAST_CHUNKS = 8
PAST_ROWS = N_PAST_CHUNKS * CHUNK
MAX_REL = 128
CONV_WIDTH = 3
EPS = 1e-6
FFN_RES = 0.5
NEG_INF = -1e30
ATTN_SCALE = HEAD_DIM ** -0.5
LOG2_E = 1.4426950408889634

SUBLANES = 8
LANES = 128
V7X_VMEM_BYTES = 64 * 1024 * 1024
VMEM_RESERVE_BYTES = 8 * 1024 * 1024
VMEM_LIMIT_BYTES = V7X_VMEM_BYTES - VMEM_RESERVE_BYTES

PROMPT_ROW_TILE = 512
PROMPT_QO_ROW_TILE = 1024
PROMPT_PROJ_PARTS = 2
FFN_ROW_TILE = 1040
FFN_COL_TILE = 512
FFN_CAST_COL_TILE = 256
MIX_COL_TILE = 512
PROMPT_MIX_ROW_TILE = 1024
PROMPT_MIX_PARTS = 2
PROMPT_MIX_COL_TILE = 256
KV_COL_TILE = 1024
CAST_COL_TILE = 512
ATTN_Q_TILE = 512
ATTN_SUB = 256
ATTN_WIN = ATTN_SUB + PAST_ROWS
ATTN_STAGES = 4

Weight = collections.namedtuple("Weight", ["arr", "lead", "col_off", "col_tiled"])
Rows = collections.namedtuple("Rows", ["arr", "first"])
Dest = collections.namedtuple("Dest", ["rows", "first", "alias"])
IN_PLACE = "in place"


def _weight(arr, lead=(), col_off=0, col_tiled=False):
    return Weight(arr, tuple(lead), col_off, col_tiled)


def _wspec(w, block, index, **mode):
    lead = w.lead

    def index_map(*ids):
        r, c = index(*ids)
        return lead + (r, c + w.col_off)

    return pl.BlockSpec((None,) * len(lead) + block, index_map, **mode)


def _resident(n_blocks):
    return dict(pipeline_mode=pl.Buffered(1)) if n_blocks == 1 else {}


def _col_spec(w, rows, tn, col):
    if w.col_tiled:
        assert w.arr.shape[1:] == (rows, tn)
        return pl.BlockSpec((None, rows, tn), lambda *ids: (col(*ids), 0, 0))
    return _wspec(w, (rows, tn), lambda *ids: (0, col(*ids)))


def _lead_spec(lead, block):
    lead = tuple(lead)
    return pl.BlockSpec((None,) * len(lead) + block, lambda *ids: lead + (0,) * len(block))


def _is_f32(w):
    return w.arr.dtype == F32


def _bf16_weight(w_ref, w16_ref):
    if w16_ref is None:
        return w_ref[...]
    w16 = w_ref[...].astype(BF16)
    w16_ref[...] = w16
    return w16


def _dest_args(src, dest):
    if dest.alias is None:
        return [], [], {}
    if dest.alias is IN_PLACE:
        assert src.arr.shape[0] == dest.rows and src.first == dest.first
        return [], [], {0: 0}
    return [pl.BlockSpec(memory_space=pl.ANY)], [dest.alias], {0: 0}


def _params(semantics):
    return pltpu.CompilerParams(dimension_semantics=semantics,
                                vmem_limit_bytes=VMEM_LIMIT_BYTES)


def _rms_rows(x, g):
    ms = jnp.mean(x * x, axis=-1, keepdims=True)
    return (x * lax.rsqrt(ms + EPS)) * g


def _ffn_kernel(*refs, emit, aliased, n_split):
    h_ref, g_ref, wg_ref, wu_ref, wd_ref, out_ref, *rest = refs[aliased:]
    split_ref = rest.pop(0) if n_split else None
    wg16_ref, wu16_ref, wd16_ref = rest[:3] if emit else (None, None, None)
    xn_ref = rest[-1]
    f = pl.program_id(1)

    @pl.when(f == 0)
    def _():
        h = h_ref[...]
        xn_ref[...] = _rms_rows(h, g_ref[...]).astype(BF16)
        out_ref[...] = h

    xn = xn_ref[...]
    gate = jnp.dot(xn, _bf16_weight(wg_ref, wg16_ref), preferred_element_type=F32)
    up = jnp.dot(xn, _bf16_weight(wu_ref, wu16_ref), preferred_element_type=F32)
    act = (gate * jax.nn.sigmoid(gate) * (FFN_RES * up)).astype(BF16)
    out_ref[...] += jnp.dot(act, _bf16_weight(wd_ref, wd16_ref), preferred_element_type=F32)

    if n_split:
        @pl.when(f == pl.num_programs(1) - 1)
        def _():
            split_ref[...] = out_ref[out_ref.shape[0] - n_split:, :]


def _ffn_stream_kernel(*refs, aliased, specs, grid):
    h_hbm, g_ref, wg_hbm, wu_hbm, wd_hbm, out_hbm, xn_ref = refs[aliased:]

    def step(idx, h_ref, wg_ref, wu_ref, wd_ref, out_ref):
        f = (idx if isinstance(idx, tuple) else idx.index)[1]

        @pl.when(f == 0)
        def _():
            h = h_ref[...]
            xn_ref[...] = _rms_rows(h, g_ref[...]).astype(BF16)
            out_ref[...] = h

        xn = xn_ref[...]
        gate = jnp.dot(xn, wg_ref[...], preferred_element_type=F32)
        up = jnp.dot(xn, wu_ref[...], preferred_element_type=F32)
        act = (gate * jax.nn.sigmoid(gate) * (FFN_RES * up)).astype(BF16)
        out_ref[...] += jnp.dot(act, wd_ref[...], preferred_element_type=F32)

    pltpu.emit_pipeline(step, grid=grid, in_specs=specs[:-1], out_specs=specs[-1:],
                        _explicit_indices=True)(h_hbm, wg_hbm, wu_hbm, wd_hbm, out_hbm)


def _ffn_stream(src, n_tiles, dest, norm_g, g_lead, wg, wu, wd, tm):
    d = src.arr.shape[1]
    d_ff = wd.arr.shape[-2]
    tf = FFN_COL_TILE
    assert not _is_f32(wg) and wg.col_tiled and wu.col_tiled and not wd.lead
    specs = [pl.BlockSpec((tm, d), lambda i, f: (src.first + i, 0)),
             pl.BlockSpec((None, d, tf), lambda i, f: (f, 0, 0)),
             pl.BlockSpec((None, d, tf), lambda i, f: (f, 0, 0)),
             pl.BlockSpec((tf, d), lambda i, f: (f, 0)),
             pl.BlockSpec((tm, d), lambda i, f: (dest.first + i, 0))]
    alias_specs, alias_args, aliases = _dest_args(src, dest)
    hbm = pl.BlockSpec(memory_space=pl.ANY)
    out = pl.pallas_call(
        functools.partial(_ffn_stream_kernel, aliased=len(alias_args), specs=specs,
                          grid=(n_tiles, d_ff // tf)),
        out_shape=jax.ShapeDtypeStruct((dest.rows, d), F32),
        in_specs=alias_specs + [hbm, pl.BlockSpec(memory_space=pltpu.VMEM), hbm, hbm, hbm],
        out_specs=hbm,
        scratch_shapes=[pltpu.VMEM((tm, d), BF16)],
        input_output_aliases=aliases,
        compiler_params=pltpu.CompilerParams(vmem_limit_bytes=VMEM_LIMIT_BYTES),
        name="ffn_stream",
    )(*alias_args, src.arr, norm_g[tuple(g_lead)], wg.arr, wu.arr, wd.arr)
    return out, ()


def _ffn(src, n_tiles, dest, norm_g, g_lead, wg, wu, wd, tm, n_split=0):
    d = src.arr.shape[1]
    d_ff = wd.arr.shape[-2]
    emit = _is_f32(wg)
    tf = FFN_CAST_COL_TILE if emit else FFN_COL_TILE
    out_shape = [jax.ShapeDtypeStruct((dest.rows, d), F32)]
    out_specs = [pl.BlockSpec((tm, d), lambda i, f: (dest.first + i, 0))]
    if n_split:
        assert n_tiles == 1
        out_shape += [jax.ShapeDtypeStruct((n_split, d), F32)]
        out_specs += [pl.BlockSpec((n_split, d), lambda i, f: (0, 0))]
    h_mode = {}
    if emit:
        assert n_tiles == 1
        per = FFN_COL_TILE // tf
        tiled = jax.ShapeDtypeStruct((d_ff // FFN_COL_TILE, d, FFN_COL_TILE), BF16)
        out_shape += [tiled, tiled, jax.ShapeDtypeStruct((d_ff, d), BF16)]
        out_specs += [pl.BlockSpec((None, d, tf), lambda i, f: (f // per, 0, f % per)),
                      pl.BlockSpec((None, d, tf), lambda i, f: (f // per, 0, f % per)),
                      pl.BlockSpec((tf, d), lambda i, f: (f, 0))]
        h_mode = dict(pipeline_mode=pl.Buffered(1))
    alias_specs, alias_args, aliases = _dest_args(src, dest)
    res = pl.pallas_call(
        functools.partial(_ffn_kernel, emit=emit, aliased=len(alias_args), n_split=n_split),
        out_shape=out_shape,
        grid=(n_tiles, d_ff // tf),
        in_specs=alias_specs + [
            pl.BlockSpec((tm, d), lambda i, f: (src.first + i, 0), **h_mode),
            _lead_spec(g_lead, (1, d)),
            _col_spec(wg, d, tf, lambda i, f: f),
            _col_spec(wu, d, tf, lambda i, f: f),
            _wspec(wd, (tf, d), lambda i, f: (f, 0)),
        ],
        out_specs=out_specs,
        scratch_shapes=[pltpu.VMEM((tm, d), BF16)],
        input_output_aliases=aliases,
        compiler_params=_params(("parallel", "arbitrary")),
        name="ffn",
    )(*alias_args, src.arr, norm_g, wg.arr, wu.arr, wd.arr)
    n_act = 2 if n_split else 1
    w16 = tuple(_weight(a, col_tiled=a.ndim == 3) for a in res[n_act:])
    return tuple(res[:n_act]) + (w16,)


def _mixer_kernel(*refs, seq_len, tm, emit, aliased, n_parts):
    (h_ref, g_ref, wb_ref, wc_ref, wx_ref, cw_ref, wo_ref, p0_ref, p1_ref,
     out_ref, st_ref, *rest) = refs[aliased:]
    wb16_ref, wc16_ref, wx16_ref, wo16_ref = rest[:4] if emit else (None,) * 4
    xn_ref, carry_ref = rest[-2:]
    i = pl.program_id(0)
    j = pl.program_id(1)

    @pl.when(j == 0)
    def _():
        h = h_ref[...]
        xn_ref[...] = _rms_rows(h, g_ref[...]).astype(BF16)
        out_ref[...] = h

    if seq_len >= tm:
        @pl.when((i % (seq_len // tm)) == 0)
        def _():
            carry_ref[j] = p0_ref[0]

    wb, wc, wx, wo = (_bf16_weight(wb_ref, wb16_ref), _bf16_weight(wc_ref, wc16_ref),
                      _bf16_weight(wx_ref, wx16_ref), _bf16_weight(wo_ref, wo16_ref))
    cw = cw_ref[...]
    tn = cw.shape[1]
    hm = tm // n_parts
    bs, us = [], []
    for part in range(n_parts):
        xn = xn_ref[part * hm:(part + 1) * hm, :]
        bs.append(jnp.dot(xn, wb, preferred_element_type=F32))
        c = jnp.dot(xn, wc, preferred_element_type=F32)
        x = jnp.dot(xn, wx, preferred_element_type=F32)
        us.append(c * x)
    row = lax.broadcasted_iota(jnp.int32, (hm, tn), 0)
    for part in range(n_parts):
        u = us[part]
        if seq_len >= tm:
            prev = carry_ref[j] if part == 0 else us[part - 1][hm - 2:hm, :]
            p0 = prev[0:1, :]
            p1 = prev[1:2, :]
            pos = row
        else:
            p0 = p0_ref[...]
            p1 = p1_ref[...]
            pos = row % seq_len
        um1 = jnp.where(pos == 0, p1, pltpu.roll(u, 1, 0))
        um2 = jnp.where(pos == 0, p0, jnp.where(pos == 1, p1, pltpu.roll(u, 2, 0)))
        conv = cw[0:1, :] * um2 + cw[1:2, :] * um1 + cw[2:3, :] * u
        v = (bs[part] * conv).astype(BF16)
        out_ref[part * hm:(part + 1) * hm, :] += jnp.dot(v, wo, preferred_element_type=F32)
    if seq_len >= tm:
        carry_ref[j] = us[-1][hm - 2:hm, :]
        st_ref[0] = us[-1][hm - 2:hm, :]
    else:
        st_ref[...] = us[0].reshape(tm // seq_len, seq_len, tn)[:, seq_len - 2:seq_len, :]


def _mixer(src, rows, dest, norm_g, g_lead, wb, wc, wx, conv_w, cw_lead, wo, state, seq_len, tm,
           tn, n_parts=1):
    d = src.arr.shape[1]
    n_seq = rows // seq_len
    n_j = d // tn
    n_i = rows // tm
    emit = _is_f32(wb)
    if seq_len >= tm:
        tiles_per_seq = seq_len // tm
        p0, p1 = state, state
        p_spec = pl.BlockSpec((1, CONV_WIDTH - 1, tn), lambda i, j: (i // tiles_per_seq, 0, j))
        st_shape = jax.ShapeDtypeStruct((n_i, CONV_WIDTH - 1, d), F32)
        st_spec = pl.BlockSpec((1, CONV_WIDTH - 1, tn), lambda i, j: (i, 0, j))
    else:
        assert tm == rows and tm % seq_len == 0 and n_parts == 1
        p0 = jnp.repeat(state[:, 0], seq_len, axis=0)
        p1 = jnp.repeat(state[:, 1], seq_len, axis=0)
        p_spec = pl.BlockSpec((tm, tn), lambda i, j: (i, j))
        st_shape = jax.ShapeDtypeStruct((n_seq, CONV_WIDTH - 1, d), F32)
        st_spec = pl.BlockSpec((n_seq, CONV_WIDTH - 1, tn), lambda i, j: (0, 0, j))
    out_shape = [jax.ShapeDtypeStruct((dest.rows, d), F32), st_shape]
    out_specs = [pl.BlockSpec((tm, d), lambda i, j: (dest.first + i, 0)), st_spec]
    if emit:
        assert n_i == 1
        out_shape += [jax.ShapeDtypeStruct((d, d), BF16)] * 4
        out_specs += [pl.BlockSpec((d, tn), lambda i, j: (0, j))] * 3
        out_specs += [pl.BlockSpec((tn, d), lambda i, j: (j, 0))]
    alias_specs, alias_args, aliases = _dest_args(src, dest)
    res = pl.pallas_call(
        functools.partial(_mixer_kernel, seq_len=seq_len, tm=tm, emit=emit,
                          aliased=len(alias_args), n_parts=n_parts),
        out_shape=out_shape,
        grid=(n_i, n_j),
        in_specs=alias_specs + [
            pl.BlockSpec((tm, d), lambda i, j: (src.first + i, 0)),
            _lead_spec(g_lead, (1, d)),
            _wspec(wb, (d, tn), lambda i, j: (0, j)),
            _wspec(wc, (d, tn), lambda i, j: (0, j)),
            _wspec(wx, (d, tn), lambda i, j: (0, j)),
            pl.BlockSpec((None,) * len(cw_lead) + (CONV_WIDTH, tn),
                         lambda i, j: tuple(cw_lead) + (0, j)),
            _wspec(wo, (tn, d), lambda i, j: (j, 0)),
            p_spec,
            p_spec,
        ],
        out_specs=out_specs,
        scratch_shapes=[pltpu.VMEM((tm, d), BF16),
                        pltpu.VMEM((n_j, CONV_WIDTH - 1, tn), F32)],
        input_output_aliases=aliases,
        compiler_params=_params(("arbitrary", "arbitrary")),
        name="conv_mixer",
    )(*alias_args, src.arr, norm_g, wb.arr, wc.arr, wx.arr, conv_w, wo.arr, p0, p1)
    out, st = res[0], res[1]
    if seq_len >= tm:
        st = st[seq_len // tm - 1::seq_len // tm]
    return out, st, tuple(_weight(a) for a in res[2:])


def _kv_kernel(h_ref, g_ref, wk_ref, wv_ref, kg_ref, k32_ref, v32_ref, k16_ref, v16_ref,
               *rest, emit, n_parts, v_transposed):
    wk16_ref, wv16_ref = rest if emit else (None, None)
    wk, wv = _bf16_weight(wk_ref, wk16_ref), _bf16_weight(wv_ref, wv16_ref)
    g, kg = g_ref[...], kg_ref[...]
    hm = h_ref.shape[0] // n_parts
    ks, vs = [], []
    for part in range(n_parts):
        xn = _rms_rows(h_ref[part * hm:(part + 1) * hm, :], g).astype(BF16)
        ks.append(jnp.dot(xn, wk, preferred_element_type=F32))
        vs.append(jnp.dot(xn, wv, preferred_element_type=F32))
    for part, (k, v) in enumerate(zip(ks, vs)):
        rows = slice(part * hm, (part + 1) * hm)
        v32_ref[rows, :] = v
        for hd in range(k.shape[1] // HEAD_DIM):
            cols = slice(hd * HEAD_DIM, (hd + 1) * HEAD_DIM)
            kh = _rms_rows(k[:, cols], kg)
            k32_ref[rows, cols] = kh
            k16_ref[hd, rows, :] = kh.astype(BF16)
            if v_transposed:
                v16_ref[hd, :, rows] = v[:, cols].T.astype(BF16)
            else:
                v16_ref[hd, rows, :] = v[:, cols].astype(BF16)


def _kv_proj(src, rows, norm_g, wk, wv, k_gain, seq_len, tm, v_transposed, n_parts=1):
    d = src.arr.shape[1]
    emit = _is_f32(wk)
    tn = CAST_COL_TILE if emit else KV_COL_TILE
    hpb = tn // HEAD_DIM
    if seq_len >= tm:
        assert tm == min(PAST_ROWS, seq_len) and seq_len % tm == 0
        tiles_per_seq = seq_len // tm
        kept_rows = (rows // seq_len) * tm
        kept_spec = pl.BlockSpec((tm, tn), lambda j, i: (i // tiles_per_seq, j))
    else:
        assert tm % seq_len == 0 and seq_len <= PAST_ROWS
        kept_rows = rows
        kept_spec = pl.BlockSpec((tm, tn), lambda j, i: (i, j))
    if v_transposed:
        v16_shape = jax.ShapeDtypeStruct((N_HEADS, HEAD_DIM, rows), BF16)
        v16_spec = pl.BlockSpec((hpb, HEAD_DIM, tm), lambda j, i: (j, 0, i))
    else:
        v16_shape = jax.ShapeDtypeStruct((N_HEADS, rows, HEAD_DIM), BF16)
        v16_spec = pl.BlockSpec((hpb, tm, HEAD_DIM), lambda j, i: (j, i, 0))
    out_shape = [jax.ShapeDtypeStruct((kept_rows, d), F32),
                 jax.ShapeDtypeStruct((kept_rows, d), F32),
                 jax.ShapeDtypeStruct((N_HEADS, rows, HEAD_DIM), BF16), v16_shape]
    out_specs = [kept_spec, kept_spec,
                 pl.BlockSpec((hpb, tm, HEAD_DIM), lambda j, i: (j, i, 0)), v16_spec]
    if emit:
        assert rows == tm
        out_shape += [jax.ShapeDtypeStruct((d, d), BF16)] * 2
        out_specs += [pl.BlockSpec((d, tn), lambda j, i: (0, j))] * 2
    res = pl.pallas_call(
        functools.partial(_kv_kernel, emit=emit, n_parts=n_parts, v_transposed=v_transposed),
        out_shape=out_shape,
        grid=(d // tn, rows // tm),
        in_specs=[
            pl.BlockSpec((tm, d), lambda j, i: (src.first + i, 0)),
            pl.BlockSpec((1, d), lambda j, i: (0, 0)),
            _wspec(wk, (d, tn), lambda j, i: (0, j)),
            _wspec(wv, (d, tn), lambda j, i: (0, j)),
            pl.BlockSpec((1, HEAD_DIM), lambda j, i: (0, 0)),
        ],
        out_specs=out_specs,
        compiler_params=_params(("arbitrary", "arbitrary")),
        name="kv_proj",
    )(src.arr, norm_g, wk.arr, wv.arr, k_gain)
    return res[:4], tuple(_weight(a) for a in res[4:])


def _q_kernel(h_ref, g_ref, wq_ref, qg_ref, q16_ref, *rest, emit):
    wq16_ref = rest[0] if emit else None
    xn_ref = rest[-1]

    @pl.when(pl.program_id(1) == 0)
    def _():
        xn_ref[...] = _rms_rows(h_ref[...], g_ref[...]).astype(BF16)

    q = jnp.dot(xn_ref[...], _bf16_weight(wq_ref, wq16_ref), preferred_element_type=F32)
    qg = qg_ref[...]
    for hd in range(q.shape[1] // HEAD_DIM):
        cols = slice(hd * HEAD_DIM, (hd + 1) * HEAD_DIM)
        q16_ref[hd] = _rms_rows(q[:, cols], qg).astype(BF16)


def _q_proj(src, rows, norm_g, g_lead, wq, q_gain, qg_lead, tm):
    d = src.arr.shape[1]
    emit = _is_f32(wq)
    tn = CAST_COL_TILE if emit else d
    hpb = tn // HEAD_DIM
    out_shape = [jax.ShapeDtypeStruct((N_HEADS, rows, HEAD_DIM), BF16)]
    out_specs = [pl.BlockSpec((hpb, tm, HEAD_DIM), lambda i, j: (j, i, 0))]
    if emit:
        assert rows == tm
        out_shape += [jax.ShapeDtypeStruct((d, d), BF16)]
        out_specs += [pl.BlockSpec((d, tn), lambda i, j: (0, j))]
    res = pl.pallas_call(
        functools.partial(_q_kernel, emit=emit),
        out_shape=out_shape,
        grid=(rows // tm, d // tn),
        in_specs=[
            pl.BlockSpec((tm, d), lambda i, j: (src.first + i, 0)),
            _lead_spec(g_lead, (1, d)),
            _wspec(wq, (d, tn), lambda i, j: (0, j), **_resident(d // tn)),
            _lead_spec(qg_lead, (1, HEAD_DIM)),
        ],
        out_specs=out_specs,
        scratch_shapes=[pltpu.VMEM((tm, d), BF16)],
        compiler_params=_params(("parallel", "arbitrary")),
        name="q_proj",
    )(src.arr, norm_g, wq.arr, q_gain)
    return res[0], tuple(_weight(a) for a in res[1:])


def _o_kernel(*refs, emit, att_transposed, aliased):
    h_ref, att_ref, wo_ref, out_ref, *rest = refs[aliased:]
    wo16_ref = rest[0] if emit else None
    wo = _bf16_weight(wo_ref, wo16_ref)
    if att_transposed:
        att_t = att_ref[...].reshape(D_MODEL, att_ref.shape[2])
        y = lax.dot_general(att_t, wo, (((0,), (0,)), ((), ())), preferred_element_type=F32)
    else:
        att2d_ref = rest[-1]

        @pl.when(pl.program_id(1) == 0)
        def _():
            for hd in range(N_HEADS):
                att2d_ref[:, hd * HEAD_DIM:(hd + 1) * HEAD_DIM] = att_ref[hd]

        y = jnp.dot(att2d_ref[...], wo, preferred_element_type=F32)
    out_ref[...] = h_ref[...] + y


def _o_proj(src, rows, dest, att16, wo, tm, att_transposed):
    d = src.arr.shape[1]
    emit = _is_f32(wo)
    tn = CAST_COL_TILE if emit else d
    out_shape = [jax.ShapeDtypeStruct((dest.rows, d), F32)]
    out_specs = [pl.BlockSpec((tm, tn), lambda i, j: (dest.first + i, j))]
    if emit:
        assert rows == tm
        out_shape += [jax.ShapeDtypeStruct((d, d), BF16)]
        out_specs += [pl.BlockSpec((d, tn), lambda i, j: (0, j))]
    if att_transposed:
        att_spec = pl.BlockSpec((N_HEADS, HEAD_DIM, tm), lambda i, j: (0, 0, i))
        scratch = []
    else:
        att_spec = pl.BlockSpec((N_HEADS, tm, HEAD_DIM), lambda i, j: (0, i, 0))
        scratch = [pltpu.VMEM((tm, d), BF16)]
    alias_specs, alias_args, aliases = _dest_args(src, dest)
    res = pl.pallas_call(
        functools.partial(_o_kernel, emit=emit, att_transposed=att_transposed,
                          aliased=len(alias_args)),
        out_shape=out_shape,
        grid=(rows // tm, d // tn),
        in_specs=alias_specs + [
            pl.BlockSpec((tm, tn), lambda i, j: (src.first + i, j)),
            att_spec,
            _wspec(wo, (d, tn), lambda i, j: (0, j), **_resident(d // tn)),
        ],
        out_specs=out_specs,
        scratch_shapes=scratch,
        input_output_aliases=aliases,
        compiler_params=_params(("parallel", "arbitrary")),
        name="o_proj",
    )(*alias_args, src.arr, att16, wo.arr)
    return res[0], tuple(_weight(a) for a in res[1:])


def _build_prompt_bias(near_ref, far_ref, bias_ref):
    assert ATTN_WIN == 3 * ATTN_SUB and PAST_ROWS - MAX_REL == ATTN_WIN // 2
    row = lax.broadcasted_iota(jnp.int32, (ATTN_SUB, ATTN_SUB), 0)
    col = lax.broadcasted_iota(jnp.int32, (ATTN_SUB, ATTN_SUB), 1)
    for h in range(N_HEADS):
        circ = jnp.broadcast_to(near_ref[h], (ATTN_SUB, ATTN_SUB))
        for bit in range(ATTN_SUB.bit_length() - 1):
            circ = jnp.where((row >> bit) & 1 == 1, pltpu.roll(circ, 1 << bit, 1), circ)
        far = jnp.broadcast_to(far_ref[h], (ATTN_SUB, ATTN_SUB))
        for t in range(ATTN_WIN // ATTN_SUB):
            key = row + t * ATTN_SUB
            kc, qc = key // CHUNK, col // CHUNK
            band = (kc >= qc) & (kc <= qc + N_PAST_CHUNKS)
            val = jnp.where(key - col <= ATTN_WIN // 2, far, circ)
            bias_ref[h, t * ATTN_SUB:(t + 1) * ATTN_SUB, :] = jnp.where(band, val, NEG_INF)


def _attn_prompt_kernel(q_ref, kp_ref, kc_ref, vp_ref, vc_ref, near_ref, far_ref, o_ref,
                        bias_ref, s_refs, p_refs):
    i = pl.program_id(1)

    @pl.when((pl.program_id(0) == 0) & (i == 0))
    def _():
        _build_prompt_bias(near_ref, far_ref, bias_ref)
        p_refs[...] = jnp.zeros(p_refs.shape, BF16)

    n_before_start = jnp.where(i == 0, ATTN_Q_TILE, 0)
    groups = CHUNK // SUBLANES
    lane_tiles = ATTN_SUB // LANES
    live_blocks = [(c, lt) for c in range(ATTN_WIN // CHUNK) for lt in range(lane_tiles)
                   if lt * LANES // CHUNK <= c <= ((lt + 1) * LANES - 1) // CHUNK + N_PAST_CHUNKS]

    subs = ATTN_Q_TILE // ATTN_SUB
    n_units = N_HEADS * subs

    def unit(u):
        return u // subs, (u % subs) * ATTN_SUB, u % ATTN_STAGES

    def scores(u):
        h, lo, slot = unit(u)
        q = q_ref[h, lo:lo + ATTN_SUB, :]
        n_past = ATTN_Q_TILE - lo
        dims = (((1,), (1,)), ((), ()))
        s_refs[slot, 0:n_past, :] = lax.dot_general(kp_ref[h, lo:, :], q, dims,
                                                    preferred_element_type=F32)
        s_refs[slot, n_past:, :] = lax.dot_general(kc_ref[h, 0:ATTN_WIN - n_past, :], q, dims,
                                                   preferred_element_type=F32)

    def block(c, lt):
        return slice(c * CHUNK, (c + 1) * CHUNK), slice(lt * LANES, (lt + 1) * LANES)

    def biased_max(u):
        h, lo, slot = unit(u)
        s_ref = s_refs.at[slot]
        m8 = [jnp.full((SUBLANES, LANES), NEG_INF, F32) for _ in range(lane_tiles)]
        for c, lt in live_blocks:
            rows, cols = block(c, lt)
            before_start = c * CHUNK < n_before_start - lo
            x = s_ref[rows, cols] * (ATTN_SCALE * LOG2_E) + bias_ref[h, rows, cols]
            x = jnp.where(before_start, NEG_INF, x)
            s_ref[rows, cols] = x
            m8[lt] = jnp.maximum(m8[lt], jnp.max(x.reshape(groups, SUBLANES, LANES), axis=0))
        return [jnp.max(v, axis=0, keepdims=True) for v in m8]

    def exponentials(u, m):
        slot = unit(u)[2]
        s_ref, p_ref = s_refs.at[slot], p_refs.at[slot]
        l8 = [jnp.zeros((SUBLANES, LANES), F32) for _ in range(lane_tiles)]
        for c, lt in live_blocks:
            rows, cols = block(c, lt)
            e = jnp.exp2(s_ref[rows, cols] - m[lt])
            p_ref[rows, cols] = e.astype(BF16)
            l8[lt] = l8[lt] + jnp.sum(e.reshape(groups, SUBLANES, LANES), axis=0)
        return 1.0 / jnp.concatenate([jnp.sum(v, axis=0, keepdims=True) for v in l8], axis=1)

    def values(u, inv):
        h, lo, slot = unit(u)
        p_ref = p_refs.at[slot]
        n_past = ATTN_Q_TILE - lo
        o = jnp.dot(vp_ref[h, :, lo:], p_ref[0:n_past, :], preferred_element_type=F32)
        o = o + jnp.dot(vc_ref[h, :, 0:ATTN_WIN - n_past], p_ref[n_past:, :],
                        preferred_element_type=F32)
        o_ref[h, :, lo:lo + ATTN_SUB] = (o * inv).astype(BF16)

    m, inv = {}, {}
    for r in range(n_units + ATTN_STAGES - 1):
        if 0 <= r - 3:
            values(r - 3, inv.pop(r - 3))
        if 0 <= r - 2 < n_units:
            inv[r - 2] = exponentials(r - 2, m.pop(r - 2))
        if 0 <= r - 1 < n_units:
            m[r - 1] = biased_max(r - 1)
        if r < n_units:
            scores(r)


def _prompt_bias_rows(rel_bias):
    assert ATTN_SUB == 2 * MAX_REL
    k_mod = (-jnp.arange(ATTN_SUB)) % ATTN_SUB
    idx = 2 * MAX_REL - (k_mod - ATTN_WIN // 2) % ATTN_SUB
    near = rel_bias[:, None, idx].astype(F32) * LOG2_E
    far = jnp.broadcast_to(rel_bias[:, None, 2 * MAX_REL:].astype(F32) * LOG2_E, near.shape)
    return near, far


def _attn_prompt(q16, k16, v16t, rel_bias, n_seq, seq_len):
    tiles = seq_len // ATTN_Q_TILE
    cur = lambda b, i: b * tiles + i
    past = lambda b, i: b * tiles + jnp.maximum(i - 1, 0)
    rows = lambda at: pl.BlockSpec((N_HEADS, ATTN_Q_TILE, HEAD_DIM), lambda b, i: (0, at(b, i), 0))
    cols = lambda at: pl.BlockSpec((N_HEADS, HEAD_DIM, ATTN_Q_TILE), lambda b, i: (0, 0, at(b, i)))
    near, far = _prompt_bias_rows(rel_bias)
    bias_row = pl.BlockSpec(near.shape, lambda b, i: (0, 0, 0))
    return pl.pallas_call(
        _attn_prompt_kernel,
        out_shape=jax.ShapeDtypeStruct(v16t.shape, BF16),
        grid=(n_seq, tiles),
        in_specs=[rows(cur), rows(past), rows(cur), cols(past), cols(cur), bias_row, bias_row],
        out_specs=cols(cur),
        scratch_shapes=[pltpu.VMEM((N_HEADS, ATTN_WIN, ATTN_SUB), F32),
                        pltpu.VMEM((ATTN_STAGES, ATTN_WIN, ATTN_SUB), F32),
                        pltpu.VMEM((ATTN_STAGES, ATTN_WIN, ATTN_SUB), BF16)],
        compiler_params=_params(("arbitrary", "arbitrary")),
        name="attn_prompt",
    )(q16, k16, k16, v16t, v16t, near, far)


def _attn_sample_kernel(q_ref, kn_ref, vn_ref, ck_ref, cv_ref, bc_ref, bn_ref, o_ref):
    cache_len = ck_ref.shape[1] // N_HEADS
    dims = (((1,), (1,)), ((), ()))

    def gather(h):
        kc = ck_ref[0, pl.ds(h, cache_len, stride=N_HEADS), :].astype(BF16)
        vc = cv_ref[0, pl.ds(h, cache_len, stride=N_HEADS), :].astype(BF16)
        return kc, vc

    def scores(h, kc):
        q = q_ref[h]
        sc = lax.dot_general(q, kc, dims, preferred_element_type=F32) * ATTN_SCALE + bc_ref[h]
        sn = lax.dot_general(q, kn_ref[h], dims, preferred_element_type=F32) * ATTN_SCALE + bn_ref[h]
        return sc, sn

    def probabilities(sc, sn):
        m = jnp.maximum(jnp.max(sc, axis=-1, keepdims=True), jnp.max(sn, axis=-1, keepdims=True))
        ec = jnp.exp(sc - m)
        en = jnp.exp(sn - m)
        inv = 1.0 / (jnp.sum(ec, axis=-1, keepdims=True) + jnp.sum(en, axis=-1, keepdims=True))
        return (ec * inv).astype(BF16), (en * inv).astype(BF16)

    def values(h, pc, pn, vc):
        o = jnp.dot(pc, vc, preferred_element_type=F32)
        o = o + jnp.dot(pn, vn_ref[h], preferred_element_type=F32)
        o_ref[h] = o.astype(BF16)

    kv, s, p = {}, {}, {}
    for r in range(N_HEADS + ATTN_STAGES - 1):
        if 0 <= r - 3:
            values(r - 3, *p.pop(r - 3), kv.pop(r - 3)[1])
        if 0 <= r - 2 < N_HEADS:
            p[r - 2] = probabilities(*s.pop(r - 2))
        if 0 <= r - 1 < N_HEADS:
            s[r - 1] = scores(r - 1, kv[r - 1][0])
        if r < N_HEADS:
            kv[r] = gather(r)


def _attn_sample(q16, k16, v16, cache_k, cache_v, rel_bias, n_seq, seq_len):
    cache_len = cache_k.shape[1]
    ck = cache_k.reshape(n_seq, cache_len * N_HEADS, HEAD_DIM)
    cv = cache_v.reshape(n_seq, cache_len * N_HEADS, HEAD_DIM)
    n_far = max(cache_len - MAX_REL, 0)
    a = jnp.arange(seq_len)[:, None]
    w = jnp.arange(n_far, cache_len + seq_len)[None, :]
    idx = jnp.clip(a - w + cache_len, -MAX_REL, MAX_REL) + MAX_REL
    near = rel_bias[:, idx].astype(F32)
    far = jnp.broadcast_to(rel_bias[:, None, 2 * MAX_REL:].astype(F32), (N_HEADS, seq_len, n_far))
    bias_c = jnp.concatenate([far, near[:, :, :cache_len - n_far]], axis=2)
    bias_n = near[:, :, cache_len - n_far:]
    new = pl.BlockSpec((N_HEADS, seq_len, HEAD_DIM), lambda b: (0, b, 0))
    cache = pl.BlockSpec((1, cache_len * N_HEADS, HEAD_DIM), lambda b: (b, 0, 0))
    return pl.pallas_call(
        _attn_sample_kernel,
        out_shape=jax.ShapeDtypeStruct(q16.shape, BF16),
        grid=(n_seq,),
        in_specs=[new, new, new, cache, cache,
                  pl.BlockSpec(bias_c.shape, lambda b: (0, 0, 0)),
                  pl.BlockSpec(bias_n.shape, lambda b: (0, 0, 0))],
        out_specs=new,
        compiler_params=_params(("parallel",)),
        name="attn_sample",
    )(q16, k16, v16, ck, cv, bias_c, bias_n)


def kernel(x_prompt, x_sample, state_conv, cache_k, cache_v, ffn_norm, ffn_w_gate, ffn_w_up,
           ffn_w_down, mix_norm, conv_w_in, conv_w, conv_w_out, kv_norm, w_kv, k_gain,
           w_q, q_gain, rel_bias, w_o):
    depth = ffn_norm.shape[0]
    assert depth == 2 and conv_w_in.shape[0] == 1 and w_q.shape[0] == 1
    d = D_MODEL
    n_p, t_p, _ = x_prompt.shape
    n_s, t_s, _ = x_sample.shape
    m_p, m_s = n_p * t_p, n_s * t_s
    m_all = m_p + m_s
    tm_p, tm_f = PROMPT_ROW_TILE, FFN_ROW_TILE
    assert m_all % tm_f == 0 and m_s < tm_f and m_p % m_s == 0 and m_p % tm_p == 0
    ffn_tiles = m_all // tm_f
    s_first = m_p // m_s

    ffn_norm4 = ffn_norm.reshape(depth, 2, 1, d)
    mix_norm3 = mix_norm.reshape(depth, 1, d)
    kv_norm2 = kv_norm.reshape(1, d)
    k_gain2 = k_gain.reshape(1, HEAD_DIM)
    q_gain3 = q_gain.reshape(-1, 1, HEAD_DIM)

    def ffn(tail, main, layer, slot, final=False):
        w32 = [_weight(w, (layer, slot)) for w in (ffn_w_gate, ffn_w_up, ffn_w_down)]
        rows = m_p if final else m_all
        in_place = tail.arr is main.arr and tail.arr.shape[0] == rows
        *ys, w16 = _ffn(tail, 1, Dest(rows, ffn_tiles - 1, IN_PLACE if in_place else None),
                        ffn_norm4, (layer, slot), *w32, tm_f, n_split=m_s if final else 0)
        main = Rows(ys[0], 0) if in_place else main
        h, _ = _ffn_stream(main, ffn_tiles - 1, Dest(rows, 0, IN_PLACE if in_place else ys[0]),
                           ffn_norm4, (layer, slot), *w16, tm_f)
        return (h, ys[1]) if final else h

    def both(h):
        return Rows(h, ffn_tiles - 1), Rows(h, 0)

    xp = x_prompt.reshape(m_p, d)
    x_tail = jnp.concatenate([xp[m_p - (tm_f - m_s):], x_sample.reshape(m_s, d)], axis=0)
    h = ffn(Rows(x_tail, 0), Rows(xp, 0), 0, 0)

    mix_blocks = d // MIX_COL_TILE
    w_mix32 = (_weight(conv_w_in, (0,)), _weight(conv_w_in, (0,), mix_blocks),
               _weight(conv_w_in, (0,), 2 * mix_blocks))
    h, conv_s, w_mix16 = _mixer(Rows(h, s_first), m_s, Dest(m_all, s_first, IN_PLACE), mix_norm3,
                                (0,), *w_mix32, conv_w, (0,), _weight(conv_w_out, (0,)),
                                state_conv[0], t_s, m_s, MIX_COL_TILE)
    conv_zero = jnp.zeros((n_p, CONV_WIDTH - 1, d), F32)
    h, conv_p, _ = _mixer(Rows(h, 0), m_p, Dest(m_all, 0, IN_PLACE), mix_norm3, (0,),
                          *w_mix16[:3], conv_w, (0,), w_mix16[3], conv_zero, t_p,
                          PROMPT_MIX_ROW_TILE, PROMPT_MIX_COL_TILE, PROMPT_MIX_PARTS)

    h = ffn(*both(h), 0, 1)

    proj_blocks = d // CAST_COL_TILE
    (k_s, v_s, k16_s, v16_s), w_kv16 = _kv_proj(
        Rows(h, s_first), m_s, kv_norm2, _weight(w_kv), _weight(w_kv, (), proj_blocks), k_gain2,
        t_s, m_s, v_transposed=False)
    (k_p, v_p, k16_p, v16t_p), _ = _kv_proj(
        Rows(h, 0), m_p, kv_norm2, *w_kv16, k_gain2, t_p, tm_p, v_transposed=True,
        n_parts=PROMPT_PROJ_PARTS)

    h = ffn(*both(h), 1, 0)

    q16_s, w_q16 = _q_proj(Rows(h, s_first), m_s, mix_norm3, (1,), _weight(w_q, (0,)),
                           q_gain3, (0,), m_s)
    q16_p, _ = _q_proj(Rows(h, 0), m_p, mix_norm3, (1,), *w_q16, q_gain3, (0,), PROMPT_QO_ROW_TILE)
    att_s = _attn_sample(q16_s, k16_s, v16_s, cache_k, cache_v, rel_bias[0], n_s, t_s)
    att_p = _attn_prompt(q16_p, k16_p, v16t_p, rel_bias[0], n_p, t_p)
    h, w_o16 = _o_proj(Rows(h, s_first), m_s, Dest(m_all, s_first, IN_PLACE), att_s,
                       _weight(w_o, (0,)), m_s, att_transposed=False)
    h, _ = _o_proj(Rows(h, 0), m_p, Dest(m_all, 0, IN_PLACE), att_p, *w_o16, PROMPT_QO_ROW_TILE,
                   att_transposed=True)

    y_p, y_s = ffn(*both(h), 1, 1, final=True)

    keep_p, keep_s = min(PAST_ROWS, t_p), min(PAST_ROWS, t_s)
    heads = (N_HEADS, HEAD_DIM)
    return (y_p.reshape(n_p, t_p, d), y_s.reshape(n_s, t_s, d), conv_p[None],
            k_p.reshape(n_p, keep_p, *heads), v_p.reshape(n_p, keep_p, *heads), conv_s[None],
            k_s.reshape(n_s, keep_s, *heads), v_s.reshape(n_s, keep_s, *heads))
```
